```python
import math
import jax
import jax.numpy as jnp
from jax import lax
import numpy as np

D_MODEL = 1024
BATCH = 16
SEQ = 2048
DEPTH = 1
DEC_BATCH = 128
DEC_SEQ = 4
PAST_LEN = 8192
PAGE_SIZE = 128

NSA_HEADS = 8
NSA_KV_HEADS = 2
NSA_HEAD_DIM = 64
NSA_GROUP = NSA_HEADS // NSA_KV_HEADS
CMP_STRIDE = 16
CMP_BLOCK = 2 * CMP_STRIDE
CMP_HIDDEN = 256
SLC_BLOCK = 64
SLC_TOPK = 16
WINDOW = 512
NSA_QBLOCK = 64
SEL_FORCED = 1.0e4
ROPE_THETA = 10000.0
NSA_Q_W = NSA_HEADS * NSA_HEAD_DIM
NSA_KV_W = NSA_KV_HEADS * NSA_HEAD_DIM
HG_HEADS = 4
HG_DK = 128
HG_DV = 128
HG_CHUNK = 64
HG_W = HG_HEADS * HG_DK
HG_VW = HG_HEADS * HG_DV
N_EXPERTS = 32
TOP_K = 4
D_FF = D_MODEL
SWIGLU_LIMIT = 7.0
SWIGLU_ALPHA = 1.702
MOE_BLOCK = 256
LN_EPS = 1e-5
RMS_EPS = 1e-6
DN_ALPHA = (2 * DEPTH) ** 0.25
DN_BETA = (8 * DEPTH) ** -0.25
SPLITS = (NSA_Q_W, 6 * NSA_KV_W, 3 * NSA_HEADS, HG_W, HG_W, HG_VW, HG_VW, D_MODEL, D_MODEL)
D_IN = sum(SPLITS)

kernel_name = "nsa_hgrn2_gated_merge_moe_decoder_step"


def _layernorm(x, g, b):
    xf = x.astype(jnp.float32)
    mu = jnp.mean(xf, axis=-1, keepdims=True)
    var = jnp.mean(jnp.square(xf - mu), axis=-1, keepdims=True)
    return ((xf - mu) * lax.rsqrt(var + LN_EPS) * g + b).astype(x.dtype)


def _rope(x, pos):
    half = x.shape[-1] // 2
    inv = ROPE_THETA ** (-jnp.arange(half, dtype=jnp.float32) / half)
    ang = pos.astype(jnp.float32)[:, None] * inv[None, :]
    cos = jnp.cos(ang)[None, :, None, :]
    sin = jnp.sin(ang)[None, :, None, :]
    x1 = x[..., :half].astype(jnp.float32)
    x2 = x[..., half:].astype(jnp.float32)
    return jnp.concatenate([x1 * cos - x2 * sin, x2 * cos + x1 * sin], axis=-1).astype(x.dtype)


def _masked_softmax(s, mask, axis):
    s = jnp.where(mask, s.astype(jnp.float32), -jnp.inf)
    m = jnp.max(s, axis=axis, keepdims=True)
    m = jnp.where(jnp.isfinite(m), m, 0.0)
    e = jnp.exp(s - m)
    d = jnp.sum(e, axis=axis, keepdims=True)
    return e / jnp.where(d > 0, d, 1.0)


def _compress(rows, pe, w1, w2):
    B, T, G, Dh = rows.shape
    nfull = T // CMP_STRIDE
    ch = rows[:, :nfull * CMP_STRIDE].reshape(B, nfull, CMP_STRIDE, G, Dh)
    blk = jnp.concatenate([ch[:, :-1], ch[:, 1:]], axis=2) + pe[None, None, :, None, :]
    flat = blk.transpose(0, 1, 3, 2, 4).reshape(B, nfull - 1, G, CMP_BLOCK * Dh)
    return jax.nn.gelu(flat @ w1) @ w2


def _compress_kv(cmp_rows, cmp_params):
    pe_ck, w_ck1, w_ck2, pe_cv, w_cv1, w_cv2 = cmp_params
    ck = _compress(cmp_rows[:, :, 0], pe_ck, w_ck1, w_ck2)
    cv = _compress(cmp_rows[:, :, 1], pe_cv, w_cv1, w_cv2)
    return ck, cv


def _nsa_attend(q, qpos, ck, cv, gather_sel, nsb, wkv, wpos, gates):
    B, Tq = q.shape[:2]
    G, Hg, Dh = NSA_KV_HEADS, NSA_GROUP, NSA_HEAD_DIM
    scale = Dh ** -0.5
    qn = q.reshape(B, Tq, G, Hg, Dh)
    qr = _rope(q, qpos).reshape(B, Tq, G, Hg, Dh)
    nc = ck.shape[1]
    cidx = jnp.arange(nc, dtype=jnp.int32)
    cmask = (cidx[None, :] * CMP_STRIDE + CMP_BLOCK - 1 <= qpos[:, None])[None, :, None, None, :]
    pc = _masked_softmax(jnp.einsum('btghd,bcgd->btghc', qn, ck) * scale, cmask, -1)
    o_cmp = jnp.einsum('btghc,bcgd->btghd', pc, cv.astype(jnp.float32))
    sidx = jnp.arange(nsb, dtype=jnp.int32)
    cover = ((cidx[:, None] * CMP_STRIDE < (sidx[None, :] + 1) * SLC_BLOCK)
             & (cidx[:, None] * CMP_STRIDE + CMP_BLOCK > sidx[None, :] * SLC_BLOCK)).astype(jnp.float32)
    imp = jnp.einsum('btgc,cs->btgs', jnp.sum(pc, axis=3), cover)
    cur = (qpos // SLC_BLOCK)[:, None]
    valid = (sidx[None, :] * SLC_BLOCK <= qpos[:, None])[None, :, None, :]
    forced = ((sidx[None, :] == 0) | (sidx[None, :] == cur) | (sidx[None, :] == cur - 1))[None, :, None, :]
    score = jnp.where(forced, SEL_FORCED, jnp.where(valid, imp, -jnp.inf))
    _, sel = lax.top_k(score, min(SLC_TOPK, nsb))
    sel = sel.transpose(0, 2, 1, 3)
    kv_sel = gather_sel(sel)
    kpos = sel[..., None] * SLC_BLOCK + jnp.arange(SLC_BLOCK, dtype=jnp.int32)
    smask = (kpos <= qpos[None, None, :, None, None])[:, :, :, None]
    ps = _masked_softmax(jnp.einsum('btghd,bgtsld->bgthsl', qr, kv_sel[..., 0, :]) * scale, smask, (-2, -1))
    o_slc = jnp.einsum('bgthsl,bgtsld->btghd', ps, kv_sel[..., 1, :].astype(jnp.float32))
    dpos = qpos[:, None] - wpos[None, :]
    wmask = ((dpos >= 0) & (dpos < WINDOW) & (wpos[None, :] >= 0))[None, :, None, None, :]
    pw = _masked_softmax(jnp.einsum('btghd,blgd->btghl', qr, wkv[:, :, 0]) * scale, wmask, -1)
    o_win = jnp.einsum('btghl,blgd->btghd', pw, wkv[:, :, 1].astype(jnp.float32))
    g = jax.nn.sigmoid(gates.astype(jnp.float32)).reshape(B, Tq, 3, G, Hg, 1)
    o = g[:, :, 0] * o_cmp + g[:, :, 1] * o_slc + g[:, :, 2] * o_win
    return o.reshape(B, Tq, NSA_Q_W).astype(q.dtype)


def _nsa_rows(kv6, pos):
    cmp_rows = kv6[:, :, 0:2]
    slc_rows = jnp.stack([_rope(kv6[:, :, 2], pos), kv6[:, :, 3]], axis=2)
    win_rows = jnp.stack([_rope(kv6[:, :, 4], pos), kv6[:, :, 5]], axis=2)
    return cmp_rows, slc_rows, win_rows


def _nsa_prompt(q, kv6, gates, cmp_params):
    B, T = q.shape[:2]
    G, Dh = NSA_KV_HEADS, NSA_HEAD_DIM
    pos = jnp.arange(T, dtype=jnp.int32)
    cmp_rows, slc_rows, win_rows = _nsa_rows(kv6, pos)
    ck, cv = _compress_kv(cmp_rows, cmp_params)
    nsb = T // SLC_BLOCK
    slc_blocks = slc_rows.reshape(B, nsb, SLC_BLOCK, 2, G, Dh)
    bi = jnp.arange(B)[:, None, None, None]
    gi = jnp.arange(G)[None, :, None, None]

    def gather_sel(sel):
        return slc_blocks[bi, sel, :, :, gi, :]

    win_pad = jnp.pad(win_rows, ((0, 0), (WINDOW, 0), (0, 0), (0, 0), (0, 0)))
    qb = math.gcd(T, NSA_QBLOCK)
    nb = T // qb
    q_blocks = q.reshape(B, nb, qb, NSA_HEADS, Dh).transpose(1, 0, 2, 3, 4)
    g_blocks = gates.reshape(B, nb, qb, 3 * NSA_HEADS).transpose(1, 0, 2, 3)

    def body(args):
        i, qi, gblk = args
        start = i * qb
        qpos = start + jnp.arange(qb, dtype=jnp.int32)
        wkv = lax.dynamic_slice_in_dim(win_pad, start, WINDOW + qb, axis=1)
        wpos = start - WINDOW + jnp.arange(WINDOW + qb, dtype=jnp.int32)
        return _nsa_attend(qi, qpos, ck, cv, gather_sel, nsb, wkv, wpos, gblk)

    out = lax.map(body, (jnp.arange(nb, dtype=jnp.int32), q_blocks, g_blocks))
    out = out.transpose(1, 0, 2, 3).reshape(B, T, NSA_Q_W)
    keep = min(WINDOW, T)
    return out, cmp_rows, slc_rows, win_rows[:, T - keep:]


def _nsa_sample(q, kv6, gates, cmp_params, cache_cmp_kv, cache_slc_kv, win_buf, page_table, layer):
    B, T = q.shape[:2]
    G, Dh = NSA_KV_HEADS, NSA_HEAD_DIM
    page = cache_cmp_kv.shape[2]
    past = page_table.shape[1] * page
    pos = past + jnp.arange(T, dtype=jnp.int32)
    cmp_rows, slc_rows, win_rows = _nsa_rows(kv6, pos)
    past_cmp = cache_cmp_kv[layer, page_table].reshape(B, past, 2, G, Dh)
    ck, cv = _compress_kv(jnp.concatenate([past_cmp, cmp_rows], axis=1), cmp_params)
    nsb = -(-(past + T) // SLC_BLOCK)
    npb = past // SLC_BLOCK
    n_new = nsb - npb
    r = page // SLC_BLOCK
    new_blocks = jnp.pad(slc_rows, ((0, 0), (0, n_new * SLC_BLOCK - T), (0, 0), (0, 0), (0, 0)))
    new_blocks = new_blocks.reshape(B, n_new, SLC_BLOCK, 2, G, Dh)
    pool_blocks = cache_slc_kv.reshape(cache_slc_kv.shape[0], -1, SLC_BLOCK, 2, G, Dh)
    bi = jnp.arange(B)[:, None, None, None]
    gi = jnp.arange(G)[None, :, None, None]

    def gather_sel(sel):
        jp = jnp.minimum(sel, npb - 1)
        phys = page_table[bi, jp // r] * r + jp % r
        from_pool = pool_blocks[layer, phys, :, :, gi, :]
        from_new = new_blocks[bi, jnp.clip(sel - npb, 0, n_new - 1), :, :, gi, :]
        return jnp.where((sel >= npb)[..., None, None, None], from_new, from_pool)

    wb = win_buf.shape[1]
    wkv = jnp.concatenate([win_buf, win_rows], axis=1)
    wpos = past - wb + jnp.arange(wb + T, dtype=jnp.int32)
    out = _nsa_attend(q, pos, ck, cv, gather_sel, nsb, wkv, wpos, gates)
    return out, cmp_rows, slc_rows, wkv[:, T:]


def _hgrn2(hq, hf, hi, hg, lb, norm_w, s0):
    B, T, H, Dk = hq.shape
    Dv = hi.shape[-1]
    q = jax.nn.silu(hq.astype(jnp.float32))
    f = lb + (1.0 - lb) * jax.nn.sigmoid(hf.astype(jnp.float32))
    logf = jnp.log(f)
    k = 1.0 - f
    iv = hi.astype(jnp.float32)
    c = math.gcd(T, HG_CHUNK)
    n = T // c

    def to_chunks(a):
        return a.reshape(B, n, c, H, a.shape[-1]).transpose(1, 0, 3, 2, 4)

    causal = jnp.tril(jnp.ones((c, c), dtype=bool))[:, :, None]

    def step(S, xs):
        qc, kc, lc, ic = xs
        b = jnp.cumsum(lc, axis=2)
        diff = b[:, :, :, None, :] - b[:, :, None, :, :]
        dec = jnp.exp(jnp.where(causal, diff, -jnp.inf))
        att = jnp.einsum('bhtk,bhsk,bhtsk->bhts', qc, kc, dec)
        o = att @ ic + jnp.einsum('bhtk,bhkv->bhtv', qc * jnp.exp(b), S)
        bl = b[:, :, -1, :]
        S = jnp.exp(bl)[..., None] * S + jnp.einsum('bhsk,bhsv->bhkv', kc * jnp.exp(bl[:, :, None, :] - b), ic)
        return S, o

    S, o = lax.scan(step, s0.astype(jnp.float32), (to_chunks(q), to_chunks(k), to_chunks(logf), to_chunks(iv)))
    o = o.transpose(1, 0, 3, 2, 4).reshape(B, T, H, Dv)
    o = o * lax.rsqrt(jnp.mean(jnp.square(o), axis=-1, keepdims=True) + RMS_EPS) * norm_w
    o = o * jax.nn.silu(hg.astype(jnp.float32))
    return o.reshape(B, T, H * Dv).astype(hq.dtype), S.astype(s0.dtype)


def _moe(x, w_router, b_router, w_gu, b_gu, w_dn, b_dn):
    shp = x.shape
    xt = x.reshape(-1, shp[-1])
    N = xt.shape[0]
    logits = (xt @ w_router + b_router).astype(jnp.float32)
    top_v, top_e = lax.top_k(logits, TOP_K)
    gate = jax.nn.softmax(top_v, axis=-1)
    A = N * TOP_K
    e_flat = top_e.reshape(-1)
    tok_flat = jnp.arange(A, dtype=jnp.int32) // TOP_K
    order = jnp.argsort(e_flat)
    e_s, tok_s, w_s = e_flat[order], tok_flat[order], gate.reshape(-1)[order]
    blk = max(8, min(MOE_BLOCK, A // N_EXPERTS))
    nblk = A // blk + N_EXPERTS
    counts = jnp.bincount(e_flat, length=N_EXPERTS)
    pcounts = (counts + blk - 1) // blk * blk
    pend = jnp.cumsum(pcounts)
    pstart = pend - pcounts
    start = jnp.cumsum(counts) - counts
    dest = pstart[e_s] + jnp.arange(A, dtype=jnp.int32) - start[e_s]
    xs = jnp.zeros((nblk * blk, shp[-1]), x.dtype).at[dest].set(xt[tok_s])
    bexp = jnp.minimum(jnp.sum(jnp.arange(nblk)[:, None] * blk >= pend[None, :], axis=1), N_EXPERTS - 1)

    def expert_block(args):
        xb, e = args
        h = xb @ w_gu[e] + b_gu[e]
        gt = jnp.minimum(h[:, :D_FF], SWIGLU_LIMIT)
        up = jnp.clip(h[:, D_FF:], -SWIGLU_LIMIT, SWIGLU_LIMIT)
        return ((up + 1.0) * (gt * jax.nn.sigmoid(SWIGLU_ALPHA * gt))) @ w_dn[e] + b_dn[e]

    ys = lax.map(expert_block, (xs.reshape(nblk, blk, shp[-1]), bexp)).reshape(nblk * blk, shp[-1])
    y = jnp.zeros((N, shp[-1]), jnp.float32).at[tok_s].add(w_s[:, None] * ys[dest].astype(jnp.float32))
    return y.astype(x.dtype).reshape(shp)


def _project_in(x, w_in):
    B, T, _ = x.shape
    points = [int(v) for v in np.cumsum(SPLITS)[:-1]]
    q, kv, ng, hq, hf, hi, hg, ga, gb = jnp.split(x @ w_in, points, axis=-1)
    return (q.reshape(B, T, NSA_HEADS, NSA_HEAD_DIM),
            kv.reshape(B, T, 6, NSA_KV_HEADS, NSA_HEAD_DIM), ng,
            hq.reshape(B, T, HG_HEADS, HG_DK), hf.reshape(B, T, HG_HEADS, HG_DK),
            hi.reshape(B, T, HG_HEADS, HG_DV), hg.reshape(B, T, HG_HEADS, HG_DV), ga, gb)


def _finish(x, nsa_o, hg_o, ga, gb, w_pa, w_pb, w_o, ln1_g, ln1_b, moe_w, ln2_g, ln2_b):
    mix = (jax.nn.sigmoid(ga) * (nsa_o @ w_pa) + jax.nn.sigmoid(gb) * (hg_o @ w_pb)) @ w_o
    h = _layernorm(DN_ALPHA * x + mix, ln1_g, ln1_b)
    return _layernorm(DN_ALPHA * h + _moe(h, *moe_w), ln2_g, ln2_b)


def setup_inputs(seed: int = 0) -> dict:
    key = jax.random.key(seed)
    ks = jax.random.split(key, 40)
    f32 = jnp.float32
    n_pages = PAST_LEN // PAGE_SIZE
    n_used = DEC_BATCH * n_pages
    n_pool = (5 * n_used + 3) // 4
    wbuf = min(WINDOW, PAST_LEN)
    G, Dh = NSA_KV_HEADS, NSA_HEAD_DIM

    def nrm(k, shape, scale):
        return jax.random.normal(k, shape, f32) * scale

    return {
        "x_prompt": nrm(ks[0], (BATCH, SEQ, D_MODEL), 1.0),
        "x_sample": nrm(ks[1], (DEC_BATCH, DEC_SEQ, D_MODEL), 1.0),
        "cache_cmp_kv": nrm(ks[2], (DEPTH, n_pool, PAGE_SIZE, 2, G, Dh), 1.0),
        "cache_slc_kv": nrm(ks[3], (DEPTH, n_pool, PAGE_SIZE, 2, G, Dh), 1.0),
        "cache_win_kv": nrm(ks[4], (DEPTH, DEC_BATCH, wbuf, 2, G, Dh), 1.0),
        "state_hgrn": nrm(ks[5], (DEPTH, DEC_BATCH, HG_HEADS, HG_DK, HG_DV), 1.0),
        "page_table": jax.random.permutation(ks[6], n_pool)[:n_used].reshape(DEC_BATCH, n_pages).astype(jnp.int32),
        "w_in": nrm(ks[7], (DEPTH, D_MODEL, D_IN), D_MODEL ** -0.5),
        "pe_ck": nrm(ks[8], (DEPTH, CMP_BLOCK, Dh), 0.5),
        "w_ck1": nrm(ks[9], (DEPTH, CMP_BLOCK * Dh, CMP_HIDDEN), (CMP_BLOCK * Dh) ** -0.5),
        "w_ck2": nrm(ks[10], (DEPTH, CMP_HIDDEN, Dh), CMP_HIDDEN ** -0.5),
        "pe_cv": nrm(ks[11], (DEPTH, CMP_BLOCK, Dh), 0.5),
        "w_cv1": nrm(ks[12], (DEPTH, CMP_BLOCK * Dh, CMP_HIDDEN), (CMP_BLOCK * Dh) ** -0.5),
        "w_cv2": nrm(ks[13], (DEPTH, CMP_HIDDEN, Dh), CMP_HIDDEN ** -0.5),
        "hg_lb_logits": nrm(ks[14], (DEPTH + 1, HG_W), 0.5),
        "hg_norm_w": 1.0 + nrm(ks[15], (DEPTH, HG_DV), 0.02),
        "w_pa": nrm(ks[16], (DEPTH, NSA_Q_W, D_MODEL), NSA_Q_W ** -0.5),
        "w_pb": nrm(ks[17], (DEPTH, HG_VW, D_MODEL), HG_VW ** -0.5),
        "w_o": nrm(ks[18], (DEPTH, D_MODEL, D_MODEL), DN_BETA * D_MODEL ** -0.5),
        "ln1_g": 1.0 + nrm(ks[19], (DEPTH, D_MODEL), 0.02),
        "ln1_b": nrm(ks[20], (DEPTH, D_MODEL), 0.02),
        "w_router": nrm(ks[21], (DEPTH, D_MODEL, N_EXPERTS), D_MODEL ** -0.5),
        "b_router": nrm(ks[22], (DEPTH, N_EXPERTS), 0.01),
        "w_gu": nrm(ks[23], (DEPTH, N_EXPERTS, D_MODEL, 2 * D_FF), D_MODEL ** -0.5),
        "b_gu": nrm(ks[24], (DEPTH, N_EXPERTS, 2 * D_FF), 0.01),
        "w_dn": nrm(ks[25], (DEPTH, N_EXPERTS, D_FF, D_MODEL), DN_BETA * D_FF ** -0.5),
        "b_dn": nrm(ks[26], (DEPTH, N_EXPERTS, D_MODEL), 0.01),
        "ln2_g": 1.0 + nrm(ks[27], (DEPTH, D_MODEL), 0.02),
        "ln2_b": nrm(ks[28], (DEPTH, D_MODEL), 0.02),
    }


def reference(x_prompt, x_sample, cache_cmp_kv, cache_slc_kv, cache_win_kv, state_hgrn, page_table,
              w_in, pe_ck, w_ck1, w_ck2, pe_cv, w_cv1, w_cv2, hg_lb_logits, hg_norm_w,
              w_pa, w_pb, w_o, ln1_g, ln1_b, w_router, b_router, w_gu, b_gu, w_dn, b_dn, ln2_g, ln2_b):
    lb_all = jnp.cumsum(jax.nn.softmax(hg_lb_logits.astype(jnp.float32), axis=0), axis=0)
    hp, hs = x_prompt, x_sample
    cmp_p, cmp_s, slc_p, slc_s, win_p, win_s, hg_p, hg_s = [], [], [], [], [], [], [], []
    for l in range(DEPTH):
        cmp_params = (pe_ck[l], w_ck1[l], w_ck2[l], pe_cv[l], w_cv1[l], w_cv2[l])
        lb = lb_all[l].reshape(HG_HEADS, HG_DK)
        moe_w = (w_router[l], b_router[l], w_gu[l], b_gu[l], w_dn[l], b_dn[l])
        q, kv6, ng, hq, hf, hi, hg, ga, gb = _project_in(hp, w_in[l])
        nsa_o, c_rows, s_rows, w_rows = _nsa_prompt(q, kv6, ng, cmp_params)
        s0 = jnp.zeros((hp.shape[0], HG_HEADS, HG_DK, HG_DV), hp.dtype)
        hg_o, s_new = _hgrn2(hq, hf, hi, hg, lb, hg_norm_w[l], s0)
        hp = _finish(hp, nsa_o, hg_o, ga, gb, w_pa[l], w_pb[l], w_o[l], ln1_g[l], ln1_b[l], moe_w, ln2_g[l], ln2_b[l])
        cmp_p.append(c_rows)
        slc_p.append(s_rows)
        win_p.append(w_rows)
        hg_p.append(s_new)
        q, kv6, ng, hq, hf, hi, hg, ga, gb = _project_in(hs, w_in[l])
        nsa_o, c_rows, s_rows, w_state = _nsa_sample(q, kv6, ng, cmp_params, cache_cmp_kv, cache_slc_kv,
                                                     cache_win_kv[l], page_table, l)
        hg_o, s_new = _hgrn2(hq, hf, hi, hg, lb, hg_norm_w[l], state_hgrn[l])
        hs = _finish(hs, nsa_o, hg_o, ga, gb, w_pa[l], w_pb[l], w_o[l], ln1_g[l], ln1_b[l], moe_w, ln2_g[l], ln2_b[l])
        cmp_s.append(c_rows)
        slc_s.append(s_rows)
        win_s.append(w_state)
        hg_s.append(s_new)
    return (hp, hs, jnp.stack(cmp_p), jnp.stack(cmp_s), jnp.stack(slc_p), jnp.stack(slc_s),
            jnp.stack(win_p), jnp.stack(win_s), jnp.stack(hg_p), jnp.stack(hg_s))
```

```python
import functools
import math

import numpy as np
import jax
import jax.numpy as jnp
from jax import lax
from jax.experimental import pallas as pl
from jax.experimental.pallas import tpu as pltpu

F32 = jnp.float32
BF16 = jnp.bfloat16
I32 = jnp.int32

D_MODEL = 1024
NSA_HEADS = 8
NSA_G = 2
NSA_HG = 4
DH = 64
CMP_STRIDE = 16
CMP_HIDDEN = 256
SLC_BLOCK = 64
SLC_TOPK = 16
WINDOW = 512
SEL_FORCED = 1.0e4
ROPE_THETA = 10000.0
HG_HEADS = 4
HG_D = 128
N_EXPERTS = 32
TOP_K = 4
D_FF = 1024
SWIGLU_LIMIT = 7.0
SWIGLU_ALPHA = 1.702
LN_EPS = 1e-5
RMS_EPS = 1e-6
DN_ALPHA = 2.0 ** 0.25

LANES = 128
TM = 256
TQ = 128
SROWS = 8
HG_CHUNK = 16
TE = 256
TD = 128
NEG = -1.0e30
VMEM_LIMIT = 56 * 1024 * 1024

_C_Q, _C_KV, _C_PK, _C_NG, _C_HH, _C_GG, _C_END = 0, 1024, 1792, 2816, 3072, 5120, 7168


def _sigmoid(x):
    return 1.0 / (1.0 + jnp.exp(-x))


def _cparams(sem, vmem=VMEM_LIMIT):
    return pltpu.CompilerParams(dimension_semantics=sem, vmem_limit_bytes=vmem)


def _repack_w_in(w_in):
    d = w_in.shape[0]
    z64 = jnp.zeros((d, DH), F32)
    q = w_in[:, :512].reshape(d, NSA_HEADS, DH) * (DH ** -0.5)
    qpad = jnp.concatenate([q, jnp.zeros_like(q)], axis=-1).reshape(d, NSA_HEADS * LANES)
    kv = w_in[:, 512:1280]
    kv6 = kv.reshape(d, 6, NSA_G, DH)
    packs = []
    for kidx, vidx in ((2, 3), (4, 5)):
        for g in range(NSA_G):
            packs += [kv6[:, kidx, g], z64, kv6[:, vidx, g], kv6[:, vidx, g]]
    ng = w_in[:, 1280:1304].reshape(d, 3, NSA_G, NSA_HG)
    ngp = []
    for g in range(NSA_G):
        ngp += [ng[:, :, g, :].reshape(d, 3 * NSA_HG), jnp.zeros((d, LANES - 3 * NSA_HG), F32)]
    w = jnp.concatenate([qpad, kv] + packs + ngp + [w_in[:, 1304:3352], w_in[:, 3352:5400]], axis=-1)
    assert w.shape[1] == _C_END
    return w.astype(BF16)


def _rope_tables(t_prompt, past):
    half = DH // 2
    inv = ROPE_THETA ** (-jnp.arange(half, dtype=F32) / half)
    pos = jnp.concatenate([jnp.arange(t_prompt, dtype=I32), past + (jnp.arange(TM, dtype=I32) % SROWS)])
    ang = pos.astype(F32)[:, None] * inv[None, :]
    cos, sin = jnp.cos(ang), jnp.sin(ang)
    cos64 = jnp.concatenate([cos, cos], axis=-1)
    sin64 = jnp.concatenate([-sin, sin], axis=-1)
    one, zero = jnp.ones_like(cos64), jnp.zeros_like(cos64)
    cos_a = jnp.concatenate([cos64, cos64], axis=-1)
    sin_a = jnp.concatenate([sin64, sin64], axis=-1)
    cos_b = jnp.concatenate([cos64, one], axis=-1)
    sin_b = jnp.concatenate([sin64, zero], axis=-1)
    lane = jnp.arange(LANES, dtype=I32)[None, :]
    eadd = ((lane >= DH) & (lane < DH + 32) & (lane - DH == pos[:, None] // SLC_BLOCK)).astype(F32)
    return cos_a, sin_a, cos_b, sin_b, eadd


def _proj_kernel(x_ref, w_ref, ca_ref, sa_ref, cb_ref, sb_ref, ea_ref,
                 qn_ref, qr_ref, kv_ref, kvp_ref, ng_ref, hh_ref, gg_ref):
    xb = x_ref[...].astype(BF16)
    tm = xb.shape[0]
    lane = lax.broadcasted_iota(I32, (tm, LANES), 1)
    first = (lane % DH) < (DH // 2)
    ca, sa, cb, sb, ea = ca_ref[...], sa_ref[...], cb_ref[...], sb_ref[...], ea_ref[...]

    def rope(a, c, s):
        partner = jnp.where(first, pltpu.roll(a, LANES - DH // 2, 1), pltpu.roll(a, DH // 2, 1))
        return a * c + partner * s

    def mm(c0, n):
        return jnp.dot(xb, w_ref[:, c0:c0 + n], preferred_element_type=F32)

    for c in range(2):
        acc = mm(_C_Q + c * 512, 512)
        for j in range(4):
            a = acc[:, j * LANES:(j + 1) * LANES]
            col = c * 512 + j * LANES
            qn_ref[:, col:col + LANES] = a.astype(BF16)
            qr_ref[:, col:col + LANES] = rope(a, cb, sb).astype(BF16)
    acc = mm(_C_KV, 768)
    for j in range(6):
        a = acc[:, j * LANES:(j + 1) * LANES]
        if j in (2, 4):
            a = rope(a, ca, sa)
        kv_ref[:, j * LANES:(j + 1) * LANES] = a
    for c in range(2):
        acc = mm(_C_PK + c * 512, 512)
        for g in range(NSA_G):
            k = rope(acc[:, g * 256:g * 256 + LANES], cb, sb)
            if c == 0:
                k = k + ea
            col = c * 512 + g * 256
            kvp_ref[:, col:col + LANES] = k.astype(BF16)
            kvp_ref[:, col + LANES:col + 2 * LANES] = acc[:, g * 256 + LANES:g * 256 + 2 * LANES].astype(BF16)
    ng_ref[...] = mm(_C_NG, 256)
    for c in range(4):
        hh_ref[:, c * 512:(c + 1) * 512] = mm(_C_HH + c * 512, 512)
    for c in range(4):
        gg_ref[:, c * 512:(c + 1) * 512] = _sigmoid(mm(_C_GG + c * 512, 512))


def _project(x, w_all, tables, n_prompt_tiles, t_prompt):
    n = x.shape[0]
    tp = t_prompt // TM

    def tab_map(i):
        return (jnp.where(i < n_prompt_tiles, i % tp, tp), 0)

    row = lambda i: (i, 0)
    tab_spec = pl.BlockSpec((TM, LANES), tab_map)
    widths = (1024, 1024, 768, 1024, 256, 2048, 2048)
    dtypes = (BF16, BF16, F32, BF16, F32, F32, F32)
    return pl.pallas_call(
        _proj_kernel,
        grid=(n // TM,),
        in_specs=[pl.BlockSpec((TM, D_MODEL), row),
                  pl.BlockSpec((D_MODEL, _C_END), lambda i: (0, 0))] + [tab_spec] * 5,
        out_specs=[pl.BlockSpec((TM, w), row) for w in widths],
        out_shape=[jax.ShapeDtypeStruct((n, w), dt) for w, dt in zip(widths, dtypes)],
        compiler_params=_cparams(("arbitrary",)),
        name="proj",
    )(x, w_all, *tables)


def _gelu_tanh(x):
    return x * (0.5 * (1.0 + jnp.tanh(math.sqrt(2.0 / math.pi) * (x + 0.044715 * (x * x * x)))))


def _compress_core(xk_ref, xv_ref, z_ref, w1k_ref, w1v_ref, pek_ref, pev_ref, w2k_ref, w2v_ref, out_ref):
    nch = xk_ref.shape[0] // CMP_STRIDE
    lane = lax.broadcasted_iota(I32, (nch, LANES), 1)
    lo = lane < DH
    for p, xr in enumerate((xk_ref, xv_ref)):
        for r2 in range(CMP_STRIDE // 2):
            a = xr[pl.ds(2 * r2, nch, stride=CMP_STRIDE), :]
            b = xr[pl.ds(2 * r2 + 1, nch, stride=CMP_STRIDE), :]
            z0 = jnp.where(lo, a, pltpu.roll(b, DH, 1))
            z1 = jnp.where(lo, pltpu.roll(a, DH, 1), b)
            z_ref[2 * p, :, r2 * LANES:(r2 + 1) * LANES] = z0.astype(BF16)
            z_ref[2 * p + 1, :, r2 * LANES:(r2 + 1) * LANES] = z1.astype(BF16)
    for s in range(4):
        w1, pe, w2 = (w1k_ref, pek_ref, w2k_ref) if s < 2 else (w1v_ref, pev_ref, w2v_ref)
        pq = jnp.dot(z_ref[s], w1[...], preferred_element_type=F32)
        pb = jnp.dot(pe[...], w1[...], preferred_element_type=F32)
        bias = pb[0:1, :CMP_HIDDEN] + pb[8:9, CMP_HIDDEN:]
        h = pq[:, :CMP_HIDDEN] + pltpu.roll(pq[:, CMP_HIDDEN:], nch - 1, 0) + bias
        out_ref[s] = jnp.dot(_gelu_tanh(h).astype(BF16), w2[...], preferred_element_type=F32)


def _compress_prompt_kernel(xk_ref, xv_ref, w1k_ref, w1v_ref, pek_ref, pev_ref, w2k_ref, w2v_ref, out_ref, z_ref):
    _compress_core(xk_ref, xv_ref, z_ref, w1k_ref, w1v_ref, pek_ref, pev_ref, w2k_ref, w2v_ref, out_ref.at[0])


def _compress_sample_kernel(pt_ref, cache_ref, w1k_ref, w1v_ref, pek_ref, pev_ref, w2k_ref, w2v_ref,
                            out_ref, buf_ref, z_ref, sem):
    b = pl.program_id(0)
    nb = pl.num_programs(0)
    npages = pt_ref.shape[1]
    page = buf_ref.shape[2] // npages

    def copies(bb, slot):
        out = []
        for p in range(npages):
            pg = pt_ref[bb, p]
            for half in range(2):
                out.append(pltpu.make_async_copy(
                    cache_ref.at[pl.ds(pg * page, page), pl.ds(half * LANES, LANES)],
                    buf_ref.at[slot, half, pl.ds(p * page, page), :], sem.at[slot]))
        return out

    @pl.when(b == 0)
    def _():
        for c in copies(0, 0):
            c.start()

    @pl.when(b + 1 < nb)
    def _():
        for c in copies(b + 1, (b + 1) % 2):
            c.start()

    slot = b % 2
    for c in copies(b, slot):
        c.wait()
    _compress_core(buf_ref.at[slot, 0], buf_ref.at[slot, 1], z_ref, w1k_ref, w1v_ref, pek_ref, pev_ref,
                   w2k_ref, w2v_ref, out_ref.at[0])


def _compress_weights(pe, w1, w2):
    w1cat = jnp.concatenate([w1[:CMP_STRIDE * DH], w1[CMP_STRIDE * DH:]], axis=1).astype(BF16)
    pef = pe.reshape(2, 1, CMP_STRIDE * DH)
    pe16 = jnp.broadcast_to(pef, (2, 8, CMP_STRIDE * DH)).reshape(16, CMP_STRIDE * DH).astype(BF16)
    w2cat = jnp.concatenate([w2, w2], axis=1).astype(BF16)
    return w1cat, pe16, w2cat


def _wspecs(n_grid_args):
    const = (lambda *a: (0, 0))
    return [pl.BlockSpec((CMP_STRIDE * DH, 2 * CMP_HIDDEN), const), pl.BlockSpec((CMP_STRIDE * DH, 2 * CMP_HIDDEN), const),
            pl.BlockSpec((16, CMP_STRIDE * DH), const), pl.BlockSpec((16, CMP_STRIDE * DH), const),
            pl.BlockSpec((CMP_HIDDEN, LANES), const), pl.BlockSpec((CMP_HIDDEN, LANES), const)]


def _compress_prompt(kv, nb, t, cw):
    nch = t // CMP_STRIDE
    (w1k, pek, w2k), (w1v, pev, w2v) = cw
    return pl.pallas_call(
        _compress_prompt_kernel,
        grid=(nb,),
        in_specs=[pl.BlockSpec((t, LANES), lambda b: (b, 0)), pl.BlockSpec((t, LANES), lambda b: (b, 1))] + _wspecs(1),
        out_specs=pl.BlockSpec((1, 4, nch, LANES), lambda b: (b, 0, 0, 0)),
        out_shape=jax.ShapeDtypeStruct((nb, 4, nch, LANES), F32),
        scratch_shapes=[pltpu.VMEM((4, nch, CMP_STRIDE * DH), BF16)],
        compiler_params=_cparams(("arbitrary",)),
        name="compress_prompt",
    )(kv, kv, w1k, w1v, pek, pev, w2k, w2v)


def _compress_sample(cache2d, page_table, page, cw):
    nb, npages = page_table.shape
    r = npages * page
    nch = r // CMP_STRIDE
    (w1k, pek, w2k), (w1v, pev, w2v) = cw
    wspecs = [pl.BlockSpec(s.block_shape, lambda b, pt: (0, 0)) for s in _wspecs(2)]
    return pl.pallas_call(
        _compress_sample_kernel,
        grid_spec=pltpu.PrefetchScalarGridSpec(
            num_scalar_prefetch=1,
            grid=(nb,),
            in_specs=[pl.BlockSpec(memory_space=pl.ANY)] + wspecs,
            out_specs=pl.BlockSpec((1, 4, nch, LANES), lambda b, pt: (b, 0, 0, 0)),
            scratch_shapes=[pltpu.VMEM((2, 2, r, LANES), F32), pltpu.VMEM((4, nch, CMP_STRIDE * DH), BF16),
                            pltpu.SemaphoreType.DMA((2,))]),
        out_shape=jax.ShapeDtypeStruct((nb, 4, nch, LANES), F32),
        compiler_params=_cparams(("arbitrary",)),
        name="compress_sample",
    )(page_table, cache2d, w1k, w1v, pek, pev, w2k, w2v)


def _softmax_update(s, m, l, acc, v):
    m_new = jnp.maximum(m, jnp.max(s, axis=1, keepdims=True))
    alpha = jnp.exp(m - m_new)
    p = jnp.exp(s - m_new)
    l = alpha * l + jnp.sum(p, axis=1, keepdims=True)
    acc = alpha * acc + jnp.dot(p.astype(BF16), v, preferred_element_type=F32)
    return m_new, l, acc


def _topk_mask(score, sidx, in_sel, lane0, nsel, k):
    rank = jnp.zeros(score.shape, I32)
    for sp in range(nsel):
        c = score[:, lane0 + sp:lane0 + sp + 1]
        beats = (c > score) | ((c == score) & (sp < sidx))
        rank = rank + beats.astype(I32)
    return in_sel & (rank < k)


def _nsa_prompt_kernel(qn_ref, qr_ref, ck_ref, cv_ref, ks_ref, kw_ref, ng_ref, cover_ref, out_ref, *, t, nsb, ksel):
    qt = pl.program_id(2)
    t0 = qt * TQ
    row = lax.broadcasted_iota(I32, (TQ, LANES), 0)
    lane = lax.broadcasted_iota(I32, (TQ, LANES), 1)
    tpos = t0 + row
    nck = ck_ref.shape[2]

    ckb = ck_ref[0, 0].astype(BF16)
    cvb = cv_ref[0, 0].astype(BF16)
    cidx = lax.broadcasted_iota(I32, (TQ, nck), 1)
    cmask = (cidx * CMP_STRIDE + 2 * CMP_STRIDE - 1 <= t0 + lax.broadcasted_iota(I32, (TQ, nck), 0)) & (cidx < nck - 1)
    psum = jnp.zeros((TQ, nck), F32)
    o_cmp = []
    for h in range(NSA_HG):
        q = qn_ref[:, h * LANES:(h + 1) * LANES]
        s = lax.dot_general(q, ckb, (((1,), (1,)), ((), ())), preferred_element_type=F32)
        s = jnp.where(cmask, s, -jnp.inf)
        m = jnp.max(s, axis=1, keepdims=True)
        m = jnp.where(m > -jnp.inf, m, 0.0)
        e = jnp.exp(s - m)
        d = jnp.sum(e, axis=1, keepdims=True)
        p = e / jnp.where(d > 0, d, 1.0)
        psum = psum + p
        o_cmp.append(jnp.dot(p.astype(BF16), cvb, preferred_element_type=F32))

    cover = cover_ref[...]
    p_hi = psum.astype(BF16)
    p_lo = (psum - p_hi.astype(F32)).astype(BF16)
    imp = jnp.dot(p_hi, cover, preferred_element_type=F32) + jnp.dot(p_lo, cover, preferred_element_type=F32)
    sidx = lane - DH
    in_sel = (sidx >= 0) & (sidx < nsb)
    cur = tpos // SLC_BLOCK
    valid = sidx * SLC_BLOCK <= tpos
    forced = (sidx == 0) | (sidx == cur) | (sidx == cur - 1)
    score = jnp.where(forced, SEL_FORCED, jnp.where(valid, imp, -jnp.inf))
    selected = _topk_mask(score, sidx, in_sel, DH, nsb, ksel)
    selb = jnp.where((sidx >= 0) & (sidx < 32) & jnp.logical_not(selected), NEG, 0.0)

    diag_ok = lane <= row
    zero_state = (jnp.full((TQ, 1), NEG, F32), jnp.zeros((TQ, 1), F32), jnp.zeros((TQ, LANES), F32))

    qs = [(qr_ref[:, h * LANES:(h + 1) * LANES].astype(F32) + selb).astype(BF16) for h in range(NSA_HG)]

    def slc_tile(j, st, diag):
        r0 = pl.multiple_of(j * TQ, TQ)
        kt = ks_ref[pl.ds(r0, TQ), 0:LANES]
        vt = ks_ref[pl.ds(r0, TQ), LANES:2 * LANES]
        new = []
        for h in range(NSA_HG):
            s = lax.dot_general(qs[h], kt, (((1,), (1,)), ((), ())), preferred_element_type=F32)
            if diag:
                s = jnp.where(diag_ok, s, NEG)
            new.append(_softmax_update(s, *st[h], vt))
        return tuple(new)

    st = lax.fori_loop(0, qt, lambda j, c: slc_tile(j, c, False), tuple(zero_state for _ in range(NSA_HG)))
    st = slc_tile(qt, st, True)
    o_slc = [acc / l for (_, l, acc) in st]

    qw = [qr_ref[:, h * LANES:(h + 1) * LANES] for h in range(NSA_HG)]
    nwin = WINDOW // TQ

    def win_tile(i, st):
        j = qt - nwin + i
        jc = jnp.maximum(j, 0)
        r0 = pl.multiple_of(jc * TQ, TQ)
        kt = kw_ref[pl.ds(r0, TQ), 0:LANES]
        vt = kw_ref[pl.ds(r0, TQ), LANES:2 * LANES]
        kpos = jc * TQ + lane
        ok = (kpos <= tpos) & (tpos - kpos < WINDOW) & (j >= 0)
        new = []
        for h in range(NSA_HG):
            s = lax.dot_general(qw[h], kt, (((1,), (1,)), ((), ())), preferred_element_type=F32)
            s = jnp.where(ok, s, NEG)
            new.append(_softmax_update(s, *st[h], vt))
        return tuple(new)

    st = lax.fori_loop(0, nwin + 1, win_tile, tuple(zero_state for _ in range(NSA_HG)))
    o_win = [acc / l for (_, l, acc) in st]

    gates = _sigmoid(ng_ref[...])
    tot = []
    for h in range(NSA_HG):
        g0 = gates[:, h:h + 1]
        g1 = gates[:, NSA_HG + h:NSA_HG + h + 1]
        g2 = gates[:, 2 * NSA_HG + h:2 * NSA_HG + h + 1]
        tot.append(g0 * o_cmp[h] + g1 * o_slc[h] + g2 * o_win[h])
    lo = lane < DH
    out_ref[:, 0:LANES] = jnp.where(lo, tot[0], tot[1]).astype(BF16)
    out_ref[:, LANES:2 * LANES] = jnp.where(lo, tot[2], tot[3]).astype(BF16)


def _cover_matrix(nc, nsb, lane0, rows, cols):
    c = np.arange(rows)[:, None]
    s = np.arange(cols)[None, :] - lane0
    cov = ((c < nc) & (s >= 0) & (s < nsb) & (c * CMP_STRIDE < (s + 1) * SLC_BLOCK)
           & (c * CMP_STRIDE + 2 * CMP_STRIDE > s * SLC_BLOCK))
    return jnp.asarray(cov.astype(np.float32), dtype=BF16)


def _nsa_prompt(qn, qr, cc, kvp, ng, nb, t):
    nqt = t // TQ
    nch = t // CMP_STRIDE
    nsb = t // SLC_BLOCK
    assert nsb <= 32
    cover = _cover_matrix(nch - 1, nsb, DH, nch, LANES)
    kern = functools.partial(_nsa_prompt_kernel, t=t, nsb=nsb, ksel=min(SLC_TOPK, nsb))
    return pl.pallas_call(
        kern,
        grid=(nb, NSA_G, nqt),
        in_specs=[pl.BlockSpec((TQ, NSA_HG * LANES), lambda b, g, q: (b * nqt + q, g)),
                  pl.BlockSpec((TQ, NSA_HG * LANES), lambda b, g, q: (b * nqt + q, g)),
                  pl.BlockSpec((1, 1, nch, LANES), lambda b, g, q: (b, g, 0, 0)),
                  pl.BlockSpec((1, 1, nch, LANES), lambda b, g, q: (b, 2 + g, 0, 0)),
                  pl.BlockSpec((t, 2 * LANES), lambda b, g, q: (b, g)),
                  pl.BlockSpec((t, 2 * LANES), lambda b, g, q: (b, 2 + g)),
                  pl.BlockSpec((TQ, LANES), lambda b, g, q: (b * nqt + q, g)),
                  pl.BlockSpec((nch, LANES), lambda b, g, q: (0, 0))],
        out_specs=pl.BlockSpec((TQ, 2 * LANES), lambda b, g, q: (b * nqt + q, g)),
        out_shape=jax.ShapeDtypeStruct((nb * t, NSA_G * 2 * LANES), BF16),
        compiler_params=_cparams(("arbitrary", "arbitrary", "arbitrary")),
        name="nsa_prompt",
    )(qn, qr, cc, cc, kvp, kvp, ng, cover)


def _nsa_sample_kernel(pt_ref, qn_ref, qr_ref, cc_ref, new_ref, win_ref, ng_ref, cover_ref, cache_ref,
                       out_ref, buf_ref, sem, *, past, tnew, nsb, ksel):
    b = pl.program_id(0)
    nb = pl.num_programs(0)
    npages = pt_ref.shape[1]
    page = past // npages
    nck = cc_ref.shape[2]
    rows = NSA_HEADS * SROWS
    half = rows // 2

    def copies(bb, slot):
        out = []
        for p in range(npages):
            pg = pt_ref[bb, p]
            out.append(pltpu.make_async_copy(cache_ref.at[pl.ds(pg * page, page), :],
                                             buf_ref.at[slot, pl.ds(p * page, page), :], sem.at[slot]))
        return out

    @pl.when(b == 0)
    def _():
        for c in copies(0, 0):
            c.start()

    @pl.when(b + 1 < nb)
    def _():
        for c in copies(b + 1, (b + 1) % 2):
            c.start()

    lane = lax.broadcasted_iota(I32, (rows, LANES), 1)
    row = lax.broadcasted_iota(I32, (rows, LANES), 0)
    tt = row % SROWS
    lo = lane < DH

    qn = jnp.concatenate([qn_ref[:, h * LANES:(h + 1) * LANES] for h in range(NSA_HEADS)], axis=0)
    qr32 = jnp.concatenate([qr_ref[:, h * LANES:(h + 1) * LANES] for h in range(NSA_HEADS)], axis=0).astype(F32)
    qr = jnp.where(row < half, qr32, pltpu.roll(qr32, DH, 1)).astype(BF16)

    nlt = nck // LANES
    o_cmp = []
    psums = []
    for g in range(NSA_G):
        ckb = cc_ref[0, g].astype(BF16)
        cvb = cc_ref[0, 2 + g].astype(BF16)
        qg = qn[g * half:(g + 1) * half]
        s = lax.dot_general(qg, ckb, (((1,), (1,)), ((), ())), preferred_element_type=F32)
        cl = lax.broadcasted_iota(I32, s.shape, 1)
        s = jnp.where(cl < nck - 1, s, -jnp.inf)
        m = jnp.max(s, axis=1, keepdims=True)
        e = jnp.exp(s - m)
        p = e / jnp.sum(e, axis=1, keepdims=True)
        o_cmp.append(jnp.dot(p.astype(BF16), cvb, preferred_element_type=F32))
        ps = p[0:SROWS]
        for h in range(1, NSA_HG):
            ps = ps + p[h * SROWS:(h + 1) * SROWS]
        psums.append(ps)
    psum = jnp.concatenate(psums, axis=0)
    cover = cover_ref[...]
    p_hi = psum.astype(BF16)
    p_lo = (psum - p_hi.astype(F32)).astype(BF16)
    imp = jnp.dot(p_hi, cover, preferred_element_type=F32) + jnp.dot(p_lo, cover, preferred_element_type=F32)
    nsl = imp.shape[1]
    sidx = lax.broadcasted_iota(I32, imp.shape, 1)
    tpos16 = past + lax.broadcasted_iota(I32, imp.shape, 0) % SROWS
    cur = tpos16 // SLC_BLOCK
    valid = sidx * SLC_BLOCK <= tpos16
    forced = (sidx == 0) | (sidx == cur) | (sidx == cur - 1)
    score = jnp.where(forced, SEL_FORCED, jnp.where(valid, imp, -jnp.inf))
    in_sel = sidx < nsb
    selected = _topk_mask(score, sidx, in_sel, 0, nsb, ksel)
    selb16 = jnp.where(selected, 0.0, NEG)
    selrows = jnp.concatenate([selb16[g * SROWS:(g + 1) * SROWS] for g in range(NSA_G) for _ in range(NSA_HG)],
                              axis=0).astype(BF16)

    slot = b % 2
    for c in copies(b, slot):
        c.wait()

    kt_rows = 512
    nkt = past // kt_rows
    bpt = kt_rows // SLC_BLOCK
    srow = lax.broadcasted_iota(I32, (nsl, kt_rows), 0)
    kcol = lax.broadcasted_iota(I32, (nsl, kt_rows), 1)

    def slc_tile(j, st):
        r0 = pl.multiple_of(j * kt_rows, kt_rows)
        kt = buf_ref[slot, pl.ds(r0, kt_rows), 0:LANES].astype(BF16)
        vt = buf_ref[slot, pl.ds(r0, kt_rows), LANES:2 * LANES].astype(BF16)
        expand = (srow == j * bpt + kcol // SLC_BLOCK).astype(BF16)
        s = lax.dot_general(qr, kt, (((1,), (1,)), ((), ())), preferred_element_type=F32)
        s = s + jnp.dot(selrows, expand, preferred_element_type=F32)
        return _softmax_update(s, *st, vt)

    zero_state = (jnp.full((rows, 1), NEG, F32), jnp.zeros((rows, 1), F32), jnp.zeros((rows, LANES), F32))
    st = lax.fori_loop(0, nkt, slc_tile, zero_state)
    newk = new_ref[:, 2 * LANES:3 * LANES].astype(BF16)
    newv = new_ref[:, 3 * LANES:4 * LANES].astype(BF16)
    kk = lax.broadcasted_iota(I32, (rows, SROWS), 1)
    tt8 = lax.broadcasted_iota(I32, (rows, SROWS), 0) % SROWS
    new_ok = (kk <= tt8) & (kk < tnew)
    nb_new = past // SLC_BLOCK
    bias_new = selrows[:, nb_new:nb_new + 1].astype(F32)
    s = lax.dot_general(qr, newk, (((1,), (1,)), ((), ())), preferred_element_type=F32)
    s = jnp.where(new_ok, s + bias_new, NEG)
    m, l, acc = _softmax_update(s, *st, newv)
    o_slc = acc / l

    wb = win_ref.shape[1]
    wk = win_ref[0, :, 0:LANES].astype(BF16)
    wv = win_ref[0, :, LANES:2 * LANES].astype(BF16)
    wi = lax.broadcasted_iota(I32, (rows, wb), 1)
    wt = lax.broadcasted_iota(I32, (rows, wb), 0) % SROWS
    w_ok = (wt + wb - wi < WINDOW) if wb >= WINDOW else (wi >= 0)
    s = lax.dot_general(qr, wk, (((1,), (1,)), ((), ())), preferred_element_type=F32)
    s = jnp.where(w_ok, s, NEG)
    st = _softmax_update(s, *zero_state, wv)
    nwk = new_ref[:, 4 * LANES:5 * LANES].astype(BF16)
    nwv = new_ref[:, 5 * LANES:6 * LANES].astype(BF16)
    s = lax.dot_general(qr, nwk, (((1,), (1,)), ((), ())), preferred_element_type=F32)
    s = jnp.where(new_ok, s, NEG)
    m, l, acc = _softmax_update(s, *st, nwv)
    o_win = acc / l

    gates = _sigmoid(ng_ref[...])
    outs = []
    for hh in range(NSA_HEADS):
        g, h = divmod(hh, NSA_HG)
        gl = gates[:, g * LANES:(g + 1) * LANES]
        g0 = gl[:, h:h + 1]
        g1 = gl[:, NSA_HG + h:NSA_HG + h + 1]
        g2 = gl[:, 2 * NSA_HG + h:2 * NSA_HG + h + 1]
        r0 = hh * SROWS
        oc = o_cmp[g][h * SROWS:(h + 1) * SROWS]
        outs.append(g0 * oc + g1 * o_slc[r0:r0 + SROWS] + g2 * o_win[r0:r0 + SROWS])
    lo8 = lax.broadcasted_iota(I32, (SROWS, LANES), 1) < DH
    out_ref[:, 0 * LANES:1 * LANES] = jnp.where(lo8, outs[0], pltpu.roll(outs[1], DH, 1)).astype(BF16)
    out_ref[:, 1 * LANES:2 * LANES] = jnp.where(lo8, outs[2], pltpu.roll(outs[3], DH, 1)).astype(BF16)
    out_ref[:, 2 * LANES:3 * LANES] = jnp.where(lo8, pltpu.roll(outs[4], DH, 1), outs[5]).astype(BF16)
    out_ref[:, 3 * LANES:4 * LANES] = jnp.where(lo8, pltpu.roll(outs[6], DH, 1), outs[7]).astype(BF16)


def _nsa_sample(qn, qr, cc, kv, win_cache, ng, slc2d, page_table, past, tnew, row0):
    nb, npages = page_table.shape
    page = past // npages
    nck = past // CMP_STRIDE
    nsb = -(-(past + tnew) // SLC_BLOCK)
    nsl = -(-nsb // LANES) * LANES
    assert (nck - 2) * CMP_STRIDE + 2 * CMP_STRIDE - 1 <= past and (past + tnew) // CMP_STRIDE == nck
    assert past % 512 == 0 and tnew <= SROWS and past // SLC_BLOCK == nsb - 1
    cover = _cover_matrix(nck - 1, nsb, 0, nck, nsl)
    wb = win_cache.shape[1]
    kern = functools.partial(_nsa_sample_kernel, past=past, tnew=tnew, nsb=nsb, ksel=min(SLC_TOPK, nsb))
    rb = row0 // SROWS
    return pl.pallas_call(
        kern,
        grid_spec=pltpu.PrefetchScalarGridSpec(
            num_scalar_prefetch=1,
            grid=(nb,),
            in_specs=[pl.BlockSpec((SROWS, NSA_HEADS * LANES), lambda b, pt: (rb + b, 0)),
                      pl.BlockSpec((SROWS, NSA_HEADS * LANES), lambda b, pt: (rb + b, 0)),
                      pl.BlockSpec((1, 4, nck, LANES), lambda b, pt: (b, 0, 0, 0)),
                      pl.BlockSpec((SROWS, 6 * LANES), lambda b, pt: (rb + b, 0)),
                      pl.BlockSpec((1, wb, 2 * LANES), lambda b, pt: (b, 0, 0)),
                      pl.BlockSpec((SROWS, 2 * LANES), lambda b, pt: (rb + b, 0)),
                      pl.BlockSpec((nck, nsl), lambda b, pt: (0, 0)),
                      pl.BlockSpec(memory_space=pl.ANY)],
            out_specs=pl.BlockSpec((SROWS, NSA_HEADS * DH), lambda b, pt: (b, 0)),
            scratch_shapes=[pltpu.VMEM((2, past, 2 * LANES), F32), pltpu.SemaphoreType.DMA((2,))]),
        out_shape=jax.ShapeDtypeStruct((nb * SROWS, NSA_HEADS * DH), BF16),
        compiler_params=_cparams(("arbitrary",)),
        name="nsa_sample",
    )(page_table, qn, qr, cc, kv, win_cache, ng, cover, slc2d)


def _hgrn_kernel(hq_ref, hf_ref, hi_ref, hg_ref, lb_ref, nw_ref, s0_ref, o_ref, sout_ref, *, chunk, valid, nchunks):
    lb = lb_ref[0]
    nw = nw_ref[...]
    row = lax.broadcasted_iota(I32, (chunk, HG_D), 0)
    rvalid = row < valid

    def body(i, st):
        r0 = pl.multiple_of(i * chunk, chunk)
        hq = hq_ref[pl.ds(r0, chunk), :]
        hf = hf_ref[pl.ds(r0, chunk), :]
        iv = hi_ref[pl.ds(r0, chunk), :]
        hg = hg_ref[pl.ds(r0, chunk), :]
        q = hq * _sigmoid(hq)
        f = lb + (1.0 - lb) * _sigmoid(hf)
        logf = jnp.where(rvalid, jnp.log(f), 0.0)
        k = jnp.where(rvalid, 1.0 - f, 0.0)
        b = logf
        sh = 1
        while sh < chunk:
            b = b + jnp.where(row >= sh, pltpu.roll(b, sh, 0), 0.0)
            sh *= 2
        o_in = jnp.zeros((chunk, HG_D), F32)
        for s in range(valid):
            e = jnp.exp(jnp.where(row >= s, b - b[s:s + 1, :], -jnp.inf))
            a = jnp.sum(q * e * k[s:s + 1, :], axis=1, keepdims=True)
            o_in = o_in + a * iv[s:s + 1, :]
        qd = (q * jnp.exp(b)).astype(BF16)
        o = o_in + lax.dot_general(qd, st.astype(BF16), (((1,), (1,)), ((), ())), preferred_element_type=F32)
        bl = b[chunk - 1:chunk, :]
        kd = (k * jnp.exp(bl - b)).astype(BF16)
        u = lax.dot_general(iv.astype(BF16), kd, (((0,), (0,)), ((), ())), preferred_element_type=F32)
        st = st * jnp.exp(bl) + u
        ms = jnp.mean(o * o, axis=1, keepdims=True)
        y = o * lax.rsqrt(ms + RMS_EPS) * nw * (hg * _sigmoid(hg))
        o_ref[pl.ds(r0, chunk), :] = y.astype(BF16)
        return st

    st = lax.fori_loop(0, nchunks, body, s0_ref[0, 0].T)
    sout_ref[0, 0] = st.T


def _hgrn(hh, lb, nw, s0, nseq, rows_per_seq, row0, chunk, valid):
    rb = row0 // rows_per_seq
    kern = functools.partial(_hgrn_kernel, chunk=chunk, valid=valid, nchunks=rows_per_seq // chunk)
    blk = lambda k: pl.BlockSpec((rows_per_seq, HG_D), lambda b, h, k=k: (rb + b, k * HG_HEADS + h))
    return pl.pallas_call(
        kern,
        grid=(nseq, HG_HEADS),
        in_specs=[blk(0), blk(1), blk(2), blk(3),
                  pl.BlockSpec((1, 1, HG_D), lambda b, h: (h, 0, 0)),
                  pl.BlockSpec((1, HG_D), lambda b, h: (0, 0)),
                  pl.BlockSpec((1, 1, HG_D, HG_D), lambda b, h: (b, h, 0, 0))],
        out_specs=[pl.BlockSpec((rows_per_seq, HG_D), lambda b, h: (b, h)),
                   pl.BlockSpec((1, 1, HG_D, HG_D), lambda b, h: (b, h, 0, 0))],
        out_shape=[jax.ShapeDtypeStruct((nseq * rows_per_seq, HG_HEADS * HG_D), BF16),
                   jax.ShapeDtypeStruct((nseq, HG_HEADS, HG_D, HG_D), F32)],
        compiler_params=_cparams(("arbitrary", "arbitrary")),
        name="hgrn",
    )(hh, hh, hh, hh, lb, nw, s0)


def _layernorm(y, g, b):
    mu = jnp.mean(y, axis=1, keepdims=True)
    yc = y - mu
    var = jnp.mean(yc * yc, axis=1, keepdims=True)
    return yc * lax.rsqrt(var + LN_EPS) * g + b


def _finish_kernel(x_ref, a_ref, hgo_ref, gg_ref, wpa_ref, wpb_ref, wo_ref, g1_ref, b1_ref, wr_ref, br_ref, ltri_ref,
                   h_ref, topi_ref, topw_ref, pos_ref, cnt_ref, carry_ref):
    i = pl.program_id(0)

    @pl.when(i == 0)
    def _():
        carry_ref[...] = jnp.zeros_like(carry_ref)

    a = jnp.dot(a_ref[...], wpa_ref[...], preferred_element_type=F32)
    bb = jnp.dot(hgo_ref[...], wpb_ref[...], preferred_element_type=F32)
    m = gg_ref[:, :D_MODEL] * a + gg_ref[:, D_MODEL:] * bb
    mix = jnp.dot(m.astype(BF16), wo_ref[...], preferred_element_type=F32)
    h = _layernorm(DN_ALPHA * x_ref[...] + mix, g1_ref[...], b1_ref[...])
    h_ref[...] = h

    h1 = h.astype(BF16)
    h2 = (h - h1.astype(F32)).astype(BF16)
    w = wr_ref[...]
    w1 = w.astype(BF16)
    w2 = (w - w1.astype(F32)).astype(BF16)
    lg = (jnp.dot(h1, w1, preferred_element_type=F32) + jnp.dot(h1, w2, preferred_element_type=F32)
          + jnp.dot(h2, w1, preferred_element_type=F32)) + br_ref[...]
    tm = lg.shape[0]
    lane = lax.broadcasted_iota(I32, (tm, LANES), 1)
    lg = jnp.where(lane < N_EXPERTS, lg, -jnp.inf)
    vals, idxs = [], []
    for _ in range(TOP_K):
        mx = jnp.max(lg, axis=1, keepdims=True)
        ix = jnp.min(jnp.where(lg == mx, lane, LANES), axis=1, keepdims=True)
        vals.append(mx)
        idxs.append(ix)
        lg = jnp.where(lane == ix, -jnp.inf, lg)
    es = [jnp.exp(v - vals[0]) for v in vals]
    den = es[0] + es[1] + es[2] + es[3]
    topi = jnp.zeros((tm, LANES), I32)
    topw = jnp.zeros((tm, LANES), F32)
    onehot = jnp.zeros((tm, LANES), F32)
    for j in range(TOP_K):
        topi = jnp.where(lane == j, idxs[j], topi)
        topw = jnp.where(lane == j, es[j] / den, topw)
        onehot = onehot + (lane == idxs[j]).astype(F32)
    topi_ref[...] = topi
    topw_ref[...] = topw
    prefix = jnp.dot(ltri_ref[...], onehot.astype(BF16), preferred_element_type=F32) + carry_ref[0:1, :]
    pos = jnp.zeros((tm, LANES), I32)
    for j in range(TOP_K):
        pj = jnp.sum(jnp.where(lane == idxs[j], prefix, 0.0), axis=1, keepdims=True)
        pos = jnp.where(lane == j, pj.astype(I32), pos)
    pos_ref[...] = pos
    carry_ref[...] = carry_ref[...] + jnp.sum(onehot, axis=0, keepdims=True)
    cnt_ref[...] = carry_ref[...]


def _finish(x, nsa_o, hg_o, gg, wpa, wpb, wo, g1, b1, wr, br):
    n = x.shape[0]
    row = lambda i: (i, 0)
    const = lambda i: (0, 0)
    ltri = jnp.asarray(np.tril(np.ones((TM, TM), np.float32), -1), dtype=BF16)
    outs = pl.pallas_call(
        _finish_kernel,
        grid=(n // TM,),
        in_specs=[pl.BlockSpec((TM, D_MODEL), row), pl.BlockSpec((TM, 512), row), pl.BlockSpec((TM, 512), row),
                  pl.BlockSpec((TM, 2 * D_MODEL), row),
                  pl.BlockSpec((512, D_MODEL), const), pl.BlockSpec((512, D_MODEL), const),
                  pl.BlockSpec((D_MODEL, D_MODEL), const),
                  pl.BlockSpec((1, D_MODEL), const), pl.BlockSpec((1, D_MODEL), const),
                  pl.BlockSpec((D_MODEL, LANES), const), pl.BlockSpec((1, LANES), const),
                  pl.BlockSpec((TM, TM), const)],
        out_specs=[pl.BlockSpec((TM, D_MODEL), row), pl.BlockSpec((TM, LANES), row), pl.BlockSpec((TM, LANES), row),
                   pl.BlockSpec((TM, LANES), row), pl.BlockSpec((8, LANES), const)],
        out_shape=[jax.ShapeDtypeStruct((n, D_MODEL), F32), jax.ShapeDtypeStruct((n, LANES), I32),
                   jax.ShapeDtypeStruct((n, LANES), F32), jax.ShapeDtypeStruct((n, LANES), I32),
                   jax.ShapeDtypeStruct((8, LANES), F32)],
        scratch_shapes=[pltpu.VMEM((8, LANES), F32)],
        compiler_params=_cparams(("arbitrary",)),
        name="finish",
    )(x, nsa_o, hg_o, gg, wpa, wpb, wo, g1, b1, wr, br, ltri)
    return outs


def _dispatch_kernel(dest_ref, h_ref, xs_in_ref, xs_ref, sem):
    del xs_in_ref

    def copy(r, j):
        d = dest_ref[0, 0, r * TOP_K + j]
        return pltpu.make_async_copy(h_ref.at[pl.ds(r, 1), :], xs_ref.at[pl.ds(d, 1), :], sem)

    def start(r, c):
        for j in range(TOP_K):
            copy(r, j).start()
        return c

    def wait(r, c):
        for j in range(TOP_K):
            copy(r, j).wait()
        return c

    lax.fori_loop(0, TD, start, 0)
    lax.fori_loop(0, TD, wait, 0)


def _dispatch(dest3, h, nrows):
    n = h.shape[0]
    xs0 = jnp.zeros((nrows, D_MODEL), F32)
    return pl.pallas_call(
        _dispatch_kernel,
        grid=(n // TD,),
        in_specs=[pl.BlockSpec((1, 1, TD * TOP_K), lambda i: (i, 0, 0), memory_space=pltpu.SMEM),
                  pl.BlockSpec((TD, D_MODEL), lambda i: (i, 0)),
                  pl.BlockSpec(memory_space=pl.ANY)],
        out_specs=pl.BlockSpec(memory_space=pl.ANY),
        out_shape=jax.ShapeDtypeStruct((nrows, D_MODEL), F32),
        scratch_shapes=[pltpu.SemaphoreType.DMA(())],
        input_output_aliases={2: 0},
        compiler_params=_cparams(("arbitrary",)),
        name="dispatch",
    )(dest3, h, xs0)


def _experts_kernel(bexp_ref, nused_ref, x_ref, wgu_ref, bgu_ref, wdn_ref, bdn_ref, y_ref):
    i = pl.program_id(0)

    @pl.when(i < nused_ref[0])
    def _():
        xb = x_ref[...].astype(BF16)
        acc = jnp.zeros((TE, D_MODEL), F32)
        cw = 256
        for c in range(D_FF // cw):
            gt = jnp.dot(xb, wgu_ref[0, :, c * cw:(c + 1) * cw], preferred_element_type=F32) + bgu_ref[0, :, c * cw:(c + 1) * cw]
            up = (jnp.dot(xb, wgu_ref[0, :, D_FF + c * cw:D_FF + (c + 1) * cw], preferred_element_type=F32)
                  + bgu_ref[0, :, D_FF + c * cw:D_FF + (c + 1) * cw])
            gt = jnp.minimum(gt, SWIGLU_LIMIT)
            up = jnp.clip(up, -SWIGLU_LIMIT, SWIGLU_LIMIT)
            act = (up + 1.0) * (gt * _sigmoid(SWIGLU_ALPHA * gt))
            acc = acc + jnp.dot(act.astype(BF16), wdn_ref[0, c * cw:(c + 1) * cw, :], preferred_element_type=F32)
        y_ref[...] = acc + bdn_ref[0]

    @pl.when(i >= nused_ref[0])
    def _():
        y_ref[...] = jnp.zeros_like(y_ref)


def _experts(bexp, nused, xs, wgu, bgu, wdn, bdn):
    nt = xs.shape[0] // TE
    return pl.pallas_call(
        _experts_kernel,
        grid_spec=pltpu.PrefetchScalarGridSpec(
            num_scalar_prefetch=2,
            grid=(nt,),
            in_specs=[pl.BlockSpec((TE, D_MODEL), lambda i, be, nu: (i, 0)),
                      pl.BlockSpec((1, D_MODEL, 2 * D_FF), lambda i, be, nu: (be[i], 0, 0)),
                      pl.BlockSpec((1, 1, 2 * D_FF), lambda i, be, nu: (be[i], 0, 0)),
                      pl.BlockSpec((1, D_FF, D_MODEL), lambda i, be, nu: (be[i], 0, 0)),
                      pl.BlockSpec((1, 1, D_MODEL), lambda i, be, nu: (be[i], 0, 0))],
            out_specs=pl.BlockSpec((TE, D_MODEL), lambda i, be, nu: (i, 0))),
        out_shape=jax.ShapeDtypeStruct(xs.shape, F32),
        compiler_params=_cparams(("arbitrary",)),
        name="experts",
    )(bexp, nused, xs, wgu, bgu, wdn, bdn)


def _combine_kernel(dest_ref, h_ref, topw_ref, g2_ref, b2_ref, ys_ref, out_ref, buf_ref, sem):
    def copy(r, j):
        d = dest_ref[0, 0, r * TOP_K + j]
        return pltpu.make_async_copy(ys_ref.at[pl.ds(d, 1), :], buf_ref.at[j, pl.ds(r, 1), :], sem)

    def start(r, c):
        for j in range(TOP_K):
            copy(r, j).start()
        return c

    def wait(r, c):
        for j in range(TOP_K):
            copy(r, j).wait()
        return c

    lax.fori_loop(0, TD, start, 0)
    lax.fori_loop(0, TD, wait, 0)
    tw = topw_ref[...]
    moe = tw[:, 0:1] * buf_ref[0]
    for j in range(1, TOP_K):
        moe = moe + tw[:, j:j + 1] * buf_ref[j]
    out_ref[...] = _layernorm(DN_ALPHA * h_ref[...] + moe, g2_ref[...], b2_ref[...])


def _combine(dest3, h, topw, g2, b2, ys):
    n = h.shape[0]
    return pl.pallas_call(
        _combine_kernel,
        grid=(n // TD,),
        in_specs=[pl.BlockSpec((1, 1, TD * TOP_K), lambda i: (i, 0, 0), memory_space=pltpu.SMEM),
                  pl.BlockSpec((TD, D_MODEL), lambda i: (i, 0)),
                  pl.BlockSpec((TD, LANES), lambda i: (i, 0)),
                  pl.BlockSpec((1, D_MODEL), lambda i: (0, 0)), pl.BlockSpec((1, D_MODEL), lambda i: (0, 0)),
                  pl.BlockSpec(memory_space=pl.ANY)],
        out_specs=pl.BlockSpec((TD, D_MODEL), lambda i: (i, 0)),
        out_shape=jax.ShapeDtypeStruct((n, D_MODEL), F32),
        scratch_shapes=[pltpu.VMEM((TOP_K, TD, D_MODEL), F32), pltpu.SemaphoreType.DMA(())],
        compiler_params=_cparams(("arbitrary",)),
        name="combine",
    )(dest3, h, topw, g2, b2, ys)


def kernel(x_prompt, x_sample, cache_cmp_kv, cache_slc_kv, cache_win_kv, state_hgrn, page_table, w_in, pe_ck, w_ck1,
           w_ck2, pe_cv, w_cv1, w_cv2, hg_lb_logits, hg_norm_w, w_pa, w_pb, w_o, ln1_g, ln1_b, w_router, b_router,
           w_gu, b_gu, w_dn, b_dn, ln2_g, ln2_b):
    nb, t, d = x_prompt.shape
    ndb, tnew, _ = x_sample.shape
    depth, npool, page = cache_cmp_kv.shape[:3]
    past = page_table.shape[1] * page
    assert depth == 1 and d == D_MODEL and t % TM == 0 and (ndb * SROWS) % TM == 0 and tnew <= SROWS
    n_p = nb * t
    n_s = ndb * SROWS
    n = n_p + n_s

    xs_pad = jnp.pad(x_sample, ((0, 0), (0, SROWS - tnew), (0, 0)))
    x = jnp.concatenate([x_prompt.reshape(n_p, d), xs_pad.reshape(n_s, d)], axis=0)

    w_all = _repack_w_in(w_in[0])
    tables = _rope_tables(t, past)
    qn, qr, kv, kvp, ng, hh, gg = _project(x, w_all, tables, n_p // TM, t)

    cw = (_compress_weights(pe_ck[0], w_ck1[0], w_ck2[0]), _compress_weights(pe_cv[0], w_cv1[0], w_cv2[0]))
    cc_p = _compress_prompt(kv, nb, t, cw)
    cmp2d = cache_cmp_kv[0].reshape(npool * page, 2 * NSA_G * DH)
    cc_s = _compress_sample(cmp2d, page_table, page, cw)
    nsa_p = _nsa_prompt(qn, qr, cc_p, kvp, ng, nb, t)
    slc2d = cache_slc_kv[0].reshape(npool * page, 2 * NSA_G * DH)
    wbuf = cache_win_kv.shape[2]
    win2d = cache_win_kv[0].reshape(ndb, wbuf, 2 * NSA_G * DH)
    nsa_s = _nsa_sample(qn, qr, cc_s, kv, win2d, ng, slc2d, page_table, past, tnew, n_p)
    nsa_o = jnp.concatenate([nsa_p, nsa_s], axis=0)

    lb_all = jnp.cumsum(jax.nn.softmax(hg_lb_logits.astype(F32), axis=0), axis=0)
    lb = lb_all[0].reshape(HG_HEADS, 1, HG_D)
    nw = hg_norm_w[0].reshape(1, HG_D)
    hg_p, st_p = _hgrn(hh, lb, nw, jnp.zeros((nb, HG_HEADS, HG_D, HG_D), F32), nb, t, 0, HG_CHUNK, HG_CHUNK)
    hg_s, st_s = _hgrn(hh, lb, nw, state_hgrn[0], ndb, SROWS, n_p, SROWS, tnew)
    hg_o = jnp.concatenate([hg_p, hg_s], axis=0)

    wr = jnp.pad(w_router[0], ((0, 0), (0, LANES - N_EXPERTS)))
    br = jnp.pad(b_router[0], (0, LANES - N_EXPERTS)).reshape(1, LANES)
    h, topi, topw, pos, cnt = _finish(x, nsa_o, hg_o, gg, w_pa[0].astype(BF16), w_pb[0].astype(BF16),
                                      w_o[0].astype(BF16), ln1_g[0].reshape(1, d), ln1_b[0].reshape(1, d), wr, br)

    counts = cnt[0, :N_EXPERTS].astype(I32)
    pcounts = (counts + TE - 1) // TE * TE
    pend = jnp.cumsum(pcounts)
    pstart = pend - pcounts
    nt = (n * TOP_K) // TE + N_EXPERTS
    bexp = jnp.minimum(jnp.sum(jnp.arange(nt, dtype=I32)[:, None] * TE >= pend[None, :], axis=1), N_EXPERTS - 1).astype(I32)
    nused = (pend[-1:] // TE).astype(I32)
    dest = pstart[topi[:, :TOP_K]] + pos[:, :TOP_K]
    dest3 = dest.reshape(n // TD, 1, TD * TOP_K)

    xs = _dispatch(dest3, h, nt * TE)
    ys = _experts(bexp, nused, xs, w_gu[0].astype(BF16), b_gu[0].reshape(N_EXPERTS, 1, 2 * D_FF),
                  w_dn[0].astype(BF16), b_dn[0].reshape(N_EXPERTS, 1, D_MODEL))
    y = _combine(dest3, h, topw, ln2_g[0].reshape(1, d), ln2_b[0].reshape(1, d), ys)

    y_p = y[:n_p].reshape(nb, t, d)
    y_s = y[n_p:].reshape(ndb, SROWS, d)[:, :tnew]
    kv_p = kv[:n_p].reshape(nb, t, 3, 2, NSA_G, DH)
    kv_s = kv[n_p:].reshape(ndb, SROWS, 3, 2, NSA_G, DH)[:, :tnew]
    keep = min(WINDOW, t)
    win_s = jnp.concatenate([cache_win_kv[0], kv_s[:, :, 2]], axis=1)[:, tnew:]
    return (y_p, y_s, kv_p[:, :, 0][None], kv_s[:, :, 0][None], kv_p[:, :, 1][None], kv_s[:, :, 1][None],
            kv_p[:, t - keep:, 2][None], win_s[None], st_p[None], st_s[None])
```

```python
import functools
import math

import numpy as np
import jax
import jax.numpy as jnp
from jax import lax
from jax.experimental import pallas as pl
from jax.experimental.pallas import tpu as pltpu

F32 = jnp.float32
BF16 = jnp.bfloat16
I32 = jnp.int32

D_MODEL = 1024
NSA_HEADS = 8
NSA_G = 2
NSA_HG = 4
DH = 64
CMP_STRIDE = 16
CMP_HIDDEN = 256
SLC_BLOCK = 64
SLC_TOPK = 16
WINDOW = 512
SEL_FORCED = 1.0e4
ROPE_THETA = 10000.0
HG_HEADS = 4
HG_D = 128
N_EXPERTS = 32
TOP_K = 4
D_FF = 1024
SWIGLU_LIMIT = 7.0
SWIGLU_ALPHA = 1.702
LN_EPS = 1e-5
RMS_EPS = 1e-6
DN_ALPHA = 2.0 ** 0.25

LANES = 128
TM = 256
TQ = 128
SROWS = 8
HG_CHUNK = 16
TE = 256
TD = 128
NEG = -1.0e30
VMEM_LIMIT = 56 * 1024 * 1024

_C_Q, _C_KV, _C_PK, _C_NG, _C_HH, _C_GG, _C_END = 0, 1024, 1792, 2816, 3072, 5120, 7168


def _sigmoid(x):
    return 1.0 / (1.0 + jnp.exp(-x))


def _cparams(sem, vmem=VMEM_LIMIT):
    return pltpu.CompilerParams(dimension_semantics=sem, vmem_limit_bytes=vmem)


def _repack_w_in(w_in):
    d = w_in.shape[0]
    z64 = jnp.zeros((d, DH), F32)
    q = w_in[:, :512].reshape(d, NSA_HEADS, DH) * (DH ** -0.5)
    qpad = jnp.concatenate([q, jnp.zeros_like(q)], axis=-1).reshape(d, NSA_HEADS * LANES)
    kv = w_in[:, 512:1280]
    kv6 = kv.reshape(d, 6, NSA_G, DH)
    packs = []
    for kidx, vidx in ((2, 3), (4, 5)):
        for g in range(NSA_G):
            packs += [kv6[:, kidx, g], z64, kv6[:, vidx, g], z64]
    ng = w_in[:, 1280:1304].reshape(d, 3, NSA_G, NSA_HG)
    ngp = []
    for g in range(NSA_G):
        ngp += [ng[:, :, g, :].reshape(d, 3 * NSA_HG), jnp.zeros((d, LANES - 3 * NSA_HG), F32)]
    w = jnp.concatenate([qpad, kv] + packs + ngp + [w_in[:, 1304:3352], w_in[:, 3352:5400]], axis=-1)
    assert w.shape[1] == _C_END
    return w.astype(BF16)


def _rope_tables(t_prompt, past):
    half = DH // 2
    inv = ROPE_THETA ** (-jnp.arange(half, dtype=F32) / half)
    pos = jnp.concatenate([jnp.arange(t_prompt, dtype=I32), past + (jnp.arange(TM, dtype=I32) % SROWS)])
    ang = pos.astype(F32)[:, None] * inv[None, :]
    cos, sin = jnp.cos(ang), jnp.sin(ang)
    cos64 = jnp.concatenate([cos, cos], axis=-1)
    sin64 = jnp.concatenate([-sin, sin], axis=-1)
    one, zero = jnp.ones_like(cos64), jnp.zeros_like(cos64)
    cos_a = jnp.concatenate([cos64, cos64], axis=-1)
    sin_a = jnp.concatenate([sin64, sin64], axis=-1)
    cos_b = jnp.concatenate([cos64, one], axis=-1)
    sin_b = jnp.concatenate([sin64, zero], axis=-1)
    return cos_a, sin_a, cos_b, sin_b


def _proj_kernel(x_ref, w_ref, ca_ref, sa_ref, cb_ref, sb_ref,
                 qn_ref, qr_ref, kv_ref, kvp_ref, ng_ref, hh_ref, gg_ref):
    xb = x_ref[...].astype(BF16)
    tm = xb.shape[0]
    lane = lax.broadcasted_iota(I32, (tm, LANES), 1)
    first = (lane % DH) < (DH // 2)
    ca, sa, cb, sb = ca_ref[...], sa_ref[...], cb_ref[...], sb_ref[...]
    ones_hi = (lane >= DH).astype(F32)

    def rope(a, c, s):
        partner = jnp.where(first, pltpu.roll(a, LANES - DH // 2, 1), pltpu.roll(a, DH // 2, 1))
        return a * c + partner * s

    def mm(c0, n):
        return jnp.dot(xb, w_ref[:, c0:c0 + n], preferred_element_type=F32)

    for c in range(2):
        acc = mm(_C_Q + c * 512, 512)
        for j in range(4):
            a = acc[:, j * LANES:(j + 1) * LANES]
            col = c * 512 + j * LANES
            qn_ref[:, col:col + LANES] = a.astype(BF16)
            qr_ref[:, col:col + LANES] = rope(a, cb, sb).astype(BF16)
    acc = mm(_C_KV, 768)
    for j in range(6):
        a = acc[:, j * LANES:(j + 1) * LANES]
        if j in (2, 4):
            a = rope(a, ca, sa)
        kv_ref[:, j * LANES:(j + 1) * LANES] = a
    for c in range(2):
        acc = mm(_C_PK + c * 512, 512)
        for g in range(NSA_G):
            k = rope(acc[:, g * 256:g * 256 + LANES], cb, sb)
            col = c * 512 + g * 256
            kvp_ref[:, col:col + LANES] = k.astype(BF16)
            v1 = acc[:, g * 256 + LANES:g * 256 + 2 * LANES] + ones_hi
            kvp_ref[:, col + LANES:col + 2 * LANES] = v1.astype(BF16)
    ng_ref[...] = mm(_C_NG, 256)
    for c in range(4):
        hh_ref[:, c * 512:(c + 1) * 512] = mm(_C_HH + c * 512, 512)
    for c in range(4):
        gg_ref[:, c * 512:(c + 1) * 512] = _sigmoid(mm(_C_GG + c * 512, 512))


def _project(x, w_all, tables, n_prompt_tiles, t_prompt):
    n = x.shape[0]
    tp = t_prompt // TM

    def tab_map(i):
        return (jnp.where(i < n_prompt_tiles, i % tp, tp), 0)

    row = lambda i: (i, 0)
    tab_spec = pl.BlockSpec((TM, LANES), tab_map)
    widths = (1024, 1024, 768, 1024, 256, 2048, 2048)
    dtypes = (BF16, BF16, F32, BF16, F32, F32, F32)
    return pl.pallas_call(
        _proj_kernel,
        grid=(n // TM,),
        in_specs=[pl.BlockSpec((TM, D_MODEL), row),
                  pl.BlockSpec((D_MODEL, _C_END), lambda i: (0, 0))] + [tab_spec] * 4,
        out_specs=[pl.BlockSpec((TM, w), row) for w in widths],
        out_shape=[jax.ShapeDtypeStruct((n, w), dt) for w, dt in zip(widths, dtypes)],
        compiler_params=_cparams(("arbitrary",)),
        name="proj",
    )(x, w_all, *tables)


def _gelu_tanh(x):
    return x * (0.5 * (1.0 + jnp.tanh(math.sqrt(2.0 / math.pi) * (x + 0.044715 * (x * x * x)))))


def _compress_core(xk_ref, xv_ref, z_ref, w1k_ref, w1v_ref, pek_ref, pev_ref, w2k_ref, w2v_ref, out_ref):
    nch = xk_ref.shape[0] // CMP_STRIDE
    lane = lax.broadcasted_iota(I32, (nch, LANES), 1)
    lo = lane < DH
    for p, xr in enumerate((xk_ref, xv_ref)):
        for r2 in range(CMP_STRIDE // 2):
            a = xr[pl.ds(2 * r2, nch, stride=CMP_STRIDE), :]
            b = xr[pl.ds(2 * r2 + 1, nch, stride=CMP_STRIDE), :]
            z0 = jnp.where(lo, a, pltpu.roll(b, DH, 1))
            z1 = jnp.where(lo, pltpu.roll(a, DH, 1), b)
            z_ref[2 * p, :, r2 * LANES:(r2 + 1) * LANES] = z0.astype(BF16)
            z_ref[2 * p + 1, :, r2 * LANES:(r2 + 1) * LANES] = z1.astype(BF16)
    for s in range(4):
        w1, pe, w2 = (w1k_ref, pek_ref, w2k_ref) if s < 2 else (w1v_ref, pev_ref, w2v_ref)
        pq = jnp.dot(z_ref[s], w1[...], preferred_element_type=F32)
        pb = jnp.dot(pe[...], w1[...], preferred_element_type=F32)
        bias = pb[0:1, :CMP_HIDDEN] + pb[8:9, CMP_HIDDEN:]
        h = pq[:, :CMP_HIDDEN] + pltpu.roll(pq[:, CMP_HIDDEN:], nch - 1, 0) + bias
        out_ref[s] = jnp.dot(_gelu_tanh(h).astype(BF16), w2[...], preferred_element_type=F32)


def _compress_prompt_kernel(xk_ref, xv_ref, w1k_ref, w1v_ref, pek_ref, pev_ref, w2k_ref, w2v_ref, out_ref, z_ref):
    _compress_core(xk_ref, xv_ref, z_ref, w1k_ref, w1v_ref, pek_ref, pev_ref, w2k_ref, w2v_ref, out_ref.at[0])


def _compress_sample_kernel(pt_ref, cache_ref, w1k_ref, w1v_ref, pek_ref, pev_ref, w2k_ref, w2v_ref,
                            out_ref, buf_ref, x_ref, z_ref, sem):
    b = pl.program_id(0)
    nb = pl.num_programs(0)
    npages = pt_ref.shape[1]
    page = buf_ref.shape[-1]

    def copies(bb, slot):
        return [pltpu.make_async_copy(cache_ref.at[pt_ref[bb, p]], buf_ref.at[slot, p], sem.at[slot])
                for p in range(npages)]

    @pl.when(b == 0)
    def _():
        for c in copies(0, 0):
            c.start()

    @pl.when(b + 1 < nb)
    def _():
        for c in copies(b + 1, (b + 1) % 2):
            c.start()

    slot = b % 2
    for c in copies(b, slot):
        c.wait()

    def to_rows(p, carry):
        r0 = pl.multiple_of(p * page, page)
        for kv in range(2):
            x_ref[kv, pl.ds(r0, page), :] = buf_ref[slot, p, kv].T
        return carry

    lax.fori_loop(0, npages, to_rows, 0)
    _compress_core(x_ref.at[0], x_ref.at[1], z_ref, w1k_ref, w1v_ref, pek_ref, pev_ref,
                   w2k_ref, w2v_ref, out_ref.at[0])


def _compress_weights(pe, w1, w2):
    w1cat = jnp.concatenate([w1[:CMP_STRIDE * DH], w1[CMP_STRIDE * DH:]], axis=1).astype(BF16)
    pef = pe.reshape(2, 1, CMP_STRIDE * DH)
    pe16 = jnp.broadcast_to(pef, (2, 8, CMP_STRIDE * DH)).reshape(16, CMP_STRIDE * DH).astype(BF16)
    w2cat = jnp.concatenate([w2, w2], axis=1).astype(BF16)
    return w1cat, pe16, w2cat


def _wspecs(n_grid_args):
    const = (lambda *a: (0, 0))
    return [pl.BlockSpec((CMP_STRIDE * DH, 2 * CMP_HIDDEN), const), pl.BlockSpec((CMP_STRIDE * DH, 2 * CMP_HIDDEN), const),
            pl.BlockSpec((16, CMP_STRIDE * DH), const), pl.BlockSpec((16, CMP_STRIDE * DH), const),
            pl.BlockSpec((CMP_HIDDEN, LANES), const), pl.BlockSpec((CMP_HIDDEN, LANES), const)]


def _compress_prompt(kv, nb, t, cw):
    nch = t // CMP_STRIDE
    (w1k, pek, w2k), (w1v, pev, w2v) = cw
    return pl.pallas_call(
        _compress_prompt_kernel,
        grid=(nb,),
        in_specs=[pl.BlockSpec((t, LANES), lambda b: (b, 0)), pl.BlockSpec((t, LANES), lambda b: (b, 1))] + _wspecs(1),
        out_specs=pl.BlockSpec((1, 4, nch, LANES), lambda b: (b, 0, 0, 0)),
        out_shape=jax.ShapeDtypeStruct((nb, 4, nch, LANES), F32),
        scratch_shapes=[pltpu.VMEM((4, nch, CMP_STRIDE * DH), BF16)],
        compiler_params=_cparams(("arbitrary",)),
        name="compress_prompt",
    )(kv, kv, w1k, w1v, pek, pev, w2k, w2v)


def _compress_sample(cache_t, page_table, cw):
    nb, npages = page_table.shape
    page = cache_t.shape[-1]
    assert page == LANES and cache_t.shape[1:3] == (2, NSA_G * DH)
    r = npages * page
    nch = r // CMP_STRIDE
    (w1k, pek, w2k), (w1v, pev, w2v) = cw
    wspecs = [pl.BlockSpec(s.block_shape, lambda b, pt: (0, 0)) for s in _wspecs(2)]
    return pl.pallas_call(
        _compress_sample_kernel,
        grid_spec=pltpu.PrefetchScalarGridSpec(
            num_scalar_prefetch=1,
            grid=(nb,),
            in_specs=[pl.BlockSpec(memory_space=pl.ANY)] + wspecs,
            out_specs=pl.BlockSpec((1, 4, nch, LANES), lambda b, pt: (b, 0, 0, 0)),
            scratch_shapes=[pltpu.VMEM((2, npages, 2, NSA_G * DH, page), F32), pltpu.VMEM((2, r, LANES), F32),
                            pltpu.VMEM((4, nch, CMP_STRIDE * DH), BF16), pltpu.SemaphoreType.DMA((2,))]),
        out_shape=jax.ShapeDtypeStruct((nb, 4, nch, LANES), F32),
        compiler_params=_cparams(("arbitrary",)),
        name="compress_sample",
    )(page_table, cache_t, w1k, w1v, pek, pev, w2k, w2v)


def _softmax_update(s, m, l, acc, v, v_is_t=False):
    m_new = jnp.maximum(m, jnp.max(s, axis=1, keepdims=True))
    alpha = jnp.exp(m - m_new)
    p = jnp.exp(s - m_new)
    l = alpha * l + jnp.sum(p, axis=1, keepdims=True)
    dims = (((1,), (1,)), ((), ())) if v_is_t else (((1,), (0,)), ((), ()))
    acc = alpha * acc + lax.dot_general(p.astype(BF16), v, dims, preferred_element_type=F32)
    return m_new, l, acc


def _topk_mask(score, sidx, in_sel, lane0, nsel, k):
    rank = jnp.zeros(score.shape, I32)
    for sp in range(nsel):
        c = score[:, lane0 + sp:lane0 + sp + 1]
        beats = (c > score) | ((c == score) & (sp < sidx))
        rank = rank + beats.astype(I32)
    return in_sel & (rank < k)


def _nsa_prompt_kernel(qn_ref, qr_ref, ck_ref, cv_ref, ks_ref, vs_ref, kw_ref, vw_ref, ng_ref, cover_ref,
                       out_ref, sel_ref, s_ref, *, nsb, ksel):
    qt = pl.program_id(2)
    t0 = qt * TQ
    nck = ck_ref.shape[2]
    krow = lax.broadcasted_iota(I32, (TQ, TQ), 0)
    qcol = lax.broadcasted_iota(I32, (TQ, TQ), 1)
    tq = t0 + qcol
    nt = (((1,), (1,)), ((), ()))

    ckb = ck_ref[0, 0].astype(BF16)
    cvt = cv_ref[0, 0].T.astype(BF16)
    cidx = lax.broadcasted_iota(I32, (nck, TQ), 0)
    cmask = (cidx * CMP_STRIDE + 2 * CMP_STRIDE - 1 <= t0 + lax.broadcasted_iota(I32, (nck, TQ), 1)) & (cidx < nck - 1)
    psum = jnp.zeros((nck, TQ), F32)
    o_cmp = []
    for h in range(NSA_HG):
        s = lax.dot_general(ckb, qn_ref[:, h * LANES:(h + 1) * LANES], nt, preferred_element_type=F32)
        s = jnp.where(cmask, s, -jnp.inf)
        m = jnp.max(s, axis=0, keepdims=True)
        m = jnp.where(m > -jnp.inf, m, 0.0)
        e = jnp.exp(s - m)
        d = jnp.sum(e, axis=0, keepdims=True)
        p = e / jnp.where(d > 0, d, 1.0)
        psum = psum + p
        o_cmp.append(jnp.dot(cvt, p.astype(BF16), preferred_element_type=F32)[0:DH])

    cover = cover_ref[...]
    p_hi = psum.astype(BF16)
    p_lo = (psum - p_hi.astype(F32)).astype(BF16)
    imp = jnp.dot(cover, p_hi, preferred_element_type=F32) + jnp.dot(cover, p_lo, preferred_element_type=F32)
    nsr = imp.shape[0]
    sidx = lax.broadcasted_iota(I32, (nsr, TQ), 0)
    tq_s = t0 + lax.broadcasted_iota(I32, (nsr, TQ), 1)
    cur = tq_s // SLC_BLOCK
    valid = sidx * SLC_BLOCK <= tq_s
    forced = (sidx == 0) | (sidx == cur) | (sidx == cur - 1)
    score = jnp.where(forced, SEL_FORCED, jnp.where(valid, imp, -jnp.inf))
    rank = jnp.zeros((nsr, TQ), I32)
    for sp in range(nsb):
        c = score[sp:sp + 1, :]
        rank = rank + ((c > score) | ((c == score) & (sp < sidx))).astype(I32)
    sel_ref[...] = jnp.where((rank < ksel) & (sidx < nsb), 0.0, NEG)

    ntiles = ks_ref.shape[0] // TQ
    qs = [qr_ref[:, h * LANES:(h + 1) * LANES] for h in range(NSA_HG)]
    unroll = 2
    vrows = DH + 8

    def attend(k_ref, v_ref, j_first, n_tiles, bias_fn):
        n_it = (n_tiles + unroll - 1) // unroll

        def tile_of(i, u):
            idx = i * unroll + u
            j = j_first + idx
            jc = jnp.clip(j, 0, ntiles - 1)
            return idx, j, jc, pl.multiple_of(jc * TQ, TQ)

        def scores(i, mx):
            mx = list(mx)
            for u in range(unroll):
                idx, j, jc, r0 = tile_of(i, u)
                kt = k_ref[pl.ds(r0, TQ), :]
                bias = jnp.where(idx < n_tiles, bias_fn(j, jc), NEG)
                for h in range(NSA_HG):
                    s = lax.dot_general(kt, qs[h], nt, preferred_element_type=F32) + bias
                    s_ref[h, idx] = s
                    mx[h] = jnp.maximum(mx[h], jnp.max(s.reshape(TQ // 8, 8, TQ), axis=0))
            return tuple(mx)

        mx = lax.fori_loop(0, n_it, scores, tuple(jnp.full((8, TQ), NEG, F32) for _ in range(NSA_HG)))
        ms = [jnp.max(m8, axis=0, keepdims=True) for m8 in mx]

        def values(i, accs):
            accs = list(accs)
            for u in range(unroll):
                idx, j, jc, r0 = tile_of(i, u)
                vt = v_ref[0:vrows, pl.ds(r0, TQ)]
                for h in range(NSA_HG):
                    p = jnp.exp(s_ref[h, idx] - ms[h])
                    accs[h] = accs[h] + jnp.dot(vt, p.astype(BF16), preferred_element_type=F32)
            return tuple(accs)

        accs = lax.fori_loop(0, n_it, values, tuple(jnp.zeros((vrows, TQ), F32) for _ in range(NSA_HG)))
        return [acc[0:DH] / acc[DH:DH + 1] for acc in accs]

    diag_bias = jnp.where(krow <= qcol, 0.0, NEG)
    dq = qcol - krow

    def slc_bias(j, jc):
        sel = jnp.where(krow < SLC_BLOCK, sel_ref[pl.ds(2 * jc, 1), :], sel_ref[pl.ds(2 * jc + 1, 1), :])
        return sel + jnp.where(j == qt, diag_bias, 0.0)

    def win_bias(j, jc):
        off = t0 - jc * TQ
        return jnp.where((dq >= -off) & (dq < WINDOW - off) & (j >= 0), 0.0, NEG)

    nwin = WINDOW // TQ
    gates = _sigmoid(ng_ref[...]).T
    o_slc = attend(ks_ref, vs_ref, 0, qt + 1, slc_bias)
    o_win = attend(kw_ref, vw_ref, qt - nwin, nwin + 1, win_bias)
    tot = []
    for h in range(NSA_HG):
        tot.append(gates[h:h + 1] * o_cmp[h] + gates[NSA_HG + h:NSA_HG + h + 1] * o_slc[h]
                   + gates[2 * NSA_HG + h:2 * NSA_HG + h + 1] * o_win[h])
    out_ref[...] = jnp.concatenate(tot, axis=0).T.astype(BF16)


def _cover_matrix(nc, nsb, lane0, rows, cols):
    c = np.arange(rows)[:, None]
    s = np.arange(cols)[None, :] - lane0
    cov = ((c < nc) & (s >= 0) & (s < nsb) & (c * CMP_STRIDE < (s + 1) * SLC_BLOCK)
           & (c * CMP_STRIDE + 2 * CMP_STRIDE > s * SLC_BLOCK))
    return jnp.asarray(cov.astype(np.float32), dtype=BF16)


def _nsa_prompt(qn, qr, cc, kvp, vt, ng, nb, t):
    nqt = t // TQ
    nch = t // CMP_STRIDE
    nsb = t // SLC_BLOCK
    nsr = 32
    assert nsb <= nsr and SLC_BLOCK * 2 == TQ
    cover = _cover_matrix(nch - 1, nsb, 0, nch, nsr).T
    kern = functools.partial(_nsa_prompt_kernel, nsb=nsb, ksel=min(SLC_TOPK, nsb))
    return pl.pallas_call(
        kern,
        grid=(nb, NSA_G, nqt),
        in_specs=[pl.BlockSpec((TQ, NSA_HG * LANES), lambda b, g, q: (b * nqt + q, g)),
                  pl.BlockSpec((TQ, NSA_HG * LANES), lambda b, g, q: (b * nqt + q, g)),
                  pl.BlockSpec((1, 1, nch, LANES), lambda b, g, q: (b, g, 0, 0)),
                  pl.BlockSpec((1, 1, nch, LANES), lambda b, g, q: (b, 2 + g, 0, 0)),
                  pl.BlockSpec((t, LANES), lambda b, g, q: (b, 2 * g)),
                  pl.BlockSpec((None, None, LANES, t), lambda b, g, q: (g, b, 0, 0)),
                  pl.BlockSpec((t, LANES), lambda b, g, q: (b, 4 + 2 * g)),
                  pl.BlockSpec((None, None, LANES, t), lambda b, g, q: (2 + g, b, 0, 0)),
                  pl.BlockSpec((TQ, LANES), lambda b, g, q: (b * nqt + q, g)),
                  pl.BlockSpec((nsr, nch), lambda b, g, q: (0, 0))],
        out_specs=pl.BlockSpec((TQ, 2 * LANES), lambda b, g, q: (b * nqt + q, g)),
        out_shape=jax.ShapeDtypeStruct((nb * t, NSA_G * 2 * LANES), BF16),
        scratch_shapes=[pltpu.VMEM((nsr, TQ), F32),
                        pltpu.VMEM((NSA_HG, 2 * (-(-max(nqt, WINDOW // TQ + 1) // 2)), TQ, TQ), F32)],
        compiler_params=_cparams(("arbitrary", "arbitrary", "arbitrary")),
        name="nsa_prompt",
    )(qn, qr, cc, cc, kvp, vt, kvp, vt, ng, cover)


def _nsa_sample_kernel(pt_ref, qn_ref, qr_ref, cc_ref, new_ref, win_ref, ng_ref, cover_ref, cache_ref,
                       out_ref, buf_ref, sem, *, past, tnew, nsb, ksel):
    b = pl.program_id(0)
    nb = pl.num_programs(0)
    npages = pt_ref.shape[1]
    page = past // npages
    nck = cc_ref.shape[2]
    rows = NSA_HEADS * SROWS
    half = rows // 2

    def copies(bb, slot):
        return [pltpu.make_async_copy(cache_ref.at[pt_ref[bb, p]], buf_ref.at[slot, p], sem.at[slot])
                for p in range(npages)]

    @pl.when(b == 0)
    def _():
        for c in copies(0, 0):
            c.start()

    @pl.when(b + 1 < nb)
    def _():
        for c in copies(b + 1, (b + 1) % 2):
            c.start()

    lane = lax.broadcasted_iota(I32, (rows, LANES), 1)
    row = lax.broadcasted_iota(I32, (rows, LANES), 0)
    tt = row % SROWS
    lo = lane < DH

    qn = jnp.concatenate([qn_ref[:, h * LANES:(h + 1) * LANES] for h in range(NSA_HEADS)], axis=0)
    qr32 = jnp.concatenate([qr_ref[:, h * LANES:(h + 1) * LANES] for h in range(NSA_HEADS)], axis=0).astype(F32)
    qr = jnp.where(row < half, qr32, pltpu.roll(qr32, DH, 1)).astype(BF16)

    nlt = nck // LANES
    o_cmp = []
    psums = []
    for g in range(NSA_G):
        ckb = cc_ref[0, g].astype(BF16)
        cvb = cc_ref[0, 2 + g].astype(BF16)
        qg = qn[g * half:(g + 1) * half]
        s = lax.dot_general(qg, ckb, (((1,), (1,)), ((), ())), preferred_element_type=F32)
        cl = lax.broadcasted_iota(I32, s.shape, 1)
        s = jnp.where(cl < nck - 1, s, -jnp.inf)
        m = jnp.max(s, axis=1, keepdims=True)
        e = jnp.exp(s - m)
        p = e / jnp.sum(e, axis=1, keepdims=True)
        o_cmp.append(jnp.dot(p.astype(BF16), cvb, preferred_element_type=F32))
        ps = p[0:SROWS]
        for h in range(1, NSA_HG):
            ps = ps + p[h * SROWS:(h + 1) * SROWS]
        psums.append(ps)
    psum = jnp.concatenate(psums, axis=0)
    cover = cover_ref[...]
    p_hi = psum.astype(BF16)
    p_lo = (psum - p_hi.astype(F32)).astype(BF16)
    imp = jnp.dot(p_hi, cover, preferred_element_type=F32) + jnp.dot(p_lo, cover, preferred_element_type=F32)
    nsl = imp.shape[1]
    sidx = lax.broadcasted_iota(I32, imp.shape, 1)
    tpos16 = past + lax.broadcasted_iota(I32, imp.shape, 0) % SROWS
    cur = tpos16 // SLC_BLOCK
    valid = sidx * SLC_BLOCK <= tpos16
    forced = (sidx == 0) | (sidx == cur) | (sidx == cur - 1)
    score = jnp.where(forced, SEL_FORCED, jnp.where(valid, imp, -jnp.inf))
    in_sel = sidx < nsb
    selected = _topk_mask(score, sidx, in_sel, 0, nsb, ksel)
    selb16 = jnp.where(selected, 0.0, NEG)
    selrows = jnp.concatenate([selb16[g * SROWS:(g + 1) * SROWS] for g in range(NSA_G) for _ in range(NSA_HG)],
                              axis=0).astype(BF16)

    slot = b % 2
    for c in copies(b, slot):
        c.wait()

    ppt = 4
    kt_rows = ppt * page
    nkt = npages // ppt
    bpt = kt_rows // SLC_BLOCK
    srow = lax.broadcasted_iota(I32, (nsl, kt_rows), 0)
    kcol = lax.broadcasted_iota(I32, (nsl, kt_rows), 1)

    def slc_tile(j, st):
        kt = jnp.concatenate([buf_ref[slot, j * ppt + i, 0] for i in range(ppt)], axis=1).astype(BF16)
        vt = jnp.concatenate([buf_ref[slot, j * ppt + i, 1] for i in range(ppt)], axis=1).astype(BF16)
        expand = (srow == j * bpt + kcol // SLC_BLOCK).astype(BF16)
        s = jnp.dot(qr, kt, preferred_element_type=F32) + jnp.dot(selrows, expand, preferred_element_type=F32)
        return _softmax_update(s, *st, vt, v_is_t=True)

    zero_state = (jnp.full((rows, 1), NEG, F32), jnp.zeros((rows, 1), F32), jnp.zeros((rows, LANES), F32))
    st = lax.fori_loop(0, nkt, slc_tile, zero_state)
    newk = new_ref[:, 2 * LANES:3 * LANES].astype(BF16)
    newv = new_ref[:, 3 * LANES:4 * LANES].astype(BF16)
    kk = lax.broadcasted_iota(I32, (rows, SROWS), 1)
    tt8 = lax.broadcasted_iota(I32, (rows, SROWS), 0) % SROWS
    new_ok = (kk <= tt8) & (kk < tnew)
    nb_new = past // SLC_BLOCK
    bias_new = selrows[:, nb_new:nb_new + 1].astype(F32)
    s = lax.dot_general(qr, newk, (((1,), (1,)), ((), ())), preferred_element_type=F32)
    s = jnp.where(new_ok, s + bias_new, NEG)
    m, l, acc = _softmax_update(s, *st, newv)
    o_slc = acc / l

    wb = win_ref.shape[3]
    wk = win_ref[0, 0].astype(BF16)
    wv = win_ref[0, 1].astype(BF16)
    wi = lax.broadcasted_iota(I32, (rows, wb), 1)
    wt = lax.broadcasted_iota(I32, (rows, wb), 0) % SROWS
    w_ok = (wt + wb - wi < WINDOW) if wb >= WINDOW else (wi >= 0)
    s = jnp.dot(qr, wk, preferred_element_type=F32)
    s = jnp.where(w_ok, s, NEG)
    st = _softmax_update(s, *zero_state, wv, v_is_t=True)
    nwk = new_ref[:, 4 * LANES:5 * LANES].astype(BF16)
    nwv = new_ref[:, 5 * LANES:6 * LANES].astype(BF16)
    s = lax.dot_general(qr, nwk, (((1,), (1,)), ((), ())), preferred_element_type=F32)
    s = jnp.where(new_ok, s, NEG)
    m, l, acc = _softmax_update(s, *st, nwv)
    o_win = acc / l

    gates = _sigmoid(ng_ref[...])
    outs = []
    for hh in range(NSA_HEADS):
        g, h = divmod(hh, NSA_HG)
        gl = gates[:, g * LANES:(g + 1) * LANES]
        g0 = gl[:, h:h + 1]
        g1 = gl[:, NSA_HG + h:NSA_HG + h + 1]
        g2 = gl[:, 2 * NSA_HG + h:2 * NSA_HG + h + 1]
        r0 = hh * SROWS
        oc = o_cmp[g][h * SROWS:(h + 1) * SROWS]
        outs.append(g0 * oc + g1 * o_slc[r0:r0 + SROWS] + g2 * o_win[r0:r0 + SROWS])
    lo8 = lax.broadcasted_iota(I32, (SROWS, LANES), 1) < DH
    out_ref[:, 0 * LANES:1 * LANES] = jnp.where(lo8, outs[0], pltpu.roll(outs[1], DH, 1)).astype(BF16)
    out_ref[:, 1 * LANES:2 * LANES] = jnp.where(lo8, outs[2], pltpu.roll(outs[3], DH, 1)).astype(BF16)
    out_ref[:, 2 * LANES:3 * LANES] = jnp.where(lo8, pltpu.roll(outs[4], DH, 1), outs[5]).astype(BF16)
    out_ref[:, 3 * LANES:4 * LANES] = jnp.where(lo8, pltpu.roll(outs[6], DH, 1), outs[7]).astype(BF16)


def _nsa_sample(qn, qr, cc, kv, win_t, ng, slc_t, page_table, past, tnew, row0):
    nb, npages = page_table.shape
    page = slc_t.shape[-1]
    assert page * npages == past and npages % 4 == 0 and page % SLC_BLOCK == 0
    nck = past // CMP_STRIDE
    nsb = -(-(past + tnew) // SLC_BLOCK)
    nsl = -(-nsb // LANES) * LANES
    assert (nck - 2) * CMP_STRIDE + 2 * CMP_STRIDE - 1 <= past and (past + tnew) // CMP_STRIDE == nck
    assert tnew <= SROWS and past // SLC_BLOCK == nsb - 1
    cover = _cover_matrix(nck - 1, nsb, 0, nck, nsl)
    wb = win_t.shape[3]
    kern = functools.partial(_nsa_sample_kernel, past=past, tnew=tnew, nsb=nsb, ksel=min(SLC_TOPK, nsb))
    rb = row0 // SROWS
    return pl.pallas_call(
        kern,
        grid_spec=pltpu.PrefetchScalarGridSpec(
            num_scalar_prefetch=1,
            grid=(nb,),
            in_specs=[pl.BlockSpec((SROWS, NSA_HEADS * LANES), lambda b, pt: (rb + b, 0)),
                      pl.BlockSpec((SROWS, NSA_HEADS * LANES), lambda b, pt: (rb + b, 0)),
                      pl.BlockSpec((1, 4, nck, LANES), lambda b, pt: (b, 0, 0, 0)),
                      pl.BlockSpec((SROWS, 6 * LANES), lambda b, pt: (rb + b, 0)),
                      pl.BlockSpec((1, 2, NSA_G * DH, wb), lambda b, pt: (b, 0, 0, 0)),
                      pl.BlockSpec((SROWS, 2 * LANES), lambda b, pt: (rb + b, 0)),
                      pl.BlockSpec((nck, nsl), lambda b, pt: (0, 0)),
                      pl.BlockSpec(memory_space=pl.ANY)],
            out_specs=pl.BlockSpec((SROWS, NSA_HEADS * DH), lambda b, pt: (b, 0)),
            scratch_shapes=[pltpu.VMEM((2, npages, 2, NSA_G * DH, page), F32), pltpu.SemaphoreType.DMA((2,))]),
        out_shape=jax.ShapeDtypeStruct((nb * SROWS, NSA_HEADS * DH), BF16),
        compiler_params=_cparams(("arbitrary",)),
        name="nsa_sample",
    )(page_table, qn, qr, cc, kv, win_t, ng, cover, slc_t)


def _hgrn_kernel(hq_ref, hf_ref, hi_ref, hg_ref, lb_ref, nw_ref, s0_ref, o_ref, sout_ref, *, chunk, valid, nchunks):
    lb = lb_ref[0]
    nw = nw_ref[...]
    row = lax.broadcasted_iota(I32, (chunk, HG_D), 0)
    rvalid = row < valid

    def body(i, st):
        r0 = pl.multiple_of(i * chunk, chunk)
        hq = hq_ref[pl.ds(r0, chunk), :]
        hf = hf_ref[pl.ds(r0, chunk), :]
        iv = hi_ref[pl.ds(r0, chunk), :]
        hg = hg_ref[pl.ds(r0, chunk), :]
        q = hq * _sigmoid(hq)
        f = lb + (1.0 - lb) * _sigmoid(hf)
        logf = jnp.where(rvalid, jnp.log(f), 0.0)
        k = jnp.where(rvalid, 1.0 - f, 0.0)
        b = logf
        sh = 1
        while sh < chunk:
            b = b + jnp.where(row >= sh, pltpu.roll(b, sh, 0), 0.0)
            sh *= 2
        o_in = jnp.zeros((chunk, HG_D), F32)
        for s in range(valid):
            e = jnp.exp(jnp.where(row >= s, b - b[s:s + 1, :], -jnp.inf))
            a = jnp.sum(q * e * k[s:s + 1, :], axis=1, keepdims=True)
            o_in = o_in + a * iv[s:s + 1, :]
        qd = (q * jnp.exp(b)).astype(BF16)
        o = o_in + lax.dot_general(qd, st.astype(BF16), (((1,), (1,)), ((), ())), preferred_element_type=F32)
        bl = b[chunk - 1:chunk, :]
        kd = (k * jnp.exp(bl - b)).astype(BF16)
        u = lax.dot_general(iv.astype(BF16), kd, (((0,), (0,)), ((), ())), preferred_element_type=F32)
        st = st * jnp.exp(bl) + u
        ms = jnp.mean(o * o, axis=1, keepdims=True)
        y = o * lax.rsqrt(ms + RMS_EPS) * nw * (hg * _sigmoid(hg))
        o_ref[pl.ds(r0, chunk), :] = y.astype(BF16)
        return st

    st = lax.fori_loop(0, nchunks, body, s0_ref[0, 0].T, unroll=min(4, nchunks))
    sout_ref[0, 0] = st.T


def _hgrn(hh, lb, nw, s0, nseq, rows_per_seq, row0, chunk, valid):
    rb = row0 // rows_per_seq
    kern = functools.partial(_hgrn_kernel, chunk=chunk, valid=valid, nchunks=rows_per_seq // chunk)
    blk = lambda k: pl.BlockSpec((rows_per_seq, HG_D), lambda b, h, k=k: (rb + b, k * HG_HEADS + h))
    return pl.pallas_call(
        kern,
        grid=(nseq, HG_HEADS),
        in_specs=[blk(0), blk(1), blk(2), blk(3),
                  pl.BlockSpec((1, 1, HG_D), lambda b, h: (h, 0, 0)),
                  pl.BlockSpec((1, HG_D), lambda b, h: (0, 0)),
                  pl.BlockSpec((1, 1, HG_D, HG_D), lambda b, h: (b, h, 0, 0))],
        out_specs=[pl.BlockSpec((rows_per_seq, HG_D), lambda b, h: (b, h)),
                   pl.BlockSpec((1, 1, HG_D, HG_D), lambda b, h: (b, h, 0, 0))],
        out_shape=[jax.ShapeDtypeStruct((nseq * rows_per_seq, HG_HEADS * HG_D), BF16),
                   jax.ShapeDtypeStruct((nseq, HG_HEADS, HG_D, HG_D), F32)],
        compiler_params=_cparams(("arbitrary", "arbitrary")),
        name="hgrn",
    )(hh, hh, hh, hh, lb, nw, s0)


def _layernorm(y, g, b):
    mu = jnp.mean(y, axis=1, keepdims=True)
    yc = y - mu
    var = jnp.mean(yc * yc, axis=1, keepdims=True)
    return yc * lax.rsqrt(var + LN_EPS) * g + b


def _finish_kernel(x_ref, a_ref, hgo_ref, gg_ref, wpa_ref, wpb_ref, wo_ref, g1_ref, b1_ref, wr_ref, br_ref, ltri_ref,
                   h_ref, topi_ref, topw_ref, pos_ref, cnt_ref, carry_ref):
    i = pl.program_id(0)

    @pl.when(i == 0)
    def _():
        carry_ref[...] = jnp.zeros_like(carry_ref)

    a = jnp.dot(a_ref[...], wpa_ref[...], preferred_element_type=F32)
    bb = jnp.dot(hgo_ref[...], wpb_ref[...], preferred_element_type=F32)
    m = gg_ref[:, :D_MODEL] * a + gg_ref[:, D_MODEL:] * bb
    mix = jnp.dot(m.astype(BF16), wo_ref[...], preferred_element_type=F32)
    h = _layernorm(DN_ALPHA * x_ref[...] + mix, g1_ref[...], b1_ref[...])
    h_ref[...] = h

    h1 = h.astype(BF16)
    h2 = (h - h1.astype(F32)).astype(BF16)
    w = wr_ref[...]
    w1 = w.astype(BF16)
    w2 = (w - w1.astype(F32)).astype(BF16)
    lg = (jnp.dot(h1, w1, preferred_element_type=F32) + jnp.dot(h1, w2, preferred_element_type=F32)
          + jnp.dot(h2, w1, preferred_element_type=F32)) + br_ref[...]
    tm = lg.shape[0]
    lane = lax.broadcasted_iota(I32, (tm, LANES), 1)
    lg = jnp.where(lane < N_EXPERTS, lg, -jnp.inf)
    vals, idxs = [], []
    for _ in range(TOP_K):
        mx = jnp.max(lg, axis=1, keepdims=True)
        ix = jnp.min(jnp.where(lg == mx, lane, LANES), axis=1, keepdims=True)
        vals.append(mx)
        idxs.append(ix)
        lg = jnp.where(lane == ix, -jnp.inf, lg)
    es = [jnp.exp(v - vals[0]) for v in vals]
    den = es[0] + es[1] + es[2] + es[3]
    topi = jnp.zeros((tm, LANES), I32)
    topw = jnp.zeros((tm, LANES), F32)
    onehot = jnp.zeros((tm, LANES), F32)
    for j in range(TOP_K):
        topi = jnp.where(lane == j, idxs[j], topi)
        topw = jnp.where(lane == j, es[j] / den, topw)
        onehot = onehot + (lane == idxs[j]).astype(F32)
    topi_ref[...] = topi
    topw_ref[...] = topw
    prefix = jnp.dot(ltri_ref[...], onehot.astype(BF16), preferred_element_type=F32) + carry_ref[0:1, :]
    pos = jnp.zeros((tm, LANES), I32)
    for j in range(TOP_K):
        pj = jnp.sum(jnp.where(lane == idxs[j], prefix, 0.0), axis=1, keepdims=True)
        pos = jnp.where(lane == j, pj.astype(I32), pos)
    pos_ref[...] = pos
    carry_ref[...] = carry_ref[...] + jnp.sum(onehot, axis=0, keepdims=True)
    cnt_ref[...] = carry_ref[...]


def _finish(x, nsa_o, hg_o, gg, wpa, wpb, wo, g1, b1, wr, br):
    n = x.shape[0]
    row = lambda i: (i, 0)
    const = lambda i: (0, 0)
    ltri = jnp.asarray(np.tril(np.ones((TM, TM), np.float32), -1), dtype=BF16)
    outs = pl.pallas_call(
        _finish_kernel,
        grid=(n // TM,),
        in_specs=[pl.BlockSpec((TM, D_MODEL), row), pl.BlockSpec((TM, 512), row), pl.BlockSpec((TM, 512), row),
                  pl.BlockSpec((TM, 2 * D_MODEL), row),
                  pl.BlockSpec((512, D_MODEL), const), pl.BlockSpec((512, D_MODEL), const),
                  pl.BlockSpec((D_MODEL, D_MODEL), const),
                  pl.BlockSpec((1, D_MODEL), const), pl.BlockSpec((1, D_MODEL), const),
                  pl.BlockSpec((D_MODEL, LANES), const), pl.BlockSpec((1, LANES), const),
                  pl.BlockSpec((TM, TM), const)],
        out_specs=[pl.BlockSpec((TM, D_MODEL), row), pl.BlockSpec((TM, LANES), row), pl.BlockSpec((TM, LANES), row),
                   pl.BlockSpec((TM, LANES), row), pl.BlockSpec((8, LANES), const)],
        out_shape=[jax.ShapeDtypeStruct((n, D_MODEL), F32), jax.ShapeDtypeStruct((n, LANES), I32),
                   jax.ShapeDtypeStruct((n, LANES), F32), jax.ShapeDtypeStruct((n, LANES), I32),
                   jax.ShapeDtypeStruct((8, LANES), F32)],
        scratch_shapes=[pltpu.VMEM((8, LANES), F32)],
        compiler_params=_cparams(("arbitrary",)),
        name="finish",
    )(x, nsa_o, hg_o, gg, wpa, wpb, wo, g1, b1, wr, br, ltri)
    return outs


def _dispatch_kernel(dest_ref, h_ref, xs_in_ref, xs_ref, sem):
    del xs_in_ref

    def copy(r, j):
        d = dest_ref[0, 0, r * TOP_K + j]
        return pltpu.make_async_copy(h_ref.at[pl.ds(r, 1), :], xs_ref.at[pl.ds(d, 1), :], sem)

    def start(r, c):
        for j in range(TOP_K):
            copy(r, j).start(priority=j % 2)
        return c

    def wait(r, c):
        for j in range(TOP_K):
            copy(r, j).wait()
        return c

    lax.fori_loop(0, TD, start, 0)
    lax.fori_loop(0, TD, wait, 0)


def _dispatch(dest3, h, nrows):
    n = h.shape[0]
    xs0 = jnp.zeros((nrows, D_MODEL), F32)
    return pl.pallas_call(
        _dispatch_kernel,
        grid=(n // TD,),
        in_specs=[pl.BlockSpec((1, 1, TD * TOP_K), lambda i: (i, 0, 0), memory_space=pltpu.SMEM),
                  pl.BlockSpec((TD, D_MODEL), lambda i: (i, 0)),
                  pl.BlockSpec(memory_space=pl.ANY)],
        out_specs=pl.BlockSpec(memory_space=pl.ANY),
        out_shape=jax.ShapeDtypeStruct((nrows, D_MODEL), F32),
        scratch_shapes=[pltpu.SemaphoreType.DMA(())],
        input_output_aliases={2: 0},
        compiler_params=_cparams(("arbitrary",)),
        name="dispatch",
    )(dest3, h, xs0)


def _experts_kernel(bexp_ref, nused_ref, x_ref, wgu_ref, bgu_ref, wdn_ref, bdn_ref, y_ref):
    i = pl.program_id(0)

    @pl.when(i < nused_ref[0])
    def _():
        xb = x_ref[...].astype(BF16)
        acc = jnp.zeros((TE, D_MODEL), F32)
        cw = 256
        for c in range(D_FF // cw):
            gt = jnp.dot(xb, wgu_ref[0, :, c * cw:(c + 1) * cw], preferred_element_type=F32) + bgu_ref[0, :, c * cw:(c + 1) * cw]
            up = (jnp.dot(xb, wgu_ref[0, :, D_FF + c * cw:D_FF + (c + 1) * cw], preferred_element_type=F32)
                  + bgu_ref[0, :, D_FF + c * cw:D_FF + (c + 1) * cw])
            gt = jnp.minimum(gt, SWIGLU_LIMIT)
            up = jnp.clip(up, -SWIGLU_LIMIT, SWIGLU_LIMIT)
            act = (up + 1.0) * (gt * _sigmoid(SWIGLU_ALPHA * gt))
            acc = acc + jnp.dot(act.astype(BF16), wdn_ref[0, c * cw:(c + 1) * cw, :], preferred_element_type=F32)
        y_ref[...] = acc + bdn_ref[0]

    @pl.when(i >= nused_ref[0])
    def _():
        y_ref[...] = jnp.zeros_like(y_ref)


def _experts(bexp, nused, xs, wgu, bgu, wdn, bdn):
    nt = xs.shape[0] // TE
    return pl.pallas_call(
        _experts_kernel,
        grid_spec=pltpu.PrefetchScalarGridSpec(
            num_scalar_prefetch=2,
            grid=(nt,),
            in_specs=[pl.BlockSpec((TE, D_MODEL), lambda i, be, nu: (i, 0)),
                      pl.BlockSpec((1, D_MODEL, 2 * D_FF), lambda i, be, nu: (be[i], 0, 0)),
                      pl.BlockSpec((1, 1, 2 * D_FF), lambda i, be, nu: (be[i], 0, 0)),
                      pl.BlockSpec((1, D_FF, D_MODEL), lambda i, be, nu: (be[i], 0, 0)),
                      pl.BlockSpec((1, 1, D_MODEL), lambda i, be, nu: (be[i], 0, 0))],
            out_specs=pl.BlockSpec((TE, D_MODEL), lambda i, be, nu: (i, 0))),
        out_shape=jax.ShapeDtypeStruct(xs.shape, F32),
        compiler_params=_cparams(("arbitrary",)),
        name="experts",
    )(bexp, nused, xs, wgu, bgu, wdn, bdn)


def _combine_kernel(dest_ref, dnext_ref, h_ref, topw_ref, g2_ref, b2_ref, ys_ref, out_ref, buf_ref, sem):
    i = pl.program_id(0)
    n = pl.num_programs(0)

    def copy(d_ref, slot, r, j):
        d = d_ref[0, 0, r * TOP_K + j]
        return pltpu.make_async_copy(ys_ref.at[pl.ds(d, 1), :], buf_ref.at[slot, j, pl.ds(r, 1), :], sem.at[slot])

    def start_all(d_ref, slot):
        def body(r, c):
            for j in range(TOP_K):
                copy(d_ref, slot, r, j).start(priority=j % 2)
            return c
        lax.fori_loop(0, TD, body, 0)

    @pl.when(i == 0)
    def _():
        start_all(dest_ref, 0)

    @pl.when(i + 1 < n)
    def _():
        start_all(dnext_ref, (i + 1) % 2)

    slot = i % 2

    def wait(r, c):
        for j in range(TOP_K):
            copy(dest_ref, slot, r, j).wait()
        return c

    lax.fori_loop(0, TD, wait, 0)
    tw = topw_ref[...]
    moe = tw[:, 0:1] * buf_ref[slot, 0]
    for j in range(1, TOP_K):
        moe = moe + tw[:, j:j + 1] * buf_ref[slot, j]
    out_ref[...] = _layernorm(DN_ALPHA * h_ref[...] + moe, g2_ref[...], b2_ref[...])


def _combine(dest3, h, topw, g2, b2, ys):
    n = h.shape[0]
    last = n // TD - 1
    return pl.pallas_call(
        _combine_kernel,
        grid=(n // TD,),
        in_specs=[pl.BlockSpec((1, 1, TD * TOP_K), lambda i: (i, 0, 0), memory_space=pltpu.SMEM),
                  pl.BlockSpec((1, 1, TD * TOP_K), lambda i: (jnp.minimum(i + 1, last), 0, 0), memory_space=pltpu.SMEM),
                  pl.BlockSpec((TD, D_MODEL), lambda i: (i, 0)),
                  pl.BlockSpec((TD, LANES), lambda i: (i, 0)),
                  pl.BlockSpec((1, D_MODEL), lambda i: (0, 0)), pl.BlockSpec((1, D_MODEL), lambda i: (0, 0)),
                  pl.BlockSpec(memory_space=pl.ANY)],
        out_specs=pl.BlockSpec((TD, D_MODEL), lambda i: (i, 0)),
        out_shape=jax.ShapeDtypeStruct((n, D_MODEL), F32),
        scratch_shapes=[pltpu.VMEM((2, TOP_K, TD, D_MODEL), F32), pltpu.SemaphoreType.DMA((2,))],
        compiler_params=_cparams(("arbitrary",)),
        name="combine",
    )(dest3, dest3, h, topw, g2, b2, ys)


def kernel(x_prompt, x_sample, cache_cmp_kv, cache_slc_kv, cache_win_kv, state_hgrn, page_table, w_in, pe_ck, w_ck1,
           w_ck2, pe_cv, w_cv1, w_cv2, hg_lb_logits, hg_norm_w, w_pa, w_pb, w_o, ln1_g, ln1_b, w_router, b_router,
           w_gu, b_gu, w_dn, b_dn, ln2_g, ln2_b):
    nb, t, d = x_prompt.shape
    ndb, tnew, _ = x_sample.shape
    depth, npool, page = cache_cmp_kv.shape[:3]
    past = page_table.shape[1] * page
    assert depth == 1 and d == D_MODEL and t % TM == 0 and (ndb * SROWS) % TM == 0 and tnew <= SROWS
    n_p = nb * t
    n_s = ndb * SROWS
    n = n_p + n_s

    xs_pad = jnp.pad(x_sample, ((0, 0), (0, SROWS - tnew), (0, 0)))
    x = jnp.concatenate([x_prompt.reshape(n_p, d), xs_pad.reshape(n_s, d)], axis=0)

    w_all = _repack_w_in(w_in[0])
    tables = _rope_tables(t, past)
    qn, qr, kv, kvp, ng, hh, gg = _project(x, w_all, tables, n_p // TM, t)

    cw = (_compress_weights(pe_ck[0], w_ck1[0], w_ck2[0]), _compress_weights(pe_cv[0], w_cv1[0], w_cv2[0]))
    cc_p = _compress_prompt(kv, nb, t, cw)

    def rows_minor(c):
        return jnp.transpose(c, (0, 2, 3, 4, 1)).reshape(c.shape[0], 2, NSA_G * DH, c.shape[1])

    cc_s = _compress_sample(rows_minor(cache_cmp_kv[0]), page_table, cw)
    vt = jnp.stack([kvp[:n_p, (2 * i + 1) * LANES:(2 * i + 2) * LANES].reshape(nb, t, LANES).transpose(0, 2, 1)
                    for i in range(4)], axis=0)
    nsa_p = _nsa_prompt(qn, qr, cc_p, kvp, vt, ng, nb, t)
    nsa_s = _nsa_sample(qn, qr, cc_s, kv, rows_minor(cache_win_kv[0]), ng, rows_minor(cache_slc_kv[0]),
                        page_table, past, tnew, n_p)
    nsa_o = jnp.concatenate([nsa_p, nsa_s], axis=0)

    lb_all = jnp.cumsum(jax.nn.softmax(hg_lb_logits.astype(F32), axis=0), axis=0)
    lb = lb_all[0].reshape(HG_HEADS, 1, HG_D)
    nw = hg_norm_w[0].reshape(1, HG_D)
    hg_p, st_p = _hgrn(hh, lb, nw, jnp.zeros((nb, HG_HEADS, HG_D, HG_D), F32), nb, t, 0, HG_CHUNK, HG_CHUNK)
    hg_s, st_s = _hgrn(hh, lb, nw, state_hgrn[0], ndb, SROWS, n_p, SROWS, tnew)
    hg_o = jnp.concatenate([hg_p, hg_s], axis=0)

    wr = jnp.pad(w_router[0], ((0, 0), (0, LANES - N_EXPERTS)))
    br = jnp.pad(b_router[0], (0, LANES - N_EXPERTS)).reshape(1, LANES)
    h, topi, topw, pos, cnt = _finish(x, nsa_o, hg_o, gg, w_pa[0].astype(BF16), w_pb[0].astype(BF16),
                                      w_o[0].astype(BF16), ln1_g[0].reshape(1, d), ln1_b[0].reshape(1, d), wr, br)

    counts = cnt[0, :N_EXPERTS].astype(I32)
    pcounts = (counts + TE - 1) // TE * TE
    pend = jnp.cumsum(pcounts)
    pstart = pend - pcounts
    nt = (n * TOP_K) // TE + N_EXPERTS
    bexp = jnp.minimum(jnp.sum(jnp.arange(nt, dtype=I32)[:, None] * TE >= pend[None, :], axis=1), N_EXPERTS - 1).astype(I32)
    nused = (pend[-1:] // TE).astype(I32)
    dest = pstart[topi[:, :TOP_K]] + pos[:, :TOP_K]
    dest3 = dest.reshape(n // TD, 1, TD * TOP_K)

    xs = _dispatch(dest3, h, nt * TE)
    ys = _experts(bexp, nused, xs, w_gu[0].astype(BF16), b_gu[0].reshape(N_EXPERTS, 1, 2 * D_FF),
                  w_dn[0].astype(BF16), b_dn[0].reshape(N_EXPERTS, 1, D_MODEL))
    y = _combine(dest3, h, topw, ln2_g[0].reshape(1, d), ln2_b[0].reshape(1, d), ys)

    y_p = y[:n_p].reshape(nb, t, d)
    y_s = y[n_p:].reshape(ndb, SROWS, d)[:, :tnew]
    kv_p = kv[:n_p].reshape(nb, t, 3, 2, NSA_G, DH)
    kv_s = kv[n_p:].reshape(ndb, SROWS, 3, 2, NSA_G, DH)[:, :tnew]
    keep = min(WINDOW, t)
    win_s = jnp.concatenate([cache_win_kv[0], kv_s[:, :, 2]], axis=1)[:, tnew:]
    return (y_p, y_s, kv_p[:, :, 0][None], kv_s[:, :, 0][None], kv_p[:, :, 1][None], kv_s[:, :, 1][None],
            kv_p[:, t - keep:, 2][None], win_s[None], st_p[None], st_s[None])
```

```python
import functools
import math

import numpy as np
import jax
import jax.numpy as jnp
from jax import lax
from jax.experimental import pallas as pl
from jax.experimental.pallas import tpu as pltpu

F32 = jnp.float32
BF16 = jnp.bfloat16
I32 = jnp.int32

D_MODEL = 1024
NSA_HEADS = 8
NSA_G = 2
NSA_HG = 4
DH = 64
CMP_STRIDE = 16
CMP_HIDDEN = 256
SLC_BLOCK = 64
SLC_TOPK = 16
WINDOW = 512
SEL_FORCED = 1.0e4
ROPE_THETA = 10000.0
HG_HEADS = 4
HG_D = 128
N_EXPERTS = 32
TOP_K = 4
D_FF = 1024
SWIGLU_LIMIT = 7.0
SWIGLU_ALPHA = 1.702
LN_EPS = 1e-5
RMS_EPS = 1e-6
DN_ALPHA = 2.0 ** 0.25

LANES = 128
TM = 256
TQ = 128
SROWS = 8
NSA_UNROLL = 4
HG_CHUNK = 16
TE = 256
TD = 128
NEG = -1.0e30
VMEM_LIMIT = 56 * 1024 * 1024

_C_Q, _C_KV, _C_PK, _C_NG, _C_HH, _C_GG, _C_END = 0, 1024, 1792, 2816, 3072, 5120, 7168


def _sigmoid(x):
    return 1.0 / (1.0 + jnp.exp(-x))


def _cparams(sem, vmem=VMEM_LIMIT):
    return pltpu.CompilerParams(dimension_semantics=sem, vmem_limit_bytes=vmem)


def _repack_w_in(w_in):
    d = w_in.shape[0]
    z64 = jnp.zeros((d, DH), F32)
    q = w_in[:, :512].reshape(d, NSA_HEADS, DH) * (DH ** -0.5)
    qpad = jnp.concatenate([q, jnp.zeros_like(q)], axis=-1).reshape(d, NSA_HEADS * LANES)
    kv = w_in[:, 512:1280]
    kv6 = kv.reshape(d, 6, NSA_G, DH)
    packs = []
    for kidx, vidx in ((2, 3), (4, 5)):
        for g in range(NSA_G):
            packs += [kv6[:, kidx, g], z64, kv6[:, vidx, g], z64]
    ng = w_in[:, 1280:1304].reshape(d, 3, NSA_G, NSA_HG)
    ngp = []
    for g in range(NSA_G):
        ngp += [ng[:, :, g, :].reshape(d, 3 * NSA_HG), jnp.zeros((d, LANES - 3 * NSA_HG), F32)]
    w = jnp.concatenate([qpad, kv] + packs + ngp + [w_in[:, 1304:3352], w_in[:, 3352:5400]], axis=-1)
    assert w.shape[1] == _C_END
    return w.astype(BF16)


def _rope_tables(t_prompt, past):
    half = DH // 2
    inv = ROPE_THETA ** (-jnp.arange(half, dtype=F32) / half)
    pos = jnp.concatenate([jnp.arange(t_prompt, dtype=I32), past + (jnp.arange(TM, dtype=I32) % SROWS)])
    ang = pos.astype(F32)[:, None] * inv[None, :]
    cos, sin = jnp.cos(ang), jnp.sin(ang)
    cos64 = jnp.concatenate([cos, cos], axis=-1)
    sin64 = jnp.concatenate([-sin, sin], axis=-1)
    one, zero = jnp.ones_like(cos64), jnp.zeros_like(cos64)
    cos_a = jnp.concatenate([cos64, cos64], axis=-1)
    sin_a = jnp.concatenate([sin64, sin64], axis=-1)
    cos_b = jnp.concatenate([cos64, one], axis=-1)
    sin_b = jnp.concatenate([sin64, zero], axis=-1)
    return cos_a, sin_a, cos_b, sin_b


def _proj_kernel(x_ref, w_ref, ca_ref, sa_ref, cb_ref, sb_ref,
                 qn_ref, qr_ref, kv_ref, kvp_ref, ng_ref, hh_ref, gg_ref):
    xb = x_ref[...].astype(BF16)
    tm = xb.shape[0]
    lane = lax.broadcasted_iota(I32, (tm, LANES), 1)
    first = (lane % DH) < (DH // 2)
    ca, sa, cb, sb = ca_ref[...], sa_ref[...], cb_ref[...], sb_ref[...]
    ones_hi = (lane >= DH).astype(F32)

    def rope(a, c, s):
        partner = jnp.where(first, pltpu.roll(a, LANES - DH // 2, 1), pltpu.roll(a, DH // 2, 1))
        return a * c + partner * s

    def mm(c0, n):
        return jnp.dot(xb, w_ref[:, c0:c0 + n], preferred_element_type=F32)

    for c in range(2):
        acc = mm(_C_Q + c * 512, 512)
        for j in range(4):
            a = acc[:, j * LANES:(j + 1) * LANES]
            col = c * 512 + j * LANES
            qn_ref[:, col:col + LANES] = a.astype(BF16)
            qr_ref[:, col:col + LANES] = rope(a, cb, sb).astype(BF16)
    acc = mm(_C_KV, 768)
    for j in range(6):
        a = acc[:, j * LANES:(j + 1) * LANES]
        if j in (2, 4):
            a = rope(a, ca, sa)
        kv_ref[:, j * LANES:(j + 1) * LANES] = a
    for c in range(2):
        acc = mm(_C_PK + c * 512, 512)
        for g in range(NSA_G):
            k = rope(acc[:, g * 256:g * 256 + LANES], cb, sb)
            col = c * 512 + g * 256
            kvp_ref[:, col:col + LANES] = k.astype(BF16)
            v1 = acc[:, g * 256 + LANES:g * 256 + 2 * LANES] + ones_hi
            kvp_ref[:, col + LANES:col + 2 * LANES] = v1.astype(BF16)
    ng_ref[...] = mm(_C_NG, 256)
    for c in range(4):
        hh_ref[:, c * 512:(c + 1) * 512] = mm(_C_HH + c * 512, 512)
    for c in range(4):
        gg_ref[:, c * 512:(c + 1) * 512] = _sigmoid(mm(_C_GG + c * 512, 512))


def _project(x, w_all, tables, n_prompt_tiles, t_prompt):
    n = x.shape[0]
    tp = t_prompt // TM

    def tab_map(i):
        return (jnp.where(i < n_prompt_tiles, i % tp, tp), 0)

    row = lambda i: (i, 0)
    tab_spec = pl.BlockSpec((TM, LANES), tab_map)
    widths = (1024, 1024, 768, 1024, 256, 2048, 2048)
    dtypes = (BF16, BF16, F32, BF16, F32, F32, F32)
    return pl.pallas_call(
        _proj_kernel,
        grid=(n // TM,),
        in_specs=[pl.BlockSpec((TM, D_MODEL), row),
                  pl.BlockSpec((D_MODEL, _C_END), lambda i: (0, 0))] + [tab_spec] * 4,
        out_specs=[pl.BlockSpec((TM, w), row) for w in widths],
        out_shape=[jax.ShapeDtypeStruct((n, w), dt) for w, dt in zip(widths, dtypes)],
        compiler_params=_cparams(("arbitrary",)),
        name="proj",
    )(x, w_all, *tables)


def _gelu_tanh(x):
    return x * (0.5 * (1.0 + jnp.tanh(math.sqrt(2.0 / math.pi) * (x + 0.044715 * (x * x * x)))))


def _compress_core(xk_ref, xv_ref, z_ref, w1k_ref, w1v_ref, pek_ref, pev_ref, w2k_ref, w2v_ref, out_ref):
    nch = xk_ref.shape[0] // CMP_STRIDE
    lane = lax.broadcasted_iota(I32, (nch, LANES), 1)
    lo = lane < DH
    for p, xr in enumerate((xk_ref, xv_ref)):
        for r2 in range(CMP_STRIDE // 2):
            a = xr[pl.ds(2 * r2, nch, stride=CMP_STRIDE), :]
            b = xr[pl.ds(2 * r2 + 1, nch, stride=CMP_STRIDE), :]
            z0 = jnp.where(lo, a, pltpu.roll(b, DH, 1))
            z1 = jnp.where(lo, pltpu.roll(a, DH, 1), b)
            z_ref[2 * p, :, r2 * LANES:(r2 + 1) * LANES] = z0.astype(BF16)
            z_ref[2 * p + 1, :, r2 * LANES:(r2 + 1) * LANES] = z1.astype(BF16)
    _compress_mlp(z_ref, w1k_ref, w1v_ref, pek_ref, pev_ref, w2k_ref, w2v_ref, out_ref)


def _compress_mlp(z_ref, w1k_ref, w1v_ref, pek_ref, pev_ref, w2k_ref, w2v_ref, out_ref):
    nch = z_ref.shape[1]
    for s in range(4):
        w1, pe, w2 = (w1k_ref, pek_ref, w2k_ref) if s < 2 else (w1v_ref, pev_ref, w2v_ref)
        pq = jnp.dot(z_ref[s], w1[...], preferred_element_type=F32)
        pb = jnp.dot(pe[...], w1[...], preferred_element_type=F32)
        bias = pb[0:1, :CMP_HIDDEN] + pb[8:9, CMP_HIDDEN:]
        h = pq[:, :CMP_HIDDEN] + pltpu.roll(pq[:, CMP_HIDDEN:], nch - 1, 0) + bias
        out_ref[s] = jnp.dot(_gelu_tanh(h).astype(BF16), w2[...], preferred_element_type=F32)


def _compress_prompt_kernel(xk_ref, xv_ref, w1k_ref, w1v_ref, pek_ref, pev_ref, w2k_ref, w2v_ref, out_ref, z_ref):
    _compress_core(xk_ref, xv_ref, z_ref, w1k_ref, w1v_ref, pek_ref, pev_ref, w2k_ref, w2v_ref, out_ref.at[0])


def _compress_sample_kernel(pt_ref, cache_ref, perm_ref, w1k_ref, w1v_ref, pek_ref, pev_ref, w2k_ref, w2v_ref,
                            out_ref, buf_ref, z_ref, sem):
    b = pl.program_id(0)
    nb = pl.num_programs(0)
    npages = pt_ref.shape[1]
    page = buf_ref.shape[-1]

    def copies(bb, slot):
        return [pltpu.make_async_copy(cache_ref.at[pt_ref[bb, p]], buf_ref.at[slot, p], sem.at[slot])
                for p in range(npages)]

    @pl.when(b == 0)
    def _():
        for c in copies(0, 0):
            c.start()

    @pl.when(b + 1 < nb)
    def _():
        for c in copies(b + 1, (b + 1) % 2):
            c.start()

    slot = b % 2
    for c in copies(b, slot):
        c.wait()

    cpp = page // CMP_STRIDE
    zero = jnp.zeros((DH, 2 * page), BF16)

    def build_z(pp, carry):
        c0 = pl.multiple_of(pp * 2 * cpp, 2 * cpp)
        for kv in range(2):
            xt = jnp.concatenate([buf_ref[slot, 2 * pp, kv], buf_ref[slot, 2 * pp + 1, kv]], axis=1).astype(BF16)
            for g in range(NSA_G):
                xg = xt[g * DH:(g + 1) * DH]
                y = jnp.concatenate([jnp.concatenate([xg, zero], axis=1), jnp.concatenate([zero, xg], axis=1)], axis=0)
                px = lax.dot_general(perm_ref[...], y, (((1,), (1,)), ((), ())), preferred_element_type=F32)
                for r2 in range(CMP_STRIDE // 2):
                    z_ref[2 * kv + g, pl.ds(c0, 2 * cpp), r2 * LANES:(r2 + 1) * LANES] = (
                        px[r2 * 2 * cpp:(r2 + 1) * 2 * cpp].astype(BF16))
        return carry

    lax.fori_loop(0, npages // 2, build_z, 0, unroll=2)
    _compress_mlp(z_ref, w1k_ref, w1v_ref, pek_ref, pev_ref, w2k_ref, w2v_ref, out_ref.at[0])


def _compress_weights(pe, w1, w2):
    w1cat = jnp.concatenate([w1[:CMP_STRIDE * DH], w1[CMP_STRIDE * DH:]], axis=1).astype(BF16)
    pef = pe.reshape(2, 1, CMP_STRIDE * DH)
    pe16 = jnp.broadcast_to(pef, (2, 8, CMP_STRIDE * DH)).reshape(16, CMP_STRIDE * DH).astype(BF16)
    w2cat = jnp.concatenate([w2, w2], axis=1).astype(BF16)
    return w1cat, pe16, w2cat


def _wspecs(n_grid_args):
    const = (lambda *a: (0, 0))
    return [pl.BlockSpec((CMP_STRIDE * DH, 2 * CMP_HIDDEN), const), pl.BlockSpec((CMP_STRIDE * DH, 2 * CMP_HIDDEN), const),
            pl.BlockSpec((16, CMP_STRIDE * DH), const), pl.BlockSpec((16, CMP_STRIDE * DH), const),
            pl.BlockSpec((CMP_HIDDEN, LANES), const), pl.BlockSpec((CMP_HIDDEN, LANES), const)]


def _compress_prompt(kv, nb, t, cw):
    nch = t // CMP_STRIDE
    (w1k, pek, w2k), (w1v, pev, w2v) = cw
    return pl.pallas_call(
        _compress_prompt_kernel,
        grid=(nb,),
        in_specs=[pl.BlockSpec((t, LANES), lambda b: (b, 0)), pl.BlockSpec((t, LANES), lambda b: (b, 1))] + _wspecs(1),
        out_specs=pl.BlockSpec((1, 4, nch, LANES), lambda b: (b, 0, 0, 0)),
        out_shape=jax.ShapeDtypeStruct((nb, 4, nch, LANES), F32),
        scratch_shapes=[pltpu.VMEM((4, nch, CMP_STRIDE * DH), BF16)],
        compiler_params=_cparams(("arbitrary",)),
        name="compress_prompt",
    )(kv, kv, w1k, w1v, pek, pev, w2k, w2v)


def _compress_sample(cache_t, page_table, cw):
    nb, npages = page_table.shape
    page = cache_t.shape[-1]
    assert page == LANES and cache_t.shape[1:3] == (2, NSA_G * DH)
    r = npages * page
    nch = r // CMP_STRIDE
    (w1k, pek, w2k), (w1v, pev, w2v) = cw
    wspecs = [pl.BlockSpec(s.block_shape, lambda b, pt: (0, 0)) for s in _wspecs(2)]
    cpp = page // CMP_STRIDE
    perm = np.zeros((page, 4 * page), np.float32)
    for r2 in range(CMP_STRIDE // 2):
        for pg in range(2):
            for j in range(cpp):
                m = r2 * 2 * cpp + pg * cpp + j
                perm[m, pg * page + CMP_STRIDE * j + 2 * r2] = 1.0
                perm[m, 2 * page + pg * page + CMP_STRIDE * j + 2 * r2 + 1] = 1.0
    assert npages % 4 == 0 and page == CMP_STRIDE * cpp
    return pl.pallas_call(
        _compress_sample_kernel,
        grid_spec=pltpu.PrefetchScalarGridSpec(
            num_scalar_prefetch=1,
            grid=(nb,),
            in_specs=[pl.BlockSpec(memory_space=pl.ANY), pl.BlockSpec((page, 4 * page), lambda b, pt: (0, 0))] + wspecs,
            out_specs=pl.BlockSpec((1, 4, nch, LANES), lambda b, pt: (b, 0, 0, 0)),
            scratch_shapes=[pltpu.VMEM((2, npages, 2, NSA_G * DH, page), F32),
                            pltpu.VMEM((4, nch, CMP_STRIDE * DH), BF16), pltpu.SemaphoreType.DMA((2,))]),
        out_shape=jax.ShapeDtypeStruct((nb, 4, nch, LANES), F32),
        compiler_params=_cparams(("arbitrary",)),
        name="compress_sample",
    )(page_table, cache_t, jnp.asarray(perm, dtype=BF16), w1k, w1v, pek, pev, w2k, w2v)


def _softmax_update(s, m, l, acc, v, v_is_t=False):
    m_new = jnp.maximum(m, jnp.max(s, axis=1, keepdims=True))
    alpha = jnp.exp(m - m_new)
    p = jnp.exp(s - m_new)
    l = alpha * l + jnp.sum(p, axis=1, keepdims=True)
    dims = (((1,), (1,)), ((), ())) if v_is_t else (((1,), (0,)), ((), ()))
    acc = alpha * acc + lax.dot_general(p.astype(BF16), v, dims, preferred_element_type=F32)
    return m_new, l, acc


def _topk_mask(score, sidx, in_sel, lane0, nsel, k):
    rank = jnp.zeros(score.shape, I32)
    for sp in range(nsel):
        c = score[:, lane0 + sp:lane0 + sp + 1]
        beats = (c > score) | ((c == score) & (sp < sidx))
        rank = rank + beats.astype(I32)
    return in_sel & (rank < k)


def _nsa_prompt_kernel(qn_ref, qr_ref, ck_ref, cv_ref, ks_ref, vs_ref, kw_ref, vw_ref, ng_ref, cover_ref,
                       out_ref, sel_ref, s_ref, *, nsb, ksel):
    qt = pl.program_id(2)
    t0 = qt * TQ
    nck = ck_ref.shape[2]
    krow = lax.broadcasted_iota(I32, (TQ, TQ), 0)
    qcol = lax.broadcasted_iota(I32, (TQ, TQ), 1)
    tq = t0 + qcol
    nt = (((1,), (1,)), ((), ()))

    ckb = ck_ref[0, 0].astype(BF16)
    cvt = cv_ref[0, 0].T.astype(BF16)
    cidx = lax.broadcasted_iota(I32, (nck, TQ), 0)
    cmask = (cidx * CMP_STRIDE + 2 * CMP_STRIDE - 1 <= t0 + lax.broadcasted_iota(I32, (nck, TQ), 1)) & (cidx < nck - 1)
    psum = jnp.zeros((nck, TQ), F32)
    o_cmp = []
    for h in range(NSA_HG):
        s = lax.dot_general(ckb, qn_ref[:, h * LANES:(h + 1) * LANES], nt, preferred_element_type=F32)
        s = jnp.where(cmask, s, -jnp.inf)
        m = jnp.max(s, axis=0, keepdims=True)
        m = jnp.where(m > -jnp.inf, m, 0.0)
        e = jnp.exp(s - m)
        d = jnp.sum(e, axis=0, keepdims=True)
        p = e / jnp.where(d > 0, d, 1.0)
        psum = psum + p
        o_cmp.append(jnp.dot(cvt, p.astype(BF16), preferred_element_type=F32)[0:DH])

    cover = cover_ref[...]
    p_hi = psum.astype(BF16)
    p_lo = (psum - p_hi.astype(F32)).astype(BF16)
    imp = jnp.dot(cover, p_hi, preferred_element_type=F32) + jnp.dot(cover, p_lo, preferred_element_type=F32)
    nsr = imp.shape[0]
    sidx = lax.broadcasted_iota(I32, (nsr, TQ), 0)
    tq_s = t0 + lax.broadcasted_iota(I32, (nsr, TQ), 1)
    cur = tq_s // SLC_BLOCK
    valid = sidx * SLC_BLOCK <= tq_s
    forced = (sidx == 0) | (sidx == cur) | (sidx == cur - 1)
    score = jnp.where(forced, SEL_FORCED, jnp.where(valid, imp, -jnp.inf))
    rank = jnp.zeros((nsr, TQ), I32)
    for sp in range(nsb):
        c = score[sp:sp + 1, :]
        rank = rank + ((c > score) | ((c == score) & (sp < sidx))).astype(I32)
    sel_ref[...] = jnp.where((rank < ksel) & (sidx < nsb), 0.0, NEG)

    ntiles = ks_ref.shape[0] // TQ
    qs = [qr_ref[:, h * LANES:(h + 1) * LANES] for h in range(NSA_HG)]
    vrows = DH + 8

    def attend(k_ref, v_ref, j_first, n_tiles, bias_fn):
        unroll = n_tiles if isinstance(n_tiles, int) else NSA_UNROLL
        n_it = (n_tiles + unroll - 1) // unroll

        def tile_of(i, u):
            idx = i * unroll + u
            j = j_first + idx
            jc = jnp.clip(j, 0, ntiles - 1)
            return idx, j, jc, pl.multiple_of(jc * TQ, TQ)

        def scores(i, mx):
            mx = list(mx)
            for u in range(unroll):
                idx, j, jc, r0 = tile_of(i, u)
                kt = k_ref[pl.ds(r0, TQ), :]
                bias = jnp.where(idx < n_tiles, bias_fn(j, jc), NEG)
                for h in range(NSA_HG):
                    s = lax.dot_general(kt, qs[h], nt, preferred_element_type=F32) + bias
                    s_ref[h, idx] = s
                    mx[h] = jnp.maximum(mx[h], jnp.max(s.reshape(TQ // 8, 8, TQ), axis=0))
            return tuple(mx)

        mx = lax.fori_loop(0, n_it, scores, tuple(jnp.full((8, TQ), NEG, F32) for _ in range(NSA_HG)))
        ms = [jnp.max(m8, axis=0, keepdims=True) for m8 in mx]

        def values(i, accs):
            accs = list(accs)
            for u in range(unroll):
                idx, j, jc, r0 = tile_of(i, u)
                vt = v_ref[0:vrows, pl.ds(r0, TQ)]
                for h in range(NSA_HG):
                    p = jnp.exp(s_ref[h, idx] - ms[h])
                    accs[h] = accs[h] + jnp.dot(vt, p.astype(BF16), preferred_element_type=F32)
            return tuple(accs)

        accs = lax.fori_loop(0, n_it, values, tuple(jnp.zeros((vrows, TQ), F32) for _ in range(NSA_HG)))
        return [acc[0:DH] / acc[DH:DH + 1] for acc in accs]

    diag_bias = jnp.where(krow <= qcol, 0.0, NEG)
    dq = qcol - krow

    def slc_bias(j, jc):
        sel = jnp.where(krow < SLC_BLOCK, sel_ref[pl.ds(2 * jc, 1), :], sel_ref[pl.ds(2 * jc + 1, 1), :])
        return sel + jnp.where(j == qt, diag_bias, 0.0)

    def win_bias(j, jc):
        off = t0 - jc * TQ
        return jnp.where((dq >= -off) & (dq < WINDOW - off) & (j >= 0), 0.0, NEG)

    nwin = WINDOW // TQ
    gates = _sigmoid(ng_ref[...]).T
    o_slc = attend(ks_ref, vs_ref, 0, qt + 1, slc_bias)
    o_win = attend(kw_ref, vw_ref, qt - nwin, nwin + 1, win_bias)
    tot = []
    for h in range(NSA_HG):
        tot.append(gates[h:h + 1] * o_cmp[h] + gates[NSA_HG + h:NSA_HG + h + 1] * o_slc[h]
                   + gates[2 * NSA_HG + h:2 * NSA_HG + h + 1] * o_win[h])
    out_ref[...] = jnp.concatenate(tot, axis=0).T.astype(BF16)


def _cover_matrix(nc, nsb, lane0, rows, cols):
    c = np.arange(rows)[:, None]
    s = np.arange(cols)[None, :] - lane0
    cov = ((c < nc) & (s >= 0) & (s < nsb) & (c * CMP_STRIDE < (s + 1) * SLC_BLOCK)
           & (c * CMP_STRIDE + 2 * CMP_STRIDE > s * SLC_BLOCK))
    return jnp.asarray(cov.astype(np.float32), dtype=BF16)


def _nsa_prompt(qn, qr, cc, kvp, vt, ng, nb, t):
    nqt = t // TQ
    nch = t // CMP_STRIDE
    nsb = t // SLC_BLOCK
    nsr = 32
    assert nsb <= nsr and SLC_BLOCK * 2 == TQ
    cover = _cover_matrix(nch - 1, nsb, 0, nch, nsr).T
    kern = functools.partial(_nsa_prompt_kernel, nsb=nsb, ksel=min(SLC_TOPK, nsb))
    return pl.pallas_call(
        kern,
        grid=(nb, NSA_G, nqt),
        in_specs=[pl.BlockSpec((TQ, NSA_HG * LANES), lambda b, g, q: (b * nqt + q, g)),
                  pl.BlockSpec((TQ, NSA_HG * LANES), lambda b, g, q: (b * nqt + q, g)),
                  pl.BlockSpec((1, 1, nch, LANES), lambda b, g, q: (b, g, 0, 0)),
                  pl.BlockSpec((1, 1, nch, LANES), lambda b, g, q: (b, 2 + g, 0, 0)),
                  pl.BlockSpec((t, LANES), lambda b, g, q: (b, 2 * g)),
                  pl.BlockSpec((None, None, LANES, t), lambda b, g, q: (g, b, 0, 0)),
                  pl.BlockSpec((t, LANES), lambda b, g, q: (b, 4 + 2 * g)),
                  pl.BlockSpec((None, None, LANES, t), lambda b, g, q: (2 + g, b, 0, 0)),
                  pl.BlockSpec((TQ, LANES), lambda b, g, q: (b * nqt + q, g)),
                  pl.BlockSpec((nsr, nch), lambda b, g, q: (0, 0))],
        out_specs=pl.BlockSpec((TQ, 2 * LANES), lambda b, g, q: (b * nqt + q, g)),
        out_shape=jax.ShapeDtypeStruct((nb * t, NSA_G * 2 * LANES), BF16),
        scratch_shapes=[pltpu.VMEM((nsr, TQ), F32),
                        pltpu.VMEM((NSA_HG, NSA_UNROLL * (-(-max(nqt, WINDOW // TQ + 1) // NSA_UNROLL)), TQ, TQ), F32)],
        compiler_params=_cparams(("arbitrary", "arbitrary", "arbitrary")),
        name="nsa_prompt",
    )(qn, qr, cc, cc, kvp, vt, kvp, vt, ng, cover)


def _nsa_sample_kernel(pt_ref, qn_ref, qr_ref, cc_ref, new_ref, win_ref, ng_ref, cover_ref, expand_ref, cache_ref,
                       out_ref, buf_ref, s_ref, sem, *, past, tnew, nsb, ksel):
    b = pl.program_id(0)
    nb = pl.num_programs(0)
    npages = pt_ref.shape[1]
    page = past // npages
    nck = cc_ref.shape[2]
    rows = NSA_HEADS * SROWS
    half = rows // 2

    def copies(bb, slot):
        return [pltpu.make_async_copy(cache_ref.at[pt_ref[bb, p]], buf_ref.at[slot, p], sem.at[slot])
                for p in range(npages)]

    @pl.when(b == 0)
    def _():
        for c in copies(0, 0):
            c.start()

    @pl.when(b + 1 < nb)
    def _():
        for c in copies(b + 1, (b + 1) % 2):
            c.start()

    lane = lax.broadcasted_iota(I32, (rows, LANES), 1)
    row = lax.broadcasted_iota(I32, (rows, LANES), 0)
    tt = row % SROWS
    lo = lane < DH

    qn = jnp.concatenate([qn_ref[:, h * LANES:(h + 1) * LANES] for h in range(NSA_HEADS)], axis=0)
    qr32 = jnp.concatenate([qr_ref[:, h * LANES:(h + 1) * LANES] for h in range(NSA_HEADS)], axis=0).astype(F32)
    qr = jnp.where(row < half, qr32, pltpu.roll(qr32, DH, 1)).astype(BF16)

    nlt = nck // LANES
    o_cmp = []
    psums = []
    for g in range(NSA_G):
        ckb = cc_ref[0, g].astype(BF16)
        cvb = cc_ref[0, 2 + g].astype(BF16)
        qg = qn[g * half:(g + 1) * half]
        s = lax.dot_general(qg, ckb, (((1,), (1,)), ((), ())), preferred_element_type=F32)
        cl = lax.broadcasted_iota(I32, s.shape, 1)
        s = jnp.where(cl < nck - 1, s, -jnp.inf)
        m = jnp.max(s, axis=1, keepdims=True)
        e = jnp.exp(s - m)
        p = e / jnp.sum(e, axis=1, keepdims=True)
        o_cmp.append(jnp.dot(p.astype(BF16), cvb, preferred_element_type=F32))
        ps = p[0:SROWS]
        for h in range(1, NSA_HG):
            ps = ps + p[h * SROWS:(h + 1) * SROWS]
        psums.append(ps)
    psum = jnp.concatenate(psums, axis=0)
    cover = cover_ref[...]
    p_hi = psum.astype(BF16)
    p_lo = (psum - p_hi.astype(F32)).astype(BF16)
    imp = jnp.dot(p_hi, cover, preferred_element_type=F32) + jnp.dot(p_lo, cover, preferred_element_type=F32)
    nsl = imp.shape[1]
    sidx = lax.broadcasted_iota(I32, imp.shape, 1)
    tpos16 = past + lax.broadcasted_iota(I32, imp.shape, 0) % SROWS
    cur = tpos16 // SLC_BLOCK
    valid = sidx * SLC_BLOCK <= tpos16
    forced = (sidx == 0) | (sidx == cur) | (sidx == cur - 1)
    score = jnp.where(forced, SEL_FORCED, jnp.where(valid, imp, -jnp.inf))
    in_sel = sidx < nsb
    selected = _topk_mask(score, sidx, in_sel, 0, nsb, ksel)
    selb16 = jnp.where(selected, 0.0, NEG)
    selrows = jnp.concatenate([selb16[g * SROWS:(g + 1) * SROWS] for g in range(NSA_G) for _ in range(NSA_HG)],
                              axis=0).astype(BF16)

    slot = b % 2
    for c in copies(b, slot):
        c.wait()

    ppt = 4
    kt_rows = ppt * page
    nkt = npages // ppt
    unroll = 2 if nkt % 2 == 0 else 1
    nt = (((1,), (1,)), ((), ()))

    def lane_fold(x, op):
        out = x[:, 0:LANES]
        for i in range(1, kt_rows // LANES):
            out = op(out, x[:, i * LANES:(i + 1) * LANES])
        return out

    newk = new_ref[:, 2 * LANES:3 * LANES].astype(BF16)
    newv = new_ref[:, 3 * LANES:4 * LANES].astype(BF16)
    kk = lax.broadcasted_iota(I32, (rows, SROWS), 1)
    tt8 = lax.broadcasted_iota(I32, (rows, SROWS), 0) % SROWS
    new_ok = (kk <= tt8) & (kk < tnew)
    nb_new = past // SLC_BLOCK
    bias_new = selrows[:, nb_new:nb_new + 1].astype(F32)
    s_new = jnp.where(new_ok, lax.dot_general(qr, newk, nt, preferred_element_type=F32) + bias_new, NEG)

    def scores(i, mx):
        for u in range(unroll):
            j = i * unroll + u
            kt = jnp.concatenate([buf_ref[slot, j * ppt + a, 0] for a in range(ppt)], axis=1).astype(BF16)
            c0 = pl.multiple_of(j * kt_rows, kt_rows)
            s = (jnp.dot(qr, kt, preferred_element_type=F32)
                 + jnp.dot(selrows, expand_ref[:, pl.ds(c0, kt_rows)], preferred_element_type=F32))
            s_ref[j] = s
            mx = jnp.maximum(mx, lane_fold(s, jnp.maximum))
        return mx

    mx = lax.fori_loop(0, nkt // unroll, scores, jnp.full((rows, LANES), NEG, F32))
    m = jnp.maximum(jnp.max(mx, axis=1, keepdims=True), jnp.max(s_new, axis=1, keepdims=True))

    def values(i, st):
        acc, ls = st
        for u in range(unroll):
            j = i * unroll + u
            vt = jnp.concatenate([buf_ref[slot, j * ppt + a, 1] for a in range(ppt)], axis=1).astype(BF16)
            p = jnp.exp(s_ref[j] - m)
            acc = acc + lax.dot_general(p.astype(BF16), vt, nt, preferred_element_type=F32)
            ls = ls + lane_fold(p, jnp.add)
        return acc, ls

    acc, ls = lax.fori_loop(0, nkt // unroll, values,
                            (jnp.zeros((rows, LANES), F32), jnp.zeros((rows, LANES), F32)))
    p_new = jnp.exp(s_new - m)
    acc = acc + jnp.dot(p_new.astype(BF16), newv, preferred_element_type=F32)
    o_slc = acc / (jnp.sum(ls, axis=1, keepdims=True) + jnp.sum(p_new, axis=1, keepdims=True))
    zero_state = (jnp.full((rows, 1), NEG, F32), jnp.zeros((rows, 1), F32), jnp.zeros((rows, LANES), F32))

    wb = win_ref.shape[3]
    wk = win_ref[0, 0].astype(BF16)
    wv = win_ref[0, 1].astype(BF16)
    wi = lax.broadcasted_iota(I32, (rows, wb), 1)
    wt = lax.broadcasted_iota(I32, (rows, wb), 0) % SROWS
    w_ok = (wt + wb - wi < WINDOW) if wb >= WINDOW else (wi >= 0)
    s = jnp.dot(qr, wk, preferred_element_type=F32)
    s = jnp.where(w_ok, s, NEG)
    st = _softmax_update(s, *zero_state, wv, v_is_t=True)
    nwk = new_ref[:, 4 * LANES:5 * LANES].astype(BF16)
    nwv = new_ref[:, 5 * LANES:6 * LANES].astype(BF16)
    s = lax.dot_general(qr, nwk, (((1,), (1,)), ((), ())), preferred_element_type=F32)
    s = jnp.where(new_ok, s, NEG)
    m, l, acc = _softmax_update(s, *st, nwv)
    o_win = acc / l

    gates = _sigmoid(ng_ref[...])
    outs = []
    for hh in range(NSA_HEADS):
        g, h = divmod(hh, NSA_HG)
        gl = gates[:, g * LANES:(g + 1) * LANES]
        g0 = gl[:, h:h + 1]
        g1 = gl[:, NSA_HG + h:NSA_HG + h + 1]
        g2 = gl[:, 2 * NSA_HG + h:2 * NSA_HG + h + 1]
        r0 = hh * SROWS
        oc = o_cmp[g][h * SROWS:(h + 1) * SROWS]
        outs.append(g0 * oc + g1 * o_slc[r0:r0 + SROWS] + g2 * o_win[r0:r0 + SROWS])
    lo8 = lax.broadcasted_iota(I32, (SROWS, LANES), 1) < DH
    out_ref[:, 0 * LANES:1 * LANES] = jnp.where(lo8, outs[0], pltpu.roll(outs[1], DH, 1)).astype(BF16)
    out_ref[:, 1 * LANES:2 * LANES] = jnp.where(lo8, outs[2], pltpu.roll(outs[3], DH, 1)).astype(BF16)
    out_ref[:, 2 * LANES:3 * LANES] = jnp.where(lo8, pltpu.roll(outs[4], DH, 1), outs[5]).astype(BF16)
    out_ref[:, 3 * LANES:4 * LANES] = jnp.where(lo8, pltpu.roll(outs[6], DH, 1), outs[7]).astype(BF16)


def _nsa_sample(qn, qr, cc, kv, win_t, ng, slc_t, page_table, past, tnew, row0):
    nb, npages = page_table.shape
    page = slc_t.shape[-1]
    assert page * npages == past and npages % 4 == 0 and page % SLC_BLOCK == 0
    nck = past // CMP_STRIDE
    nsb = -(-(past + tnew) // SLC_BLOCK)
    nsl = -(-nsb // LANES) * LANES
    assert (nck - 2) * CMP_STRIDE + 2 * CMP_STRIDE - 1 <= past and (past + tnew) // CMP_STRIDE == nck
    assert tnew <= SROWS and past // SLC_BLOCK == nsb - 1
    cover = _cover_matrix(nck - 1, nsb, 0, nck, nsl)
    expand = jnp.asarray((np.arange(nsl)[:, None] == np.arange(past)[None, :] // SLC_BLOCK).astype(np.float32), dtype=BF16)
    wb = win_t.shape[3]
    kern = functools.partial(_nsa_sample_kernel, past=past, tnew=tnew, nsb=nsb, ksel=min(SLC_TOPK, nsb))
    rb = row0 // SROWS
    return pl.pallas_call(
        kern,
        grid_spec=pltpu.PrefetchScalarGridSpec(
            num_scalar_prefetch=1,
            grid=(nb,),
            in_specs=[pl.BlockSpec((SROWS, NSA_HEADS * LANES), lambda b, pt: (rb + b, 0)),
                      pl.BlockSpec((SROWS, NSA_HEADS * LANES), lambda b, pt: (rb + b, 0)),
                      pl.BlockSpec((1, 4, nck, LANES), lambda b, pt: (b, 0, 0, 0)),
                      pl.BlockSpec((SROWS, 6 * LANES), lambda b, pt: (rb + b, 0)),
                      pl.BlockSpec((1, 2, NSA_G * DH, wb), lambda b, pt: (b, 0, 0, 0)),
                      pl.BlockSpec((SROWS, 2 * LANES), lambda b, pt: (rb + b, 0)),
                      pl.BlockSpec((nck, nsl), lambda b, pt: (0, 0)),
                      pl.BlockSpec((nsl, past), lambda b, pt: (0, 0)),
                      pl.BlockSpec(memory_space=pl.ANY)],
            out_specs=pl.BlockSpec((SROWS, NSA_HEADS * DH), lambda b, pt: (b, 0)),
            scratch_shapes=[pltpu.VMEM((2, npages, 2, NSA_G * DH, page), F32),
                            pltpu.VMEM((npages // 4, NSA_HEADS * SROWS, 4 * page), F32),
                            pltpu.SemaphoreType.DMA((2,))]),
        out_shape=jax.ShapeDtypeStruct((nb * SROWS, NSA_HEADS * DH), BF16),
        compiler_params=_cparams(("arbitrary",)),
        name="nsa_sample",
    )(page_table, qn, qr, cc, kv, win_t, ng, cover, expand, slc_t)


def _hgrn_kernel(hq_ref, hf_ref, hi_ref, hg_ref, lb_ref, nw_ref, s0_ref, o_ref, sout_ref, *, chunk, valid, nchunks):
    lb = lb_ref[0]
    nw = nw_ref[...]
    row = lax.broadcasted_iota(I32, (chunk, HG_D), 0)
    rvalid = row < valid

    def body(i, st):
        r0 = pl.multiple_of(i * chunk, chunk)
        hq = hq_ref[pl.ds(r0, chunk), :]
        hf = hf_ref[pl.ds(r0, chunk), :]
        iv = hi_ref[pl.ds(r0, chunk), :]
        hg = hg_ref[pl.ds(r0, chunk), :]
        q = hq * _sigmoid(hq)
        f = lb + (1.0 - lb) * _sigmoid(hf)
        logf = jnp.where(rvalid, jnp.log(f), 0.0)
        k = jnp.where(rvalid, 1.0 - f, 0.0)
        b = logf
        sh = 1
        while sh < chunk:
            b = b + jnp.where(row >= sh, pltpu.roll(b, sh, 0), 0.0)
            sh *= 2
        o_in = jnp.zeros((chunk, HG_D), F32)
        for s in range(valid):
            e = jnp.exp(jnp.where(row >= s, b - b[s:s + 1, :], -jnp.inf))
            a = jnp.sum(q * e * k[s:s + 1, :], axis=1, keepdims=True)
            o_in = o_in + a * iv[s:s + 1, :]
        qd = (q * jnp.exp(b)).astype(BF16)
        o = o_in + lax.dot_general(qd, st.astype(BF16), (((1,), (1,)), ((), ())), preferred_element_type=F32)
        bl = b[chunk - 1:chunk, :]
        kd = (k * jnp.exp(bl - b)).astype(BF16)
        u = lax.dot_general(iv.astype(BF16), kd, (((0,), (0,)), ((), ())), preferred_element_type=F32)
        st = st * jnp.exp(bl) + u
        ms = jnp.mean(o * o, axis=1, keepdims=True)
        y = o * lax.rsqrt(ms + RMS_EPS) * nw * (hg * _sigmoid(hg))
        o_ref[pl.ds(r0, chunk), :] = y.astype(BF16)
        return st

    st = lax.fori_loop(0, nchunks, body, s0_ref[0, 0].T, unroll=min(4, nchunks))
    sout_ref[0, 0] = st.T


def _hgrn(hh, lb, nw, s0, nseq, rows_per_seq, row0, chunk, valid):
    rb = row0 // rows_per_seq
    kern = functools.partial(_hgrn_kernel, chunk=chunk, valid=valid, nchunks=rows_per_seq // chunk)
    blk = lambda k: pl.BlockSpec((rows_per_seq, HG_D), lambda b, h, k=k: (rb + b, k * HG_HEADS + h))
    return pl.pallas_call(
        kern,
        grid=(nseq, HG_HEADS),
        in_specs=[blk(0), blk(1), blk(2), blk(3),
                  pl.BlockSpec((1, 1, HG_D), lambda b, h: (h, 0, 0)),
                  pl.BlockSpec((1, HG_D), lambda b, h: (0, 0)),
                  pl.BlockSpec((1, 1, HG_D, HG_D), lambda b, h: (b, h, 0, 0))],
        out_specs=[pl.BlockSpec((rows_per_seq, HG_D), lambda b, h: (b, h)),
                   pl.BlockSpec((1, 1, HG_D, HG_D), lambda b, h: (b, h, 0, 0))],
        out_shape=[jax.ShapeDtypeStruct((nseq * rows_per_seq, HG_HEADS * HG_D), BF16),
                   jax.ShapeDtypeStruct((nseq, HG_HEADS, HG_D, HG_D), F32)],
        compiler_params=_cparams(("arbitrary", "arbitrary")),
        name="hgrn",
    )(hh, hh, hh, hh, lb, nw, s0)


def _layernorm(y, g, b):
    mu = jnp.mean(y, axis=1, keepdims=True)
    yc = y - mu
    var = jnp.mean(yc * yc, axis=1, keepdims=True)
    return yc * lax.rsqrt(var + LN_EPS) * g + b


def _finish_kernel(x_ref, a_ref, hgo_ref, gg_ref, wpa_ref, wpb_ref, wo_ref, g1_ref, b1_ref, wr_ref, br_ref, ltri_ref,
                   h_ref, topi_ref, topw_ref, pos_ref, cnt_ref, carry_ref):
    i = pl.program_id(0)

    @pl.when(i == 0)
    def _():
        carry_ref[...] = jnp.zeros_like(carry_ref)

    a = jnp.dot(a_ref[...], wpa_ref[...], preferred_element_type=F32)
    bb = jnp.dot(hgo_ref[...], wpb_ref[...], preferred_element_type=F32)
    m = gg_ref[:, :D_MODEL] * a + gg_ref[:, D_MODEL:] * bb
    mix = jnp.dot(m.astype(BF16), wo_ref[...], preferred_element_type=F32)
    h = _layernorm(DN_ALPHA * x_ref[...] + mix, g1_ref[...], b1_ref[...])
    h_ref[...] = h

    h1 = h.astype(BF16)
    h2 = (h - h1.astype(F32)).astype(BF16)
    w = wr_ref[...]
    w1 = w.astype(BF16)
    w2 = (w - w1.astype(F32)).astype(BF16)
    lg = (jnp.dot(h1, w1, preferred_element_type=F32) + jnp.dot(h1, w2, preferred_element_type=F32)
          + jnp.dot(h2, w1, preferred_element_type=F32)) + br_ref[...]
    tm = lg.shape[0]
    lane = lax.broadcasted_iota(I32, (tm, LANES), 1)
    lg = jnp.where(lane < N_EXPERTS, lg, -jnp.inf)
    vals, idxs = [], []
    for _ in range(TOP_K):
        mx = jnp.max(lg, axis=1, keepdims=True)
        ix = jnp.min(jnp.where(lg == mx, lane, LANES), axis=1, keepdims=True)
        vals.append(mx)
        idxs.append(ix)
        lg = jnp.where(lane == ix, -jnp.inf, lg)
    es = [jnp.exp(v - vals[0]) for v in vals]
    den = es[0] + es[1] + es[2] + es[3]
    topi = jnp.zeros((tm, LANES), I32)
    topw = jnp.zeros((tm, LANES), F32)
    onehot = jnp.zeros((tm, LANES), F32)
    for j in range(TOP_K):
        topi = jnp.where(lane == j, idxs[j], topi)
        topw = jnp.where(lane == j, es[j] / den, topw)
        onehot = onehot + (lane == idxs[j]).astype(F32)
    topi_ref[...] = topi
    topw_ref[...] = topw
    prefix = jnp.dot(ltri_ref[...], onehot.astype(BF16), preferred_element_type=F32) + carry_ref[0:1, :]
    pos = jnp.zeros((tm, LANES), I32)
    for j in range(TOP_K):
        pj = jnp.sum(jnp.where(lane == idxs[j], prefix, 0.0), axis=1, keepdims=True)
        pos = jnp.where(lane == j, pj.astype(I32), pos)
    pos_ref[...] = pos
    carry_ref[...] = carry_ref[...] + jnp.sum(onehot, axis=0, keepdims=True)
    cnt_ref[...] = carry_ref[...]


def _finish(x, nsa_o, hg_o, gg, wpa, wpb, wo, g1, b1, wr, br):
    n = x.shape[0]
    row = lambda i: (i, 0)
    const = lambda i: (0, 0)
    ltri = jnp.asarray(np.tril(np.ones((TM, TM), np.float32), -1), dtype=BF16)
    outs = pl.pallas_call(
        _finish_kernel,
        grid=(n // TM,),
        in_specs=[pl.BlockSpec((TM, D_MODEL), row), pl.BlockSpec((TM, 512), row), pl.BlockSpec((TM, 512), row),
                  pl.BlockSpec((TM, 2 * D_MODEL), row),
                  pl.BlockSpec((512, D_MODEL), const), pl.BlockSpec((512, D_MODEL), const),
                  pl.BlockSpec((D_MODEL, D_MODEL), const),
                  pl.BlockSpec((1, D_MODEL), const), pl.BlockSpec((1, D_MODEL), const),
                  pl.BlockSpec((D_MODEL, LANES), const), pl.BlockSpec((1, LANES), const),
                  pl.BlockSpec((TM, TM), const)],
        out_specs=[pl.BlockSpec((TM, D_MODEL), row), pl.BlockSpec((TM, LANES), row), pl.BlockSpec((TM, LANES), row),
                   pl.BlockSpec((TM, LANES), row), pl.BlockSpec((8, LANES), const)],
        out_shape=[jax.ShapeDtypeStruct((n, D_MODEL), F32), jax.ShapeDtypeStruct((n, LANES), I32),
                   jax.ShapeDtypeStruct((n, LANES), F32), jax.ShapeDtypeStruct((n, LANES), I32),
                   jax.ShapeDtypeStruct((8, LANES), F32)],
        scratch_shapes=[pltpu.VMEM((8, LANES), F32)],
        compiler_params=_cparams(("arbitrary",)),
        name="finish",
    )(x, nsa_o, hg_o, gg, wpa, wpb, wo, g1, b1, wr, br, ltri)
    return outs


def _dispatch_kernel(dest_ref, h_ref, xs_in_ref, xs_ref, sem):
    del xs_in_ref

    def copy(r, j):
        d = dest_ref[0, 0, r * TOP_K + j]
        return pltpu.make_async_copy(h_ref.at[pl.ds(r, 1), :], xs_ref.at[pl.ds(d, 1), :], sem)

    def start(r, c):
        for j in range(TOP_K):
            copy(r, j).start(priority=j % 2)
        return c

    def wait(r, c):
        for j in range(TOP_K):
            copy(r, j).wait()
        return c

    lax.fori_loop(0, TD, start, 0)
    lax.fori_loop(0, TD, wait, 0)


def _dispatch(dest3, h, nrows):
    n = h.shape[0]
    xs0 = jnp.zeros((nrows, D_MODEL), F32)
    return pl.pallas_call(
        _dispatch_kernel,
        grid=(n // TD,),
        in_specs=[pl.BlockSpec((1, 1, TD * TOP_K), lambda i: (i, 0, 0), memory_space=pltpu.SMEM),
                  pl.BlockSpec((TD, D_MODEL), lambda i: (i, 0)),
                  pl.BlockSpec(memory_space=pl.ANY)],
        out_specs=pl.BlockSpec(memory_space=pl.ANY),
        out_shape=jax.ShapeDtypeStruct((nrows, D_MODEL), F32),
        scratch_shapes=[pltpu.SemaphoreType.DMA(())],
        input_output_aliases={2: 0},
        compiler_params=_cparams(("arbitrary",)),
        name="dispatch",
    )(dest3, h, xs0)


def _experts_kernel(bexp_ref, nused_ref, x_ref, wgu_ref, bgu_ref, wdn_ref, bdn_ref, y_ref):
    i = pl.program_id(0)

    @pl.when(i < nused_ref[0])
    def _():
        xb = x_ref[...].astype(BF16)
        acc = jnp.zeros((TE, D_MODEL), F32)
        cw = 256
        for c in range(D_FF // cw):
            gt = jnp.dot(xb, wgu_ref[0, :, c * cw:(c + 1) * cw], preferred_element_type=F32) + bgu_ref[0, :, c * cw:(c + 1) * cw]
            up = (jnp.dot(xb, wgu_ref[0, :, D_FF + c * cw:D_FF + (c + 1) * cw], preferred_element_type=F32)
                  + bgu_ref[0, :, D_FF + c * cw:D_FF + (c + 1) * cw])
            gt = jnp.minimum(gt, SWIGLU_LIMIT)
            up = jnp.clip(up, -SWIGLU_LIMIT, SWIGLU_LIMIT)
            act = (up + 1.0) * (gt * _sigmoid(SWIGLU_ALPHA * gt))
            acc = acc + jnp.dot(act.astype(BF16), wdn_ref[0, c * cw:(c + 1) * cw, :], preferred_element_type=F32)
        y_ref[...] = acc + bdn_ref[0]

    @pl.when(i >= nused_ref[0])
    def _():
        y_ref[...] = jnp.zeros_like(y_ref)


def _experts(bexp, nused, xs, wgu, bgu, wdn, bdn):
    nt = xs.shape[0] // TE
    return pl.pallas_call(
        _experts_kernel,
        grid_spec=pltpu.PrefetchScalarGridSpec(
            num_scalar_prefetch=2,
            grid=(nt,),
            in_specs=[pl.BlockSpec((TE, D_MODEL), lambda i, be, nu: (i, 0)),
                      pl.BlockSpec((1, D_MODEL, 2 * D_FF), lambda i, be, nu: (be[i], 0, 0)),
                      pl.BlockSpec((1, 1, 2 * D_FF), lambda i, be, nu: (be[i], 0, 0)),
                      pl.BlockSpec((1, D_FF, D_MODEL), lambda i, be, nu: (be[i], 0, 0)),
                      pl.BlockSpec((1, 1, D_MODEL), lambda i, be, nu: (be[i], 0, 0))],
            out_specs=pl.BlockSpec((TE, D_MODEL), lambda i, be, nu: (i, 0))),
        out_shape=jax.ShapeDtypeStruct(xs.shape, F32),
        compiler_params=_cparams(("arbitrary",)),
        name="experts",
    )(bexp, nused, xs, wgu, bgu, wdn, bdn)


def _combine_kernel(dest_ref, dnext_ref, h_ref, topw_ref, g2_ref, b2_ref, ys_ref, out_ref, buf_ref, sem):
    i = pl.program_id(0)
    n = pl.num_programs(0)

    def copy(d_ref, slot, r, j):
        d = d_ref[0, 0, r * TOP_K + j]
        return pltpu.make_async_copy(ys_ref.at[pl.ds(d, 1), :], buf_ref.at[slot, j, pl.ds(r, 1), :], sem.at[slot])

    def start_all(d_ref, slot):
        def body(r, c):
            for j in range(TOP_K):
                copy(d_ref, slot, r, j).start(priority=j % 2)
            return c
        lax.fori_loop(0, TD, body, 0)

    @pl.when(i == 0)
    def _():
        start_all(dest_ref, 0)

    @pl.when(i + 1 < n)
    def _():
        start_all(dnext_ref, (i + 1) % 2)

    slot = i % 2

    def wait(r, c):
        for j in range(TOP_K):
            copy(dest_ref, slot, r, j).wait()
        return c

    lax.fori_loop(0, TD, wait, 0)
    tw = topw_ref[...]
    moe = tw[:, 0:1] * buf_ref[slot, 0]
    for j in range(1, TOP_K):
        moe = moe + tw[:, j:j + 1] * buf_ref[slot, j]
    out_ref[...] = _layernorm(DN_ALPHA * h_ref[...] + moe, g2_ref[...], b2_ref[...])


def _combine(dest3, h, topw, g2, b2, ys):
    n = h.shape[0]
    last = n // TD - 1
    return pl.pallas_call(
        _combine_kernel,
        grid=(n // TD,),
        in_specs=[pl.BlockSpec((1, 1, TD * TOP_K), lambda i: (i, 0, 0), memory_space=pltpu.SMEM),
                  pl.BlockSpec((1, 1, TD * TOP_K), lambda i: (jnp.minimum(i + 1, last), 0, 0), memory_space=pltpu.SMEM),
                  pl.BlockSpec((TD, D_MODEL), lambda i: (i, 0)),
                  pl.BlockSpec((TD, LANES), lambda i: (i, 0)),
                  pl.BlockSpec((1, D_MODEL), lambda i: (0, 0)), pl.BlockSpec((1, D_MODEL), lambda i: (0, 0)),
                  pl.BlockSpec(memory_space=pl.ANY)],
        out_specs=pl.BlockSpec((TD, D_MODEL), lambda i: (i, 0)),
        out_shape=jax.ShapeDtypeStruct((n, D_MODEL), F32),
        scratch_shapes=[pltpu.VMEM((2, TOP_K, TD, D_MODEL), F32), pltpu.SemaphoreType.DMA((2,))],
        compiler_params=_cparams(("arbitrary",)),
        name="combine",
    )(dest3, dest3, h, topw, g2, b2, ys)


def kernel(x_prompt, x_sample, cache_cmp_kv, cache_slc_kv, cache_win_kv, state_hgrn, page_table, w_in, pe_ck, w_ck1,
           w_ck2, pe_cv, w_cv1, w_cv2, hg_lb_logits, hg_norm_w, w_pa, w_pb, w_o, ln1_g, ln1_b, w_router, b_router,
           w_gu, b_gu, w_dn, b_dn, ln2_g, ln2_b):
    nb, t, d = x_prompt.shape
    ndb, tnew, _ = x_sample.shape
    depth, npool, page = cache_cmp_kv.shape[:3]
    past = page_table.shape[1] * page
    assert depth == 1 and d == D_MODEL and t % TM == 0 and (ndb * SROWS) % TM == 0 and tnew <= SROWS
    n_p = nb * t
    n_s = ndb * SROWS
    n = n_p + n_s

    xs_pad = jnp.pad(x_sample, ((0, 0), (0, SROWS - tnew), (0, 0)))
    x = jnp.concatenate([x_prompt.reshape(n_p, d), xs_pad.reshape(n_s, d)], axis=0)

    w_all = _repack_w_in(w_in[0])
    tables = _rope_tables(t, past)
    qn, qr, kv, kvp, ng, hh, gg = _project(x, w_all, tables, n_p // TM, t)

    cw = (_compress_weights(pe_ck[0], w_ck1[0], w_ck2[0]), _compress_weights(pe_cv[0], w_cv1[0], w_cv2[0]))
    cc_p = _compress_prompt(kv, nb, t, cw)

    def rows_minor(c):
        return jnp.transpose(c, (0, 2, 3, 4, 1)).reshape(c.shape[0], 2, NSA_G * DH, c.shape[1])

    cc_s = _compress_sample(rows_minor(cache_cmp_kv[0]), page_table, cw)
    vt = jnp.stack([kvp[:n_p, (2 * i + 1) * LANES:(2 * i + 2) * LANES].reshape(nb, t, LANES).transpose(0, 2, 1)
                    for i in range(4)], axis=0)
    nsa_p = _nsa_prompt(qn, qr, cc_p, kvp, vt, ng, nb, t)
    nsa_s = _nsa_sample(qn, qr, cc_s, kv, rows_minor(cache_win_kv[0]), ng, rows_minor(cache_slc_kv[0]),
                        page_table, past, tnew, n_p)
    nsa_o = jnp.concatenate([nsa_p, nsa_s], axis=0)

    lb_all = jnp.cumsum(jax.nn.softmax(hg_lb_logits.astype(F32), axis=0), axis=0)
    lb = lb_all[0].reshape(HG_HEADS, 1, HG_D)
    nw = hg_norm_w[0].reshape(1, HG_D)
    hg_p, st_p = _hgrn(hh, lb, nw, jnp.zeros((nb, HG_HEADS, HG_D, HG_D), F32), nb, t, 0, HG_CHUNK, HG_CHUNK)
    hg_s, st_s = _hgrn(hh, lb, nw, state_hgrn[0], ndb, SROWS, n_p, SROWS, tnew)
    hg_o = jnp.concatenate([hg_p, hg_s], axis=0)

    wr = jnp.pad(w_router[0], ((0, 0), (0, LANES - N_EXPERTS)))
    br = jnp.pad(b_router[0], (0, LANES - N_EXPERTS)).reshape(1, LANES)
    h, topi, topw, pos, cnt = _finish(x, nsa_o, hg_o, gg, w_pa[0].astype(BF16), w_pb[0].astype(BF16),
                                      w_o[0].astype(BF16), ln1_g[0].reshape(1, d), ln1_b[0].reshape(1, d), wr, br)

    counts = cnt[0, :N_EXPERTS].astype(I32)
    pcounts = (counts + TE - 1) // TE * TE
    pend = jnp.cumsum(pcounts)
    pstart = pend - pcounts
    nt = (n * TOP_K) // TE + N_EXPERTS
    bexp = jnp.minimum(jnp.sum(jnp.arange(nt, dtype=I32)[:, None] * TE >= pend[None, :], axis=1), N_EXPERTS - 1).astype(I32)
    nused = (pend[-1:] // TE).astype(I32)
    dest = pstart[topi[:, :TOP_K]] + pos[:, :TOP_K]
    dest3 = dest.reshape(n // TD, 1, TD * TOP_K)

    xs = _dispatch(dest3, h, nt * TE)
    ys = _experts(bexp, nused, xs, w_gu[0].astype(BF16), b_gu[0].reshape(N_EXPERTS, 1, 2 * D_FF),
                  w_dn[0].astype(BF16), b_dn[0].reshape(N_EXPERTS, 1, D_MODEL))
    y = _combine(dest3, h, topw, ln2_g[0].reshape(1, d), ln2_b[0].reshape(1, d), ys)

    y_p = y[:n_p].reshape(nb, t, d)
    y_s = y[n_p:].reshape(ndb, SROWS, d)[:, :tnew]
    kv_p = kv[:n_p].reshape(nb, t, 3, 2, NSA_G, DH)
    kv_s = kv[n_p:].reshape(ndb, SROWS, 3, 2, NSA_G, DH)[:, :tnew]
    keep = min(WINDOW, t)
    win_s = jnp.concatenate([cache_win_kv[0], kv_s[:, :, 2]], axis=1)[:, tnew:]
    return (y_p, y_s, kv_p[:, :, 0][None], kv_s[:, :, 0][None], kv_p[:, :, 1][None], kv_s[:, :, 1][None],
            kv_p[:, t - keep:, 2][None], win_s[None], st_p[None], st_s[None])
```

```python
import functools
import math

import numpy as np
import jax
import jax.numpy as jnp
from jax import lax
from jax.experimental import pallas as pl
from jax.experimental.pallas import tpu as pltpu

F32 = jnp.float32
BF16 = jnp.bfloat16
I32 = jnp.int32

D_MODEL = 1024
NSA_HEADS = 8
NSA_G = 2
NSA_HG = 4
DH = 64
CMP_STRIDE = 16
CMP_HIDDEN = 256
SLC_BLOCK = 64
SLC_TOPK = 16
WINDOW = 512
SEL_FORCED = 1.0e4
ROPE_THETA = 10000.0
HG_HEADS = 4
HG_D = 128
N_EXPERTS = 32
TOP_K = 4
D_FF = 1024
SWIGLU_LIMIT = 7.0
SWIGLU_ALPHA = 1.702
LN_EPS = 1e-5
RMS_EPS = 1e-6
DN_ALPHA = 2.0 ** 0.25

LANES = 128
TM = 256
TQ = 128
SROWS = 8
NSA_UNROLL = 4
HG_CHUNK = 16
TE = 256
TD = 128
NEG = -1.0e30
VMEM_LIMIT = 56 * 1024 * 1024

_C_Q, _C_KV, _C_PK, _C_NG, _C_HH, _C_GG, _C_END = 0, 1024, 1792, 2816, 3072, 5120, 7168


def _sigmoid(x):
    return 1.0 / (1.0 + jnp.exp(-x))


def _cparams(sem, vmem=VMEM_LIMIT):
    return pltpu.CompilerParams(dimension_semantics=sem, vmem_limit_bytes=vmem)


def _repack_w_in(w_in):
    d = w_in.shape[0]
    z64 = jnp.zeros((d, DH), F32)
    q = w_in[:, :512].reshape(d, NSA_HEADS, DH) * (DH ** -0.5)
    qpad = jnp.concatenate([q, jnp.zeros_like(q)], axis=-1).reshape(d, NSA_HEADS * LANES)
    kv = w_in[:, 512:1280]
    kv6 = kv.reshape(d, 6, NSA_G, DH)
    packs = []
    for kidx, vidx in ((2, 3), (4, 5)):
        for g in range(NSA_G):
            packs += [kv6[:, kidx, g], z64, kv6[:, vidx, g], z64]
    ng = w_in[:, 1280:1304].reshape(d, 3, NSA_G, NSA_HG)
    ngp = []
    for g in range(NSA_G):
        ngp += [ng[:, :, g, :].reshape(d, 3 * NSA_HG), jnp.zeros((d, LANES - 3 * NSA_HG), F32)]
    w = jnp.concatenate([qpad, kv] + packs + ngp + [w_in[:, 1304:3352], w_in[:, 3352:5400]], axis=-1)
    assert w.shape[1] == _C_END
    return w.astype(BF16)


def _rope_tables(t_prompt, past):
    half = DH // 2
    inv = ROPE_THETA ** (-jnp.arange(half, dtype=F32) / half)
    pos = jnp.concatenate([jnp.arange(t_prompt, dtype=I32), past + (jnp.arange(TM, dtype=I32) % SROWS)])
    ang = pos.astype(F32)[:, None] * inv[None, :]
    cos, sin = jnp.cos(ang), jnp.sin(ang)
    cos64 = jnp.concatenate([cos, cos], axis=-1)
    sin64 = jnp.concatenate([-sin, sin], axis=-1)
    one, zero = jnp.ones_like(cos64), jnp.zeros_like(cos64)
    cos_a = jnp.concatenate([cos64, cos64], axis=-1)
    sin_a = jnp.concatenate([sin64, sin64], axis=-1)
    cos_b = jnp.concatenate([cos64, one], axis=-1)
    sin_b = jnp.concatenate([sin64, zero], axis=-1)
    return cos_a, sin_a, cos_b, sin_b


def _pick_rows(xp_ref, xs_ref, n_prompt_tiles):
    return jnp.where(pl.program_id(0) < n_prompt_tiles, xp_ref[...], xs_ref[...])


def _row_specs(n_prompt_tiles, width=D_MODEL):
    return [pl.BlockSpec((TM, width), lambda i: (jnp.minimum(i, n_prompt_tiles - 1), 0)),
            pl.BlockSpec((TM, width), lambda i: (jnp.maximum(i - n_prompt_tiles, 0), 0))]


def _proj_kernel(xp_ref, xs_ref, w_ref, ca_ref, sa_ref, cb_ref, sb_ref,
                 qn_ref, qr_ref, kv_ref, kvp_ref, ng_ref, hh_ref, gg_ref, *, n_prompt_tiles):
    xb = _pick_rows(xp_ref, xs_ref, n_prompt_tiles).astype(BF16)
    tm = xb.shape[0]
    lane = lax.broadcasted_iota(I32, (tm, LANES), 1)
    first = (lane % DH) < (DH // 2)
    ca, sa, cb, sb = ca_ref[...], sa_ref[...], cb_ref[...], sb_ref[...]
    ones_hi = (lane >= DH).astype(F32)

    def rope(a, c, s):
        partner = jnp.where(first, pltpu.roll(a, LANES - DH // 2, 1), pltpu.roll(a, DH // 2, 1))
        return a * c + partner * s

    def mm(c0, n):
        return jnp.dot(xb, w_ref[:, c0:c0 + n], preferred_element_type=F32)

    for c in range(2):
        acc = mm(_C_Q + c * 512, 512)
        for j in range(4):
            a = acc[:, j * LANES:(j + 1) * LANES]
            col = c * 512 + j * LANES
            qn_ref[:, col:col + LANES] = a.astype(BF16)
            qr_ref[:, col:col + LANES] = rope(a, cb, sb).astype(BF16)
    acc = mm(_C_KV, 768)
    for j in range(6):
        a = acc[:, j * LANES:(j + 1) * LANES]
        if j in (2, 4):
            a = rope(a, ca, sa)
        kv_ref[:, j * LANES:(j + 1) * LANES] = a
    for c in range(2):
        acc = mm(_C_PK + c * 512, 512)
        for g in range(NSA_G):
            k = rope(acc[:, g * 256:g * 256 + LANES], cb, sb)
            col = c * 512 + g * 256
            kvp_ref[:, col:col + LANES] = k.astype(BF16)
            v1 = acc[:, g * 256 + LANES:g * 256 + 2 * LANES] + ones_hi
            kvp_ref[:, col + LANES:col + 2 * LANES] = v1.astype(BF16)
    ng_ref[...] = mm(_C_NG, 256)
    for c in range(4):
        hh_ref[:, c * 512:(c + 1) * 512] = mm(_C_HH + c * 512, 512)
    for c in range(4):
        gg_ref[:, c * 512:(c + 1) * 512] = _sigmoid(mm(_C_GG + c * 512, 512))


def _project(xp, xs, w_all, tables, t_prompt):
    n_prompt_tiles = xp.shape[0] // TM
    n = xp.shape[0] + xs.shape[0]
    tp = t_prompt // TM

    def tab_map(i):
        return (jnp.where(i < n_prompt_tiles, i % tp, tp), 0)

    row = lambda i: (i, 0)
    tab_spec = pl.BlockSpec((TM, LANES), tab_map)
    widths = (1024, 1024, 768, 1024, 256, 2048, 2048)
    dtypes = (BF16, BF16, F32, BF16, F32, F32, F32)
    return pl.pallas_call(
        functools.partial(_proj_kernel, n_prompt_tiles=n_prompt_tiles),
        grid=(n // TM,),
        in_specs=_row_specs(n_prompt_tiles) + [pl.BlockSpec((D_MODEL, _C_END), lambda i: (0, 0))] + [tab_spec] * 4,
        out_specs=[pl.BlockSpec((TM, w), row) for w in widths],
        out_shape=[jax.ShapeDtypeStruct((n, w), dt) for w, dt in zip(widths, dtypes)],
        compiler_params=_cparams(("arbitrary",)),
        name="proj",
    )(xp, xs, w_all, *tables)


def _gelu_tanh(x):
    return x * (0.5 * (1.0 + jnp.tanh(math.sqrt(2.0 / math.pi) * (x + 0.044715 * (x * x * x)))))


def _compress_core(xk_ref, xv_ref, z_ref, w1k_ref, w1v_ref, pek_ref, pev_ref, w2k_ref, w2v_ref, out_ref):
    nch = xk_ref.shape[0] // CMP_STRIDE
    lane = lax.broadcasted_iota(I32, (nch, LANES), 1)
    lo = lane < DH
    for p, xr in enumerate((xk_ref, xv_ref)):
        for r2 in range(CMP_STRIDE // 2):
            a = xr[pl.ds(2 * r2, nch, stride=CMP_STRIDE), :]
            b = xr[pl.ds(2 * r2 + 1, nch, stride=CMP_STRIDE), :]
            z0 = jnp.where(lo, a, pltpu.roll(b, DH, 1))
            z1 = jnp.where(lo, pltpu.roll(a, DH, 1), b)
            z_ref[2 * p, :, r2 * LANES:(r2 + 1) * LANES] = z0.astype(BF16)
            z_ref[2 * p + 1, :, r2 * LANES:(r2 + 1) * LANES] = z1.astype(BF16)
    _compress_mlp(z_ref, w1k_ref, w1v_ref, pek_ref, pev_ref, w2k_ref, w2v_ref, out_ref)


def _compress_mlp(z_ref, w1k_ref, w1v_ref, pek_ref, pev_ref, w2k_ref, w2v_ref, out_ref):
    nch = z_ref.shape[1]
    for s in range(4):
        w1, pe, w2 = (w1k_ref, pek_ref, w2k_ref) if s < 2 else (w1v_ref, pev_ref, w2v_ref)
        pq = jnp.dot(z_ref[s], w1[...], preferred_element_type=F32)
        pb = jnp.dot(pe[...], w1[...], preferred_element_type=F32)
        bias = pb[0:1, :CMP_HIDDEN] + pb[8:9, CMP_HIDDEN:]
        h = pq[:, :CMP_HIDDEN] + pltpu.roll(pq[:, CMP_HIDDEN:], nch - 1, 0) + bias
        out_ref[s] = jnp.dot(_gelu_tanh(h).astype(BF16), w2[...], preferred_element_type=F32)


def _compress_prompt_kernel(xk_ref, xv_ref, w1k_ref, w1v_ref, pek_ref, pev_ref, w2k_ref, w2v_ref, out_ref, z_ref):
    _compress_core(xk_ref, xv_ref, z_ref, w1k_ref, w1v_ref, pek_ref, pev_ref, w2k_ref, w2v_ref, out_ref.at[0])


def _compress_sample_kernel(pt_ref, cache_ref, perm_ref, w1k_ref, w1v_ref, pek_ref, pev_ref, w2k_ref, w2v_ref,
                            out_ref, buf_ref, z_ref, sem):
    b = pl.program_id(0)
    nb = pl.num_programs(0)
    npages = pt_ref.shape[1]
    page = buf_ref.shape[-1]

    def copies(bb, slot):
        return [pltpu.make_async_copy(cache_ref.at[pt_ref[bb, p]], buf_ref.at[slot, p], sem.at[slot])
                for p in range(npages)]

    @pl.when(b == 0)
    def _():
        for c in copies(0, 0):
            c.start()

    @pl.when(b + 1 < nb)
    def _():
        for c in copies(b + 1, (b + 1) % 2):
            c.start()

    slot = b % 2
    for c in copies(b, slot):
        c.wait()

    cpp = page // CMP_STRIDE
    zero = jnp.zeros((DH, 2 * page), BF16)

    def build_z(pp, carry):
        c0 = pl.multiple_of(pp * 2 * cpp, 2 * cpp)
        for kv in range(2):
            xt = jnp.concatenate([buf_ref[slot, 2 * pp, kv], buf_ref[slot, 2 * pp + 1, kv]], axis=1).astype(BF16)
            for g in range(NSA_G):
                xg = xt[g * DH:(g + 1) * DH]
                y = jnp.concatenate([jnp.concatenate([xg, zero], axis=1), jnp.concatenate([zero, xg], axis=1)], axis=0)
                px = lax.dot_general(perm_ref[...], y, (((1,), (1,)), ((), ())), preferred_element_type=F32)
                for r2 in range(CMP_STRIDE // 2):
                    z_ref[2 * kv + g, pl.ds(c0, 2 * cpp), r2 * LANES:(r2 + 1) * LANES] = (
                        px[r2 * 2 * cpp:(r2 + 1) * 2 * cpp].astype(BF16))
        return carry

    lax.fori_loop(0, npages // 2, build_z, 0, unroll=2)
    _compress_mlp(z_ref, w1k_ref, w1v_ref, pek_ref, pev_ref, w2k_ref, w2v_ref, out_ref.at[0])


def _compress_weights(pe, w1, w2):
    w1cat = jnp.concatenate([w1[:CMP_STRIDE * DH], w1[CMP_STRIDE * DH:]], axis=1).astype(BF16)
    pef = pe.reshape(2, 1, CMP_STRIDE * DH)
    pe16 = jnp.broadcast_to(pef, (2, 8, CMP_STRIDE * DH)).reshape(16, CMP_STRIDE * DH).astype(BF16)
    w2cat = jnp.concatenate([w2, w2], axis=1).astype(BF16)
    return w1cat, pe16, w2cat


def _wspecs(n_grid_args):
    const = (lambda *a: (0, 0))
    return [pl.BlockSpec((CMP_STRIDE * DH, 2 * CMP_HIDDEN), const), pl.BlockSpec((CMP_STRIDE * DH, 2 * CMP_HIDDEN), const),
            pl.BlockSpec((16, CMP_STRIDE * DH), const), pl.BlockSpec((16, CMP_STRIDE * DH), const),
            pl.BlockSpec((CMP_HIDDEN, LANES), const), pl.BlockSpec((CMP_HIDDEN, LANES), const)]


def _compress_prompt(kv, nb, t, cw):
    nch = t // CMP_STRIDE
    (w1k, pek, w2k), (w1v, pev, w2v) = cw
    return pl.pallas_call(
        _compress_prompt_kernel,
        grid=(nb,),
        in_specs=[pl.BlockSpec((t, LANES), lambda b: (b, 0)), pl.BlockSpec((t, LANES), lambda b: (b, 1))] + _wspecs(1),
        out_specs=pl.BlockSpec((1, 4, nch, LANES), lambda b: (b, 0, 0, 0)),
        out_shape=jax.ShapeDtypeStruct((nb, 4, nch, LANES), F32),
        scratch_shapes=[pltpu.VMEM((4, nch, CMP_STRIDE * DH), BF16)],
        compiler_params=_cparams(("arbitrary",)),
        name="compress_prompt",
    )(kv, kv, w1k, w1v, pek, pev, w2k, w2v)


def _compress_sample(cache_t, page_table, cw):
    nb, npages = page_table.shape
    page = cache_t.shape[-1]
    assert page == LANES and cache_t.shape[1:3] == (2, NSA_G * DH)
    r = npages * page
    nch = r // CMP_STRIDE
    (w1k, pek, w2k), (w1v, pev, w2v) = cw
    wspecs = [pl.BlockSpec(s.block_shape, lambda b, pt: (0, 0)) for s in _wspecs(2)]
    cpp = page // CMP_STRIDE
    perm = np.zeros((page, 4 * page), np.float32)
    for r2 in range(CMP_STRIDE // 2):
        for pg in range(2):
            for j in range(cpp):
                m = r2 * 2 * cpp + pg * cpp + j
                perm[m, pg * page + CMP_STRIDE * j + 2 * r2] = 1.0
                perm[m, 2 * page + pg * page + CMP_STRIDE * j + 2 * r2 + 1] = 1.0
    assert npages % 4 == 0 and page == CMP_STRIDE * cpp
    return pl.pallas_call(
        _compress_sample_kernel,
        grid_spec=pltpu.PrefetchScalarGridSpec(
            num_scalar_prefetch=1,
            grid=(nb,),
            in_specs=[pl.BlockSpec(memory_space=pl.ANY), pl.BlockSpec((page, 4 * page), lambda b, pt: (0, 0))] + wspecs,
            out_specs=pl.BlockSpec((1, 4, nch, LANES), lambda b, pt: (b, 0, 0, 0)),
            scratch_shapes=[pltpu.VMEM((2, npages, 2, NSA_G * DH, page), F32),
                            pltpu.VMEM((4, nch, CMP_STRIDE * DH), BF16), pltpu.SemaphoreType.DMA((2,))]),
        out_shape=jax.ShapeDtypeStruct((nb, 4, nch, LANES), F32),
        compiler_params=_cparams(("arbitrary",)),
        name="compress_sample",
    )(page_table, cache_t, jnp.asarray(perm, dtype=BF16), w1k, w1v, pek, pev, w2k, w2v)


def _softmax_update(s, m, l, acc, v, v_is_t=False):
    m_new = jnp.maximum(m, jnp.max(s, axis=1, keepdims=True))
    alpha = jnp.exp(m - m_new)
    p = jnp.exp(s - m_new)
    l = alpha * l + jnp.sum(p, axis=1, keepdims=True)
    dims = (((1,), (1,)), ((), ())) if v_is_t else (((1,), (0,)), ((), ()))
    acc = alpha * acc + lax.dot_general(p.astype(BF16), v, dims, preferred_element_type=F32)
    return m_new, l, acc


def _topk_mask(score, sidx, in_sel, lane0, nsel, k):
    rank = jnp.zeros(score.shape, I32)
    for sp in range(nsel):
        c = score[:, lane0 + sp:lane0 + sp + 1]
        beats = (c > score) | ((c == score) & (sp < sidx))
        rank = rank + beats.astype(I32)
    return in_sel & (rank < k)


def _nsa_prompt_kernel(qn_ref, qr_ref, ck_ref, cv_ref, ks_ref, vs_ref, kw_ref, vw_ref, ng_ref, cover_ref,
                       out_ref, sel_ref, s_ref, *, nsb, ksel):
    qt = pl.program_id(2)
    t0 = qt * TQ
    nck = ck_ref.shape[2]
    krow = lax.broadcasted_iota(I32, (TQ, TQ), 0)
    qcol = lax.broadcasted_iota(I32, (TQ, TQ), 1)
    tq = t0 + qcol
    nt = (((1,), (1,)), ((), ()))

    def col_reduce(x, op):
        return op(op(x.reshape(x.shape[0] // 8, 8, x.shape[1]), axis=0), axis=0, keepdims=True)

    ckb = ck_ref[0, 0].astype(BF16)
    cvt = cv_ref[0, 0].T.astype(BF16)
    cidx = lax.broadcasted_iota(I32, (nck, TQ), 0)
    cmask = (cidx * CMP_STRIDE + 2 * CMP_STRIDE - 1 <= t0 + lax.broadcasted_iota(I32, (nck, TQ), 1)) & (cidx < nck - 1)
    psum = jnp.zeros((nck, TQ), F32)
    o_cmp = []
    for h in range(NSA_HG):
        s = lax.dot_general(ckb, qn_ref[:, h * LANES:(h + 1) * LANES], nt, preferred_element_type=F32)
        s = jnp.where(cmask, s, -jnp.inf)
        m = col_reduce(s, jnp.max)
        m = jnp.where(m > -jnp.inf, m, 0.0)
        e = jnp.exp(s - m)
        d = col_reduce(e, jnp.sum)
        p = e * (1.0 / jnp.where(d > 0, d, 1.0))
        psum = psum + p
        o_cmp.append(jnp.dot(cvt, p.astype(BF16), preferred_element_type=F32)[0:DH])

    cover = cover_ref[...]
    p_hi = psum.astype(BF16)
    p_lo = (psum - p_hi.astype(F32)).astype(BF16)
    imp = jnp.dot(cover, p_hi, preferred_element_type=F32) + jnp.dot(cover, p_lo, preferred_element_type=F32)
    nsr = imp.shape[0]
    sidx = lax.broadcasted_iota(I32, (nsr, TQ), 0)
    tq_s = t0 + lax.broadcasted_iota(I32, (nsr, TQ), 1)
    cur = tq_s // SLC_BLOCK
    valid = sidx * SLC_BLOCK <= tq_s
    forced = (sidx == 0) | (sidx == cur) | (sidx == cur - 1)
    score = jnp.where(forced, SEL_FORCED, jnp.where(valid, imp, -jnp.inf))
    rank = jnp.zeros((nsr, TQ), I32)
    for sp in range(nsb):
        c = score[sp:sp + 1, :]
        rank = rank + ((c > score) | ((c == score) & (sp < sidx))).astype(I32)
    sel_ref[...] = jnp.where((rank < ksel) & (sidx < nsb), 0.0, NEG)

    ntiles = ks_ref.shape[0] // TQ
    qs = [qr_ref[:, h * LANES:(h + 1) * LANES] for h in range(NSA_HG)]
    vrows = DH + 8

    def attend(k_ref, v_ref, j_first, n_tiles, bias_fn):
        unroll = n_tiles if isinstance(n_tiles, int) else NSA_UNROLL
        n_it = (n_tiles + unroll - 1) // unroll

        def tile_of(i, u):
            idx = i * unroll + u
            j = j_first + idx
            jc = jnp.clip(j, 0, ntiles - 1)
            return idx, j, jc, pl.multiple_of(jc * TQ, TQ)

        def scores(i, mx):
            mx = list(mx)
            for u in range(unroll):
                idx, j, jc, r0 = tile_of(i, u)
                kt = k_ref[pl.ds(r0, TQ), :]
                bias = jnp.where(idx < n_tiles, bias_fn(j, jc), NEG)
                for h in range(NSA_HG):
                    s = lax.dot_general(kt, qs[h], nt, preferred_element_type=F32) + bias
                    s_ref[h, idx] = s
                    mx[h] = jnp.maximum(mx[h], jnp.max(s.reshape(TQ // 8, 8, TQ), axis=0))
            return tuple(mx)

        mx = lax.fori_loop(0, n_it, scores, tuple(jnp.full((8, TQ), NEG, F32) for _ in range(NSA_HG)))
        ms = [jnp.max(m8, axis=0, keepdims=True) for m8 in mx]

        def values(i, accs):
            accs = list(accs)
            for u in range(unroll):
                idx, j, jc, r0 = tile_of(i, u)
                vt = v_ref[0:vrows, pl.ds(r0, TQ)]
                for h in range(NSA_HG):
                    p = jnp.exp(s_ref[h, idx] - ms[h])
                    accs[h] = accs[h] + jnp.dot(vt, p.astype(BF16), preferred_element_type=F32)
            return tuple(accs)

        accs = lax.fori_loop(0, n_it, values, tuple(jnp.zeros((vrows, TQ), F32) for _ in range(NSA_HG)))
        return [acc[0:DH] / acc[DH:DH + 1] for acc in accs]

    diag_bias = jnp.where(krow <= qcol, 0.0, NEG)
    dq = qcol - krow

    def slc_bias(j, jc):
        sel = jnp.where(krow < SLC_BLOCK, sel_ref[pl.ds(2 * jc, 1), :], sel_ref[pl.ds(2 * jc + 1, 1), :])
        return sel + jnp.where(j == qt, diag_bias, 0.0)

    def win_bias(j, jc):
        off = t0 - jc * TQ
        return jnp.where((dq >= -off) & (dq < WINDOW - off) & (j >= 0), 0.0, NEG)

    nwin = WINDOW // TQ
    gates = _sigmoid(ng_ref[...]).T
    o_slc = attend(ks_ref, vs_ref, 0, qt + 1, slc_bias)
    o_win = attend(kw_ref, vw_ref, qt - nwin, nwin + 1, win_bias)
    tot = []
    for h in range(NSA_HG):
        tot.append(gates[h:h + 1] * o_cmp[h] + gates[NSA_HG + h:NSA_HG + h + 1] * o_slc[h]
                   + gates[2 * NSA_HG + h:2 * NSA_HG + h + 1] * o_win[h])
    out_ref[...] = jnp.concatenate(tot, axis=0).T.astype(BF16)


def _cover_matrix(nc, nsb, lane0, rows, cols):
    c = np.arange(rows)[:, None]
    s = np.arange(cols)[None, :] - lane0
    cov = ((c < nc) & (s >= 0) & (s < nsb) & (c * CMP_STRIDE < (s + 1) * SLC_BLOCK)
           & (c * CMP_STRIDE + 2 * CMP_STRIDE > s * SLC_BLOCK))
    return jnp.asarray(cov.astype(np.float32), dtype=BF16)


def _nsa_prompt(qn, qr, cc, kvp, vt, ng, nb, t):
    nqt = t // TQ
    nch = t // CMP_STRIDE
    nsb = t // SLC_BLOCK
    nsr = 32
    assert nsb <= nsr and SLC_BLOCK * 2 == TQ
    cover = _cover_matrix(nch - 1, nsb, 0, nch, nsr).T
    kern = functools.partial(_nsa_prompt_kernel, nsb=nsb, ksel=min(SLC_TOPK, nsb))
    return pl.pallas_call(
        kern,
        grid=(nb, NSA_G, nqt),
        in_specs=[pl.BlockSpec((TQ, NSA_HG * LANES), lambda b, g, q: (b * nqt + q, g)),
                  pl.BlockSpec((TQ, NSA_HG * LANES), lambda b, g, q: (b * nqt + q, g)),
                  pl.BlockSpec((1, 1, nch, LANES), lambda b, g, q: (b, g, 0, 0)),
                  pl.BlockSpec((1, 1, nch, LANES), lambda b, g, q: (b, 2 + g, 0, 0)),
                  pl.BlockSpec((t, LANES), lambda b, g, q: (b, 2 * g)),
                  pl.BlockSpec((None, None, LANES, t), lambda b, g, q: (g, b, 0, 0)),
                  pl.BlockSpec((t, LANES), lambda b, g, q: (b, 4 + 2 * g)),
                  pl.BlockSpec((None, None, LANES, t), lambda b, g, q: (2 + g, b, 0, 0)),
                  pl.BlockSpec((TQ, LANES), lambda b, g, q: (b * nqt + q, g)),
                  pl.BlockSpec((nsr, nch), lambda b, g, q: (0, 0))],
        out_specs=pl.BlockSpec((TQ, 2 * LANES), lambda b, g, q: (b * nqt + q, g)),
        out_shape=jax.ShapeDtypeStruct((nb * t, NSA_G * 2 * LANES), BF16),
        scratch_shapes=[pltpu.VMEM((nsr, TQ), F32),
                        pltpu.VMEM((NSA_HG, NSA_UNROLL * (-(-max(nqt, WINDOW // TQ + 1) // NSA_UNROLL)), TQ, TQ), F32)],
        compiler_params=_cparams(("arbitrary", "arbitrary", "arbitrary")),
        name="nsa_prompt",
    )(qn, qr, cc, cc, kvp, vt, kvp, vt, ng, cover)


def _nsa_sample_kernel(pt_ref, qn_ref, qr_ref, cc_ref, new_ref, win_ref, ng_ref, cover_ref, expand_ref, cache_ref,
                       out_ref, buf_ref, s_ref, sem, *, past, tnew, nsb, ksel):
    b = pl.program_id(0)
    nb = pl.num_programs(0)
    npages = pt_ref.shape[1]
    page = past // npages
    nck = cc_ref.shape[2]
    rows = NSA_HEADS * SROWS
    half = rows // 2

    def copies(bb, slot):
        return [pltpu.make_async_copy(cache_ref.at[pt_ref[bb, p]], buf_ref.at[slot, p], sem.at[slot])
                for p in range(npages)]

    @pl.when(b == 0)
    def _():
        for c in copies(0, 0):
            c.start()

    @pl.when(b + 1 < nb)
    def _():
        for c in copies(b + 1, (b + 1) % 2):
            c.start()

    lane = lax.broadcasted_iota(I32, (rows, LANES), 1)
    row = lax.broadcasted_iota(I32, (rows, LANES), 0)
    tt = row % SROWS
    lo = lane < DH

    qn = jnp.concatenate([qn_ref[:, h * LANES:(h + 1) * LANES] for h in range(NSA_HEADS)], axis=0)
    qr32 = jnp.concatenate([qr_ref[:, h * LANES:(h + 1) * LANES] for h in range(NSA_HEADS)], axis=0).astype(F32)
    qr = jnp.where(row < half, qr32, pltpu.roll(qr32, DH, 1)).astype(BF16)

    nlt = nck // LANES
    o_cmp = []
    psums = []
    for g in range(NSA_G):
        ckb = cc_ref[0, g].astype(BF16)
        cvb = cc_ref[0, 2 + g].astype(BF16)
        qg = qn[g * half:(g + 1) * half]
        s = lax.dot_general(qg, ckb, (((1,), (1,)), ((), ())), preferred_element_type=F32)
        cl = lax.broadcasted_iota(I32, s.shape, 1)
        s = jnp.where(cl < nck - 1, s, -jnp.inf)
        m = jnp.max(s, axis=1, keepdims=True)
        e = jnp.exp(s - m)
        p = e / jnp.sum(e, axis=1, keepdims=True)
        o_cmp.append(jnp.dot(p.astype(BF16), cvb, preferred_element_type=F32))
        ps = p[0:SROWS]
        for h in range(1, NSA_HG):
            ps = ps + p[h * SROWS:(h + 1) * SROWS]
        psums.append(ps)
    psum = jnp.concatenate(psums, axis=0)
    cover = cover_ref[...]
    p_hi = psum.astype(BF16)
    p_lo = (psum - p_hi.astype(F32)).astype(BF16)
    imp = jnp.dot(p_hi, cover, preferred_element_type=F32) + jnp.dot(p_lo, cover, preferred_element_type=F32)
    nsl = imp.shape[1]
    sidx = lax.broadcasted_iota(I32, imp.shape, 1)
    tpos16 = past + lax.broadcasted_iota(I32, imp.shape, 0) % SROWS
    cur = tpos16 // SLC_BLOCK
    valid = sidx * SLC_BLOCK <= tpos16
    forced = (sidx == 0) | (sidx == cur) | (sidx == cur - 1)
    score = jnp.where(forced, SEL_FORCED, jnp.where(valid, imp, -jnp.inf))
    in_sel = sidx < nsb
    selected = _topk_mask(score, sidx, in_sel, 0, nsb, ksel)
    selb16 = jnp.where(selected, 0.0, NEG)
    selrows = jnp.concatenate([selb16[g * SROWS:(g + 1) * SROWS] for g in range(NSA_G) for _ in range(NSA_HG)],
                              axis=0).astype(BF16)

    slot = b % 2
    for c in copies(b, slot):
        c.wait()

    ppt = 4
    kt_rows = ppt * page
    nkt = npages // ppt
    unroll = 2 if nkt % 2 == 0 else 1
    nt = (((1,), (1,)), ((), ()))

    def lane_fold(x, op):
        out = x[:, 0:LANES]
        for i in range(1, kt_rows // LANES):
            out = op(out, x[:, i * LANES:(i + 1) * LANES])
        return out

    newk = new_ref[:, 2 * LANES:3 * LANES].astype(BF16)
    newv = new_ref[:, 3 * LANES:4 * LANES].astype(BF16)
    kk = lax.broadcasted_iota(I32, (rows, SROWS), 1)
    tt8 = lax.broadcasted_iota(I32, (rows, SROWS), 0) % SROWS
    new_ok = (kk <= tt8) & (kk < tnew)
    nb_new = past // SLC_BLOCK
    bias_new = selrows[:, nb_new:nb_new + 1].astype(F32)
    s_new = jnp.where(new_ok, lax.dot_general(qr, newk, nt, preferred_element_type=F32) + bias_new, NEG)

    def scores(i, mx):
        for u in range(unroll):
            j = i * unroll + u
            kt = jnp.concatenate([buf_ref[slot, j * ppt + a, 0] for a in range(ppt)], axis=1).astype(BF16)
            c0 = pl.multiple_of(j * kt_rows, kt_rows)
            s = (jnp.dot(qr, kt, preferred_element_type=F32)
                 + jnp.dot(selrows, expand_ref[:, pl.ds(c0, kt_rows)], preferred_element_type=F32))
            s_ref[j] = s
            mx = jnp.maximum(mx, lane_fold(s, jnp.maximum))
        return mx

    mx = lax.fori_loop(0, nkt // unroll, scores, jnp.full((rows, LANES), NEG, F32))
    m = jnp.maximum(jnp.max(mx, axis=1, keepdims=True), jnp.max(s_new, axis=1, keepdims=True))

    def values(i, st):
        acc, ls = st
        for u in range(unroll):
            j = i * unroll + u
            vt = jnp.concatenate([buf_ref[slot, j * ppt + a, 1] for a in range(ppt)], axis=1).astype(BF16)
            p = jnp.exp(s_ref[j] - m)
            acc = acc + lax.dot_general(p.astype(BF16), vt, nt, preferred_element_type=F32)
            ls = ls + lane_fold(p, jnp.add)
        return acc, ls

    acc, ls = lax.fori_loop(0, nkt // unroll, values,
                            (jnp.zeros((rows, LANES), F32), jnp.zeros((rows, LANES), F32)))
    p_new = jnp.exp(s_new - m)
    acc = acc + jnp.dot(p_new.astype(BF16), newv, preferred_element_type=F32)
    o_slc = acc / (jnp.sum(ls, axis=1, keepdims=True) + jnp.sum(p_new, axis=1, keepdims=True))
    zero_state = (jnp.full((rows, 1), NEG, F32), jnp.zeros((rows, 1), F32), jnp.zeros((rows, LANES), F32))

    wb = win_ref.shape[3]
    wk = win_ref[0, 0].astype(BF16)
    wv = win_ref[0, 1].astype(BF16)
    wi = lax.broadcasted_iota(I32, (rows, wb), 1)
    wt = lax.broadcasted_iota(I32, (rows, wb), 0) % SROWS
    w_ok = (wt + wb - wi < WINDOW) if wb >= WINDOW else (wi >= 0)
    s = jnp.dot(qr, wk, preferred_element_type=F32)
    s = jnp.where(w_ok, s, NEG)
    st = _softmax_update(s, *zero_state, wv, v_is_t=True)
    nwk = new_ref[:, 4 * LANES:5 * LANES].astype(BF16)
    nwv = new_ref[:, 5 * LANES:6 * LANES].astype(BF16)
    s = lax.dot_general(qr, nwk, (((1,), (1,)), ((), ())), preferred_element_type=F32)
    s = jnp.where(new_ok, s, NEG)
    m, l, acc = _softmax_update(s, *st, nwv)
    o_win = acc / l

    gates = _sigmoid(ng_ref[...])
    outs = []
    for hh in range(NSA_HEADS):
        g, h = divmod(hh, NSA_HG)
        gl = gates[:, g * LANES:(g + 1) * LANES]
        g0 = gl[:, h:h + 1]
        g1 = gl[:, NSA_HG + h:NSA_HG + h + 1]
        g2 = gl[:, 2 * NSA_HG + h:2 * NSA_HG + h + 1]
        r0 = hh * SROWS
        oc = o_cmp[g][h * SROWS:(h + 1) * SROWS]
        outs.append(g0 * oc + g1 * o_slc[r0:r0 + SROWS] + g2 * o_win[r0:r0 + SROWS])
    lo8 = lax.broadcasted_iota(I32, (SROWS, LANES), 1) < DH
    out_ref[:, 0 * LANES:1 * LANES] = jnp.where(lo8, outs[0], pltpu.roll(outs[1], DH, 1)).astype(BF16)
    out_ref[:, 1 * LANES:2 * LANES] = jnp.where(lo8, outs[2], pltpu.roll(outs[3], DH, 1)).astype(BF16)
    out_ref[:, 2 * LANES:3 * LANES] = jnp.where(lo8, pltpu.roll(outs[4], DH, 1), outs[5]).astype(BF16)
    out_ref[:, 3 * LANES:4 * LANES] = jnp.where(lo8, pltpu.roll(outs[6], DH, 1), outs[7]).astype(BF16)


def _nsa_sample(qn, qr, cc, kv, win_t, ng, slc_t, page_table, past, tnew, row0):
    nb, npages = page_table.shape
    page = slc_t.shape[-1]
    assert page * npages == past and npages % 4 == 0 and page % SLC_BLOCK == 0
    nck = past // CMP_STRIDE
    nsb = -(-(past + tnew) // SLC_BLOCK)
    nsl = -(-nsb // LANES) * LANES
    assert (nck - 2) * CMP_STRIDE + 2 * CMP_STRIDE - 1 <= past and (past + tnew) // CMP_STRIDE == nck
    assert tnew <= SROWS and past // SLC_BLOCK == nsb - 1
    cover = _cover_matrix(nck - 1, nsb, 0, nck, nsl)
    expand = jnp.asarray((np.arange(nsl)[:, None] == np.arange(past)[None, :] // SLC_BLOCK).astype(np.float32), dtype=BF16)
    wb = win_t.shape[3]
    kern = functools.partial(_nsa_sample_kernel, past=past, tnew=tnew, nsb=nsb, ksel=min(SLC_TOPK, nsb))
    rb = row0 // SROWS
    return pl.pallas_call(
        kern,
        grid_spec=pltpu.PrefetchScalarGridSpec(
            num_scalar_prefetch=1,
            grid=(nb,),
            in_specs=[pl.BlockSpec((SROWS, NSA_HEADS * LANES), lambda b, pt: (rb + b, 0)),
                      pl.BlockSpec((SROWS, NSA_HEADS * LANES), lambda b, pt: (rb + b, 0)),
                      pl.BlockSpec((1, 4, nck, LANES), lambda b, pt: (b, 0, 0, 0)),
                      pl.BlockSpec((SROWS, 6 * LANES), lambda b, pt: (rb + b, 0)),
                      pl.BlockSpec((1, 2, NSA_G * DH, wb), lambda b, pt: (b, 0, 0, 0)),
                      pl.BlockSpec((SROWS, 2 * LANES), lambda b, pt: (rb + b, 0)),
                      pl.BlockSpec((nck, nsl), lambda b, pt: (0, 0)),
                      pl.BlockSpec((nsl, past), lambda b, pt: (0, 0)),
                      pl.BlockSpec(memory_space=pl.ANY)],
            out_specs=pl.BlockSpec((SROWS, NSA_HEADS * DH), lambda b, pt: (b, 0)),
            scratch_shapes=[pltpu.VMEM((2, npages, 2, NSA_G * DH, page), F32),
                            pltpu.VMEM((npages // 4, NSA_HEADS * SROWS, 4 * page), F32),
                            pltpu.SemaphoreType.DMA((2,))]),
        out_shape=jax.ShapeDtypeStruct((nb * SROWS, NSA_HEADS * DH), BF16),
        compiler_params=_cparams(("arbitrary",)),
        name="nsa_sample",
    )(page_table, qn, qr, cc, kv, win_t, ng, cover, expand, slc_t)


def _hgrn_kernel(hq_ref, hf_ref, hi_ref, hg_ref, lb_ref, nw_ref, s0_ref, o_ref, sout_ref, *, chunk, valid, nchunks):
    heads = lb_ref.shape[0]
    for hd in range(heads):
        cols = slice(hd * HG_D, (hd + 1) * HG_D)
        _hgrn_head(hq_ref.at[:, cols], hf_ref.at[:, cols], hi_ref.at[:, cols], hg_ref.at[:, cols], lb_ref.at[hd],
                   nw_ref, s0_ref.at[0, hd], o_ref.at[:, cols], sout_ref.at[0, hd],
                   chunk=chunk, valid=valid, nchunks=nchunks)


def _hgrn_head(hq_ref, hf_ref, hi_ref, hg_ref, lb_ref, nw_ref, s0_ref, o_ref, sout_ref, *, chunk, valid, nchunks):
    lb = lb_ref[...]
    nw = nw_ref[...]
    row = lax.broadcasted_iota(I32, (chunk, HG_D), 0)
    rvalid = row < valid

    def body(i, st):
        r0 = pl.multiple_of(i * chunk, chunk)
        hq = hq_ref[pl.ds(r0, chunk), :]
        hf = hf_ref[pl.ds(r0, chunk), :]
        iv = hi_ref[pl.ds(r0, chunk), :]
        hg = hg_ref[pl.ds(r0, chunk), :]
        q = hq * _sigmoid(hq)
        f = lb + (1.0 - lb) * _sigmoid(hf)
        logf = jnp.where(rvalid, jnp.log(f), 0.0)
        k = jnp.where(rvalid, 1.0 - f, 0.0)
        b = logf
        sh = 1
        while sh < chunk:
            b = b + jnp.where(row >= sh, pltpu.roll(b, sh, 0), 0.0)
            sh *= 2
        o_in = jnp.zeros((chunk, HG_D), F32)
        for s in range(valid):
            e = jnp.exp(jnp.where(row >= s, b - b[s:s + 1, :], -jnp.inf))
            a = jnp.sum(q * e * k[s:s + 1, :], axis=1, keepdims=True)
            o_in = o_in + a * iv[s:s + 1, :]
        qd = (q * jnp.exp(b)).astype(BF16)
        o = o_in + lax.dot_general(qd, st.astype(BF16), (((1,), (1,)), ((), ())), preferred_element_type=F32)
        bl = b[chunk - 1:chunk, :]
        kd = (k * jnp.exp(bl - b)).astype(BF16)
        u = lax.dot_general(iv.astype(BF16), kd, (((0,), (0,)), ((), ())), preferred_element_type=F32)
        st = st * jnp.exp(bl) + u
        ms = jnp.mean(o * o, axis=1, keepdims=True)
        y = o * lax.rsqrt(ms + RMS_EPS) * nw * (hg * _sigmoid(hg))
        o_ref[pl.ds(r0, chunk), :] = y.astype(BF16)
        return st

    st = lax.fori_loop(0, nchunks, body, s0_ref[...].T, unroll=min(4, nchunks))
    sout_ref[...] = st.T


def _hgrn(hh, lb, nw, s0, nseq, rows_per_seq, row0, chunk, valid, heads_per_step):
    rb = row0 // rows_per_seq
    hps = heads_per_step
    ngrp = HG_HEADS // hps
    kern = functools.partial(_hgrn_kernel, chunk=chunk, valid=valid, nchunks=rows_per_seq // chunk)
    blk = lambda k: pl.BlockSpec((rows_per_seq, hps * HG_D), lambda b, h, k=k: (rb + b, k * ngrp + h))
    return pl.pallas_call(
        kern,
        grid=(nseq, ngrp),
        in_specs=[blk(0), blk(1), blk(2), blk(3),
                  pl.BlockSpec((hps, 1, HG_D), lambda b, h: (h, 0, 0)),
                  pl.BlockSpec((1, HG_D), lambda b, h: (0, 0)),
                  pl.BlockSpec((1, hps, HG_D, HG_D), lambda b, h: (b, h, 0, 0))],
        out_specs=[pl.BlockSpec((rows_per_seq, hps * HG_D), lambda b, h: (b, h)),
                   pl.BlockSpec((1, hps, HG_D, HG_D), lambda b, h: (b, h, 0, 0))],
        out_shape=[jax.ShapeDtypeStruct((nseq * rows_per_seq, HG_HEADS * HG_D), BF16),
                   jax.ShapeDtypeStruct((nseq, HG_HEADS, HG_D, HG_D), F32)],
        compiler_params=_cparams(("arbitrary", "arbitrary")),
        name="hgrn",
    )(hh, hh, hh, hh, lb, nw, s0)


def _layernorm(y, g, b):
    mu = jnp.mean(y, axis=1, keepdims=True)
    yc = y - mu
    var = jnp.mean(yc * yc, axis=1, keepdims=True)
    return yc * lax.rsqrt(var + LN_EPS) * g + b


def _finish_kernel(xp_ref, xs_ref, ap_ref, as_ref, hp_ref, hs_ref, gg_ref, wpa_ref, wpb_ref, wo_ref, g1_ref, b1_ref,
                   wr_ref, br_ref, ltri_ref, h_ref, topi_ref, topw_ref, pos_ref, cnt_ref, carry_ref, *, n_prompt_tiles):
    i = pl.program_id(0)

    @pl.when(i == 0)
    def _():
        carry_ref[...] = jnp.zeros_like(carry_ref)

    a = jnp.dot(_pick_rows(ap_ref, as_ref, n_prompt_tiles), wpa_ref[...], preferred_element_type=F32)
    bb = jnp.dot(_pick_rows(hp_ref, hs_ref, n_prompt_tiles), wpb_ref[...], preferred_element_type=F32)
    m = gg_ref[:, :D_MODEL] * a + gg_ref[:, D_MODEL:] * bb
    mix = jnp.dot(m.astype(BF16), wo_ref[...], preferred_element_type=F32)
    h = _layernorm(DN_ALPHA * _pick_rows(xp_ref, xs_ref, n_prompt_tiles) + mix, g1_ref[...], b1_ref[...])
    h_ref[...] = h

    h1 = h.astype(BF16)
    h2 = (h - h1.astype(F32)).astype(BF16)
    w = wr_ref[...]
    w1 = w.astype(BF16)
    w2 = (w - w1.astype(F32)).astype(BF16)
    lg = (jnp.dot(h1, w1, preferred_element_type=F32) + jnp.dot(h1, w2, preferred_element_type=F32)
          + jnp.dot(h2, w1, preferred_element_type=F32)) + br_ref[...]
    tm = lg.shape[0]
    lane = lax.broadcasted_iota(I32, (tm, LANES), 1)
    lg = jnp.where(lane < N_EXPERTS, lg, -jnp.inf)
    vals, idxs = [], []
    for _ in range(TOP_K):
        mx = jnp.max(lg, axis=1, keepdims=True)
        ix = jnp.min(jnp.where(lg == mx, lane, LANES), axis=1, keepdims=True)
        vals.append(mx)
        idxs.append(ix)
        lg = jnp.where(lane == ix, -jnp.inf, lg)
    es = [jnp.exp(v - vals[0]) for v in vals]
    den = es[0] + es[1] + es[2] + es[3]
    topi = jnp.zeros((tm, LANES), I32)
    topw = jnp.zeros((tm, LANES), F32)
    onehot = jnp.zeros((tm, LANES), F32)
    for j in range(TOP_K):
        topi = jnp.where(lane == j, idxs[j], topi)
        topw = jnp.where(lane == j, es[j] / den, topw)
        onehot = onehot + (lane == idxs[j]).astype(F32)
    topi_ref[...] = topi
    topw_ref[...] = topw
    prefix = jnp.dot(ltri_ref[...], onehot.astype(BF16), preferred_element_type=F32) + carry_ref[0:1, :]
    pos = jnp.zeros((tm, LANES), I32)
    for j in range(TOP_K):
        pj = jnp.sum(jnp.where(lane == idxs[j], prefix, 0.0), axis=1, keepdims=True)
        pos = jnp.where(lane == j, pj.astype(I32), pos)
    pos_ref[...] = pos
    carry_ref[...] = carry_ref[...] + jnp.sum(onehot, axis=0, keepdims=True)
    cnt_ref[...] = carry_ref[...]


def _finish(xp, xs, nsa_p, nsa_s, hg_p, hg_s, gg, wpa, wpb, wo, g1, b1, wr, br):
    npt = xp.shape[0] // TM
    n = xp.shape[0] + xs.shape[0]
    row = lambda i: (i, 0)
    const = lambda i: (0, 0)
    ltri = jnp.asarray(np.tril(np.ones((TM, TM), np.float32), -1), dtype=BF16)
    outs = pl.pallas_call(
        functools.partial(_finish_kernel, n_prompt_tiles=npt),
        grid=(n // TM,),
        in_specs=_row_specs(npt) + _row_specs(npt, 512) + _row_specs(npt, 512) + [
                  pl.BlockSpec((TM, 2 * D_MODEL), row),
                  pl.BlockSpec((512, D_MODEL), const), pl.BlockSpec((512, D_MODEL), const),
                  pl.BlockSpec((D_MODEL, D_MODEL), const),
                  pl.BlockSpec((1, D_MODEL), const), pl.BlockSpec((1, D_MODEL), const),
                  pl.BlockSpec((D_MODEL, LANES), const), pl.BlockSpec((1, LANES), const),
                  pl.BlockSpec((TM, TM), const)],
        out_specs=[pl.BlockSpec((TM, D_MODEL), row), pl.BlockSpec((TM, LANES), row), pl.BlockSpec((TM, LANES), row),
                   pl.BlockSpec((TM, LANES), row), pl.BlockSpec((8, LANES), const)],
        out_shape=[jax.ShapeDtypeStruct((n, D_MODEL), F32), jax.ShapeDtypeStruct((n, LANES), I32),
                   jax.ShapeDtypeStruct((n, LANES), F32), jax.ShapeDtypeStruct((n, LANES), I32),
                   jax.ShapeDtypeStruct((8, LANES), F32)],
        scratch_shapes=[pltpu.VMEM((8, LANES), F32)],
        compiler_params=_cparams(("arbitrary",)),
        name="finish",
    )(xp, xs, nsa_p, nsa_s, hg_p, hg_s, gg, wpa, wpb, wo, g1, b1, wr, br, ltri)
    return outs


def _dispatch_kernel(padrow_ref, nused_ref, dest_ref, h_ref, xs_ref, zero_ref, sem):
    @pl.when(pl.program_id(0) == 0)
    def _():
        zero_ref[...] = jnp.zeros_like(zero_ref)
        fills = [pltpu.make_async_copy(zero_ref, xs_ref.at[pl.ds(pl.multiple_of(padrow_ref[e], 8), TE + 8), :], sem)
                 for e in range(N_EXPERTS)]
        for c in fills:
            c.start()
        for c in fills:
            c.wait()
        ntiles = (xs_ref.shape[0] - TE - 8) // TE

        def fill_tile(k, c):
            cp = pltpu.make_async_copy(zero_ref, xs_ref.at[pl.ds(pl.multiple_of(k * TE, TE), TE + 8), :], sem)
            cp.start()
            cp.wait()
            return c

        lax.fori_loop(nused_ref[0], ntiles + 1, fill_tile, 0)

    def copy(r, j):
        d = dest_ref[0, 0, r * TOP_K + j]
        return pltpu.make_async_copy(h_ref.at[pl.ds(r, 1), :], xs_ref.at[pl.ds(d, 1), :], sem)

    def start(r, c):
        for j in range(TOP_K):
            copy(r, j).start(priority=j % 2)
        return c

    def wait(r, c):
        for j in range(TOP_K):
            copy(r, j).wait()
        return c

    lax.fori_loop(0, TD, start, 0)
    lax.fori_loop(0, TD, wait, 0)


def _dispatch(padrow, nused, dest3, h, nrows):
    n = h.shape[0]
    return pl.pallas_call(
        _dispatch_kernel,
        grid_spec=pltpu.PrefetchScalarGridSpec(
            num_scalar_prefetch=2,
            grid=(n // TD,),
            in_specs=[pl.BlockSpec((1, 1, TD * TOP_K), lambda i, pr, nu: (i, 0, 0), memory_space=pltpu.SMEM),
                      pl.BlockSpec((TD, D_MODEL), lambda i, pr, nu: (i, 0))],
            out_specs=pl.BlockSpec(memory_space=pl.ANY),
            scratch_shapes=[pltpu.VMEM((TE + 8, D_MODEL), F32), pltpu.SemaphoreType.DMA(())]),
        out_shape=jax.ShapeDtypeStruct((nrows + TE + 8, D_MODEL), F32),
        compiler_params=_cparams(("arbitrary",)),
        name="dispatch",
    )(padrow, nused, dest3, h)


def _experts_kernel(bexp_ref, nused_ref, x_ref, wgu_ref, bgu_ref, wdn_ref, bdn_ref, y_ref):
    i = pl.program_id(0)

    @pl.when(i < nused_ref[0])
    def _():
        xb = x_ref[...].astype(BF16)
        acc = jnp.zeros((TE, D_MODEL), F32)
        cw = 256
        for c in range(D_FF // cw):
            gt = jnp.dot(xb, wgu_ref[0, :, c * cw:(c + 1) * cw], preferred_element_type=F32) + bgu_ref[0, :, c * cw:(c + 1) * cw]
            up = (jnp.dot(xb, wgu_ref[0, :, D_FF + c * cw:D_FF + (c + 1) * cw], preferred_element_type=F32)
                  + bgu_ref[0, :, D_FF + c * cw:D_FF + (c + 1) * cw])
            gt = jnp.minimum(gt, SWIGLU_LIMIT)
            up = jnp.clip(up, -SWIGLU_LIMIT, SWIGLU_LIMIT)
            act = (up + 1.0) * (gt * _sigmoid(SWIGLU_ALPHA * gt))
            acc = acc + jnp.dot(act.astype(BF16), wdn_ref[0, c * cw:(c + 1) * cw, :], preferred_element_type=F32)
        y_ref[...] = acc + bdn_ref[0]

    @pl.when(i >= nused_ref[0])
    def _():
        y_ref[...] = jnp.zeros_like(y_ref)


def _experts(bexp, nused, xs, wgu, bgu, wdn, bdn):
    nt = bexp.shape[0]
    return pl.pallas_call(
        _experts_kernel,
        grid_spec=pltpu.PrefetchScalarGridSpec(
            num_scalar_prefetch=2,
            grid=(nt,),
            in_specs=[pl.BlockSpec((TE, D_MODEL), lambda i, be, nu: (jnp.minimum(i, nu[0] - 1), 0)),
                      pl.BlockSpec((1, D_MODEL, 2 * D_FF), lambda i, be, nu: (be[i], 0, 0)),
                      pl.BlockSpec((1, 1, 2 * D_FF), lambda i, be, nu: (be[i], 0, 0)),
                      pl.BlockSpec((1, D_FF, D_MODEL), lambda i, be, nu: (be[i], 0, 0)),
                      pl.BlockSpec((1, 1, D_MODEL), lambda i, be, nu: (be[i], 0, 0))],
            out_specs=pl.BlockSpec((TE, D_MODEL), lambda i, be, nu: (i, 0))),
        out_shape=jax.ShapeDtypeStruct((nt * TE, D_MODEL), F32),
        compiler_params=_cparams(("arbitrary",)),
        name="experts",
    )(bexp, nused, xs, wgu, bgu, wdn, bdn)


def _combine_kernel(dest_ref, dnext_ref, h_ref, topw_ref, g2_ref, b2_ref, ys_ref, out_ref, buf_ref, sem):
    i = pl.program_id(0)
    n = pl.num_programs(0)

    def copy(d_ref, slot, r, j):
        d = d_ref[0, 0, r * TOP_K + j]
        return pltpu.make_async_copy(ys_ref.at[pl.ds(d, 1), :], buf_ref.at[slot, j, pl.ds(r, 1), :], sem.at[slot])

    def start_all(d_ref, slot):
        def body(r, c):
            for j in range(TOP_K):
                copy(d_ref, slot, r, j).start(priority=j % 2)
            return c
        lax.fori_loop(0, TD, body, 0)

    @pl.when(i == 0)
    def _():
        start_all(dest_ref, 0)

    @pl.when(i + 1 < n)
    def _():
        start_all(dnext_ref, (i + 1) % 2)

    slot = i % 2

    def wait(r, c):
        for j in range(TOP_K):
            copy(dest_ref, slot, r, j).wait()
        return c

    lax.fori_loop(0, TD, wait, 0)
    tw = topw_ref[...]
    moe = tw[:, 0:1] * buf_ref[slot, 0]
    for j in range(1, TOP_K):
        moe = moe + tw[:, j:j + 1] * buf_ref[slot, j]
    out_ref[...] = _layernorm(DN_ALPHA * h_ref[...] + moe, g2_ref[...], b2_ref[...])


def _combine(dest3, h, topw, g2, b2, ys):
    n = h.shape[0]
    last = n // TD - 1
    return pl.pallas_call(
        _combine_kernel,
        grid=(n // TD,),
        in_specs=[pl.BlockSpec((1, 1, TD * TOP_K), lambda i: (i, 0, 0), memory_space=pltpu.SMEM),
                  pl.BlockSpec((1, 1, TD * TOP_K), lambda i: (jnp.minimum(i + 1, last), 0, 0), memory_space=pltpu.SMEM),
                  pl.BlockSpec((TD, D_MODEL), lambda i: (i, 0)),
                  pl.BlockSpec((TD, LANES), lambda i: (i, 0)),
                  pl.BlockSpec((1, D_MODEL), lambda i: (0, 0)), pl.BlockSpec((1, D_MODEL), lambda i: (0, 0)),
                  pl.BlockSpec(memory_space=pl.ANY)],
        out_specs=pl.BlockSpec((TD, D_MODEL), lambda i: (i, 0)),
        out_shape=jax.ShapeDtypeStruct((n, D_MODEL), F32),
        scratch_shapes=[pltpu.VMEM((2, TOP_K, TD, D_MODEL), F32), pltpu.SemaphoreType.DMA((2,))],
        compiler_params=_cparams(("arbitrary",)),
        name="combine",
    )(dest3, dest3, h, topw, g2, b2, ys)


def kernel(x_prompt, x_sample, cache_cmp_kv, cache_slc_kv, cache_win_kv, state_hgrn, page_table, w_in, pe_ck, w_ck1,
           w_ck2, pe_cv, w_cv1, w_cv2, hg_lb_logits, hg_norm_w, w_pa, w_pb, w_o, ln1_g, ln1_b, w_router, b_router,
           w_gu, b_gu, w_dn, b_dn, ln2_g, ln2_b):
    nb, t, d = x_prompt.shape
    ndb, tnew, _ = x_sample.shape
    depth, npool, page = cache_cmp_kv.shape[:3]
    past = page_table.shape[1] * page
    assert depth == 1 and d == D_MODEL and t % TM == 0 and (ndb * SROWS) % TM == 0 and tnew <= SROWS
    n_p = nb * t
    n_s = ndb * SROWS
    n = n_p + n_s

    xp = x_prompt.reshape(n_p, d)
    xs_pad = jnp.pad(x_sample, ((0, 0), (0, SROWS - tnew), (0, 0))).reshape(n_s, d)

    w_all = _repack_w_in(w_in[0])
    tables = _rope_tables(t, past)
    qn, qr, kv, kvp, ng, hh, gg = _project(xp, xs_pad, w_all, tables, t)

    cw = (_compress_weights(pe_ck[0], w_ck1[0], w_ck2[0]), _compress_weights(pe_cv[0], w_cv1[0], w_cv2[0]))
    cc_p = _compress_prompt(kv, nb, t, cw)

    def rows_minor(c):
        return jnp.transpose(c, (0, 2, 3, 4, 1)).reshape(c.shape[0], 2, NSA_G * DH, c.shape[1])

    cc_s = _compress_sample(rows_minor(cache_cmp_kv[0]), page_table, cw)
    vt = jnp.stack([kvp[:n_p, (2 * i + 1) * LANES:(2 * i + 2) * LANES].reshape(nb, t, LANES).transpose(0, 2, 1)
                    for i in range(4)], axis=0)
    nsa_p = _nsa_prompt(qn, qr, cc_p, kvp, vt, ng, nb, t)
    nsa_s = _nsa_sample(qn, qr, cc_s, kv, rows_minor(cache_win_kv[0]), ng, rows_minor(cache_slc_kv[0]),
                        page_table, past, tnew, n_p)

    lb_all = jnp.cumsum(jax.nn.softmax(hg_lb_logits.astype(F32), axis=0), axis=0)
    lb = lb_all[0].reshape(HG_HEADS, 1, HG_D)
    nw = hg_norm_w[0].reshape(1, HG_D)
    hg_p, st_p = _hgrn(hh, lb, nw, jnp.zeros((nb, HG_HEADS, HG_D, HG_D), F32), nb, t, 0, HG_CHUNK, HG_CHUNK, 1)
    hg_s, st_s = _hgrn(hh, lb, nw, state_hgrn[0], ndb, SROWS, n_p, SROWS, tnew, HG_HEADS)

    wr = jnp.pad(w_router[0], ((0, 0), (0, LANES - N_EXPERTS)))
    br = jnp.pad(b_router[0], (0, LANES - N_EXPERTS)).reshape(1, LANES)
    h, topi, topw, pos, cnt = _finish(xp, xs_pad, nsa_p, nsa_s, hg_p, hg_s, gg, w_pa[0].astype(BF16),
                                      w_pb[0].astype(BF16), w_o[0].astype(BF16), ln1_g[0].reshape(1, d),
                                      ln1_b[0].reshape(1, d), wr, br)

    counts = cnt[0, :N_EXPERTS].astype(I32)
    pcounts = (counts + TE - 1) // TE * TE
    pend = jnp.cumsum(pcounts)
    pstart = pend - pcounts
    nt = (n * TOP_K) // TE + N_EXPERTS
    bexp = jnp.minimum(jnp.sum(jnp.arange(nt, dtype=I32)[:, None] * TE >= pend[None, :], axis=1), N_EXPERTS - 1).astype(I32)
    nused = (pend[-1:] // TE).astype(I32)
    dest = pstart[topi[:, :TOP_K]] + pos[:, :TOP_K]
    dest3 = dest.reshape(n // TD, 1, TD * TOP_K)

    xs = _dispatch(((pstart + counts) // 8 * 8).astype(I32), nused, dest3, h, nt * TE)
    ys = _experts(bexp, nused, xs, w_gu[0].astype(BF16), b_gu[0].reshape(N_EXPERTS, 1, 2 * D_FF),
                  w_dn[0].astype(BF16), b_dn[0].reshape(N_EXPERTS, 1, D_MODEL))
    y = _combine(dest3, h, topw, ln2_g[0].reshape(1, d), ln2_b[0].reshape(1, d), ys)

    y_p = y[:n_p].reshape(nb, t, d)
    y_s = y[n_p:].reshape(ndb, SROWS, d)[:, :tnew]
    kv_p = kv[:n_p].reshape(nb, t, 3, 2, NSA_G, DH)
    kv_s = kv[n_p:].reshape(ndb, SROWS, 3, 2, NSA_G, DH)[:, :tnew]
    keep = min(WINDOW, t)
    win_s = jnp.concatenate([cache_win_kv[0], kv_s[:, :, 2]], axis=1)[:, tnew:]
    return (y_p, y_s, kv_p[:, :, 0][None], kv_s[:, :, 0][None], kv_p[:, :, 1][None], kv_s[:, :, 1][None],
            kv_p[:, t - keep:, 2][None], win_s[None], st_p[None], st_s[None])
```

```python
import functools
import math

import numpy as np
import jax
import jax.numpy as jnp
from jax import lax
from jax.experimental import pallas as pl
from jax.experimental.pallas import tpu as pltpu

F32 = jnp.float32
BF16 = jnp.bfloat16
I32 = jnp.int32

D_MODEL = 1024
NSA_HEADS = 8
NSA_G = 2
NSA_HG = 4
DH = 64
CMP_STRIDE = 16
CMP_HIDDEN = 256
SLC_BLOCK = 64
SLC_TOPK = 16
WINDOW = 512
SEL_FORCED = 1.0e4
ROPE_THETA = 10000.0
HG_HEADS = 4
HG_D = 128
N_EXPERTS = 32
TOP_K = 4
D_FF = 1024
SWIGLU_LIMIT = 7.0
SWIGLU_ALPHA = 1.702
LN_EPS = 1e-5
RMS_EPS = 1e-6
DN_ALPHA = 2.0 ** 0.25

LANES = 128
TM = 256
TQ = 128
SROWS = 8
NSA_UNROLL = 4
HG_CHUNK = 16
TE = 512
TD = 128
NEG = -1.0e30
VMEM_LIMIT = 56 * 1024 * 1024

_C_Q, _C_KV, _C_PK, _C_NG, _C_HH, _C_GG, _C_END = 0, 1024, 1792, 2816, 3072, 5120, 7168


def _sigmoid(x):
    return 1.0 / (1.0 + jnp.exp(-x))


def _cparams(sem, vmem=VMEM_LIMIT):
    return pltpu.CompilerParams(dimension_semantics=sem, vmem_limit_bytes=vmem)


def _repack_w_in(w_in):
    d = w_in.shape[0]
    z64 = jnp.zeros((d, DH), F32)
    q = w_in[:, :512].reshape(d, NSA_HEADS, DH) * (DH ** -0.5)
    qpad = jnp.concatenate([q, jnp.zeros_like(q)], axis=-1).reshape(d, NSA_HEADS * LANES)
    kv = w_in[:, 512:1280]
    kv6 = kv.reshape(d, 6, NSA_G, DH)
    packs = []
    for kidx, vidx in ((2, 3), (4, 5)):
        for g in range(NSA_G):
            packs += [kv6[:, kidx, g], z64, kv6[:, vidx, g], z64]
    ng = w_in[:, 1280:1304].reshape(d, 3, NSA_G, NSA_HG)
    ngp = []
    for g in range(NSA_G):
        ngp += [ng[:, :, g, :].reshape(d, 3 * NSA_HG), jnp.zeros((d, LANES - 3 * NSA_HG), F32)]
    w = jnp.concatenate([qpad, kv] + packs + ngp + [w_in[:, 1304:3352], w_in[:, 3352:5400]], axis=-1)
    assert w.shape[1] == _C_END
    return w.astype(BF16)


def _rope_tables(t_prompt, past):
    half = DH // 2
    inv = ROPE_THETA ** (-jnp.arange(half, dtype=F32) / half)
    pos = jnp.concatenate([jnp.arange(t_prompt, dtype=I32), past + (jnp.arange(TM, dtype=I32) % SROWS)])
    ang = pos.astype(F32)[:, None] * inv[None, :]
    cos, sin = jnp.cos(ang), jnp.sin(ang)
    cos64 = jnp.concatenate([cos, cos], axis=-1)
    sin64 = jnp.concatenate([-sin, sin], axis=-1)
    one, zero = jnp.ones_like(cos64), jnp.zeros_like(cos64)
    cos_a = jnp.concatenate([cos64, cos64], axis=-1)
    sin_a = jnp.concatenate([sin64, sin64], axis=-1)
    cos_b = jnp.concatenate([cos64, one], axis=-1)
    sin_b = jnp.concatenate([sin64, zero], axis=-1)
    return cos_a, sin_a, cos_b, sin_b


def _pick_rows(xp_ref, xs_ref, n_prompt_tiles):
    return jnp.where(pl.program_id(0) < n_prompt_tiles, xp_ref[...], xs_ref[...])


def _row_specs(n_prompt_tiles, width=D_MODEL):
    return [pl.BlockSpec((TM, width), lambda i: (jnp.minimum(i, n_prompt_tiles - 1), 0)),
            pl.BlockSpec((TM, width), lambda i: (jnp.maximum(i - n_prompt_tiles, 0), 0))]


def _proj_kernel(xp_ref, xs_ref, w_ref, ca_ref, sa_ref, cb_ref, sb_ref,
                 qn_ref, qr_ref, kv_ref, kvp_ref, ng_ref, hh_ref, gg_ref, *, n_prompt_tiles):
    xb = _pick_rows(xp_ref, xs_ref, n_prompt_tiles).astype(BF16)
    tm = xb.shape[0]
    lane = lax.broadcasted_iota(I32, (tm, LANES), 1)
    first = (lane % DH) < (DH // 2)
    ca, sa, cb, sb = ca_ref[...], sa_ref[...], cb_ref[...], sb_ref[...]
    ones_hi = (lane >= DH).astype(F32)

    def rope(a, c, s):
        partner = jnp.where(first, pltpu.roll(a, LANES - DH // 2, 1), pltpu.roll(a, DH // 2, 1))
        return a * c + partner * s

    def mm(c0, n):
        return jnp.dot(xb, w_ref[:, c0:c0 + n], preferred_element_type=F32)

    for c in range(2):
        acc = mm(_C_Q + c * 512, 512)
        for j in range(4):
            a = acc[:, j * LANES:(j + 1) * LANES]
            col = c * 512 + j * LANES
            qn_ref[:, col:col + LANES] = a.astype(BF16)
            qr_ref[:, col:col + LANES] = rope(a, cb, sb).astype(BF16)
    acc = mm(_C_KV, 768)
    for j in range(6):
        a = acc[:, j * LANES:(j + 1) * LANES]
        if j in (2, 4):
            a = rope(a, ca, sa)
        kv_ref[:, j * LANES:(j + 1) * LANES] = a
    for c in range(2):
        acc = mm(_C_PK + c * 512, 512)
        for g in range(NSA_G):
            k = rope(acc[:, g * 256:g * 256 + LANES], cb, sb)
            col = c * 512 + g * 256
            kvp_ref[:, col:col + LANES] = k.astype(BF16)
            v1 = acc[:, g * 256 + LANES:g * 256 + 2 * LANES] + ones_hi
            kvp_ref[:, col + LANES:col + 2 * LANES] = v1.astype(BF16)
    ng_ref[...] = mm(_C_NG, 256)
    for c in range(4):
        hh_ref[:, c * 512:(c + 1) * 512] = mm(_C_HH + c * 512, 512)
    for c in range(4):
        gg_ref[:, c * 512:(c + 1) * 512] = _sigmoid(mm(_C_GG + c * 512, 512))


def _project(xp, xs, w_all, tables, t_prompt):
    n_prompt_tiles = xp.shape[0] // TM
    n = xp.shape[0] + xs.shape[0]
    tp = t_prompt // TM

    def tab_map(i):
        return (jnp.where(i < n_prompt_tiles, i % tp, tp), 0)

    row = lambda i: (i, 0)
    tab_spec = pl.BlockSpec((TM, LANES), tab_map)
    widths = (1024, 1024, 768, 1024, 256, 2048, 2048)
    dtypes = (BF16, BF16, F32, BF16, F32, F32, F32)
    return pl.pallas_call(
        functools.partial(_proj_kernel, n_prompt_tiles=n_prompt_tiles),
        grid=(n // TM,),
        in_specs=_row_specs(n_prompt_tiles) + [pl.BlockSpec((D_MODEL, _C_END), lambda i: (0, 0))] + [tab_spec] * 4,
        out_specs=[pl.BlockSpec((TM, w), row) for w in widths],
        out_shape=[jax.ShapeDtypeStruct((n, w), dt) for w, dt in zip(widths, dtypes)],
        compiler_params=_cparams(("arbitrary",)),
        name="proj",
    )(xp, xs, w_all, *tables)


def _gelu_tanh(x):
    return x * (0.5 * (1.0 + jnp.tanh(math.sqrt(2.0 / math.pi) * (x + 0.044715 * (x * x * x)))))


def _compress_core(xk_ref, xv_ref, z_ref, w1k_ref, w1v_ref, pek_ref, pev_ref, w2k_ref, w2v_ref, out_ref):
    nch = xk_ref.shape[0] // CMP_STRIDE
    lane = lax.broadcasted_iota(I32, (nch, LANES), 1)
    lo = lane < DH
    for p, xr in enumerate((xk_ref, xv_ref)):
        for r2 in range(CMP_STRIDE // 2):
            a = xr[pl.ds(2 * r2, nch, stride=CMP_STRIDE), :]
            b = xr[pl.ds(2 * r2 + 1, nch, stride=CMP_STRIDE), :]
            z0 = jnp.where(lo, a, pltpu.roll(b, DH, 1))
            z1 = jnp.where(lo, pltpu.roll(a, DH, 1), b)
            z_ref[2 * p, :, r2 * LANES:(r2 + 1) * LANES] = z0.astype(BF16)
            z_ref[2 * p + 1, :, r2 * LANES:(r2 + 1) * LANES] = z1.astype(BF16)
    _compress_mlp(z_ref, w1k_ref, w1v_ref, pek_ref, pev_ref, w2k_ref, w2v_ref, out_ref)


def _compress_mlp(z_ref, w1k_ref, w1v_ref, pek_ref, pev_ref, w2k_ref, w2v_ref, out_ref):
    nch = z_ref.shape[1]
    for s in range(4):
        w1, pe, w2 = (w1k_ref, pek_ref, w2k_ref) if s < 2 else (w1v_ref, pev_ref, w2v_ref)
        pq = jnp.dot(z_ref[s], w1[...], preferred_element_type=F32)
        pb = jnp.dot(pe[...], w1[...], preferred_element_type=F32)
        bias = pb[0:1, :CMP_HIDDEN] + pb[8:9, CMP_HIDDEN:]
        h = pq[:, :CMP_HIDDEN] + pltpu.roll(pq[:, CMP_HIDDEN:], nch - 1, 0) + bias
        out_ref[s] = jnp.dot(_gelu_tanh(h).astype(BF16), w2[...], preferred_element_type=F32)


def _compress_prompt_kernel(xk_ref, xv_ref, w1k_ref, w1v_ref, pek_ref, pev_ref, w2k_ref, w2v_ref, out_ref, z_ref):
    _compress_core(xk_ref, xv_ref, z_ref, w1k_ref, w1v_ref, pek_ref, pev_ref, w2k_ref, w2v_ref, out_ref.at[0])


def _compress_sample_kernel(pt_ref, cache_ref, perm_ref, w1k_ref, w1v_ref, pek_ref, pev_ref, w2k_ref, w2v_ref,
                            out_ref, buf_ref, z_ref, sem):
    b = pl.program_id(0)
    nb = pl.num_programs(0)
    npages = pt_ref.shape[1]
    page = buf_ref.shape[-1]

    def copies(bb, slot):
        return [pltpu.make_async_copy(cache_ref.at[pt_ref[bb, p]], buf_ref.at[slot, p], sem.at[slot])
                for p in range(npages)]

    @pl.when(b == 0)
    def _():
        for c in copies(0, 0):
            c.start()

    @pl.when(b + 1 < nb)
    def _():
        for c in copies(b + 1, (b + 1) % 2):
            c.start()

    slot = b % 2
    for c in copies(b, slot):
        c.wait()

    cpp = page // CMP_STRIDE
    zero = jnp.zeros((DH, 2 * page), BF16)

    def build_z(pp, carry):
        c0 = pl.multiple_of(pp * 2 * cpp, 2 * cpp)
        for kv in range(2):
            xt = jnp.concatenate([buf_ref[slot, 2 * pp, kv], buf_ref[slot, 2 * pp + 1, kv]], axis=1).astype(BF16)
            for g in range(NSA_G):
                xg = xt[g * DH:(g + 1) * DH]
                y = jnp.concatenate([jnp.concatenate([xg, zero], axis=1), jnp.concatenate([zero, xg], axis=1)], axis=0)
                px = lax.dot_general(perm_ref[...], y, (((1,), (1,)), ((), ())), preferred_element_type=F32)
                for r2 in range(CMP_STRIDE // 2):
                    z_ref[2 * kv + g, pl.ds(c0, 2 * cpp), r2 * LANES:(r2 + 1) * LANES] = (
                        px[r2 * 2 * cpp:(r2 + 1) * 2 * cpp].astype(BF16))
        return carry

    lax.fori_loop(0, npages // 2, build_z, 0, unroll=4)
    _compress_mlp(z_ref, w1k_ref, w1v_ref, pek_ref, pev_ref, w2k_ref, w2v_ref, out_ref.at[0])


def _compress_weights(pe, w1, w2):
    w1cat = jnp.concatenate([w1[:CMP_STRIDE * DH], w1[CMP_STRIDE * DH:]], axis=1).astype(BF16)
    pef = pe.reshape(2, 1, CMP_STRIDE * DH)
    pe16 = jnp.broadcast_to(pef, (2, 8, CMP_STRIDE * DH)).reshape(16, CMP_STRIDE * DH).astype(BF16)
    w2cat = jnp.concatenate([w2, w2], axis=1).astype(BF16)
    return w1cat, pe16, w2cat


def _wspecs(n_grid_args):
    const = (lambda *a: (0, 0))
    return [pl.BlockSpec((CMP_STRIDE * DH, 2 * CMP_HIDDEN), const), pl.BlockSpec((CMP_STRIDE * DH, 2 * CMP_HIDDEN), const),
            pl.BlockSpec((16, CMP_STRIDE * DH), const), pl.BlockSpec((16, CMP_STRIDE * DH), const),
            pl.BlockSpec((CMP_HIDDEN, LANES), const), pl.BlockSpec((CMP_HIDDEN, LANES), const)]


def _compress_prompt(kv, nb, t, cw):
    nch = t // CMP_STRIDE
    (w1k, pek, w2k), (w1v, pev, w2v) = cw
    return pl.pallas_call(
        _compress_prompt_kernel,
        grid=(nb,),
        in_specs=[pl.BlockSpec((t, LANES), lambda b: (b, 0)), pl.BlockSpec((t, LANES), lambda b: (b, 1))] + _wspecs(1),
        out_specs=pl.BlockSpec((1, 4, nch, LANES), lambda b: (b, 0, 0, 0)),
        out_shape=jax.ShapeDtypeStruct((nb, 4, nch, LANES), F32),
        scratch_shapes=[pltpu.VMEM((4, nch, CMP_STRIDE * DH), BF16)],
        compiler_params=_cparams(("arbitrary",)),
        name="compress_prompt",
    )(kv, kv, w1k, w1v, pek, pev, w2k, w2v)


def _compress_sample(cache_t, page_table, cw):
    nb, npages = page_table.shape
    page = cache_t.shape[-1]
    assert page == LANES and cache_t.shape[1:3] == (2, NSA_G * DH)
    r = npages * page
    nch = r // CMP_STRIDE
    (w1k, pek, w2k), (w1v, pev, w2v) = cw
    wspecs = [pl.BlockSpec(s.block_shape, lambda b, pt: (0, 0)) for s in _wspecs(2)]
    cpp = page // CMP_STRIDE
    perm = np.zeros((page, 4 * page), np.float32)
    for r2 in range(CMP_STRIDE // 2):
        for pg in range(2):
            for j in range(cpp):
                m = r2 * 2 * cpp + pg * cpp + j
                perm[m, pg * page + CMP_STRIDE * j + 2 * r2] = 1.0
                perm[m, 2 * page + pg * page + CMP_STRIDE * j + 2 * r2 + 1] = 1.0
    assert npages % 4 == 0 and page == CMP_STRIDE * cpp
    return pl.pallas_call(
        _compress_sample_kernel,
        grid_spec=pltpu.PrefetchScalarGridSpec(
            num_scalar_prefetch=1,
            grid=(nb,),
            in_specs=[pl.BlockSpec(memory_space=pl.ANY), pl.BlockSpec((page, 4 * page), lambda b, pt: (0, 0))] + wspecs,
            out_specs=pl.BlockSpec((1, 4, nch, LANES), lambda b, pt: (b, 0, 0, 0)),
            scratch_shapes=[pltpu.VMEM((2, npages, 2, NSA_G * DH, page), F32),
                            pltpu.VMEM((4, nch, CMP_STRIDE * DH), BF16), pltpu.SemaphoreType.DMA((2,))]),
        out_shape=jax.ShapeDtypeStruct((nb, 4, nch, LANES), F32),
        compiler_params=_cparams(("arbitrary",)),
        name="compress_sample",
    )(page_table, cache_t, jnp.asarray(perm, dtype=BF16), w1k, w1v, pek, pev, w2k, w2v)


def _softmax_update(s, m, l, acc, v, v_is_t=False):
    m_new = jnp.maximum(m, jnp.max(s, axis=1, keepdims=True))
    alpha = jnp.exp(m - m_new)
    p = jnp.exp(s - m_new)
    l = alpha * l + jnp.sum(p, axis=1, keepdims=True)
    dims = (((1,), (1,)), ((), ())) if v_is_t else (((1,), (0,)), ((), ()))
    acc = alpha * acc + lax.dot_general(p.astype(BF16), v, dims, preferred_element_type=F32)
    return m_new, l, acc


def _topk_mask(score, sidx, in_sel, lane0, nsel, k):
    rank = jnp.zeros(score.shape, I32)
    for sp in range(nsel):
        c = score[:, lane0 + sp:lane0 + sp + 1]
        beats = (c > score) | ((c == score) & (sp < sidx))
        rank = rank + beats.astype(I32)
    return in_sel & (rank < k)


def _nsa_prompt_kernel(qn_ref, qr_ref, ck_ref, cv_ref, ks_ref, vs_ref, kw_ref, vw_ref, ng_ref, cover_ref,
                       out_ref, sel_ref, s_ref, *, nsb, ksel):
    qt = pl.program_id(2)
    t0 = qt * TQ
    nck = ck_ref.shape[2]
    krow = lax.broadcasted_iota(I32, (TQ, TQ), 0)
    qcol = lax.broadcasted_iota(I32, (TQ, TQ), 1)
    tq = t0 + qcol
    nt = (((1,), (1,)), ((), ()))

    def col_reduce(x, op):
        return op(op(x.reshape(x.shape[0] // 8, 8, x.shape[1]), axis=0), axis=0, keepdims=True)

    ckb = ck_ref[0, 0].astype(BF16)
    cvt = cv_ref[0, 0].T.astype(BF16)
    cidx = lax.broadcasted_iota(I32, (nck, TQ), 0)
    cmask = (cidx * CMP_STRIDE + 2 * CMP_STRIDE - 1 <= t0 + lax.broadcasted_iota(I32, (nck, TQ), 1)) & (cidx < nck - 1)
    psum = jnp.zeros((nck, TQ), F32)
    o_cmp = []
    for h in range(NSA_HG):
        s = lax.dot_general(ckb, qn_ref[:, h * LANES:(h + 1) * LANES], nt, preferred_element_type=F32)
        s = jnp.where(cmask, s, -jnp.inf)
        m = col_reduce(s, jnp.max)
        m = jnp.where(m > -jnp.inf, m, 0.0)
        e = jnp.exp(s - m)
        d = col_reduce(e, jnp.sum)
        p = e * (1.0 / jnp.where(d > 0, d, 1.0))
        psum = psum + p
        o_cmp.append(jnp.dot(cvt, p.astype(BF16), preferred_element_type=F32)[0:DH])

    cover = cover_ref[...]
    p_hi = psum.astype(BF16)
    p_lo = (psum - p_hi.astype(F32)).astype(BF16)
    imp = jnp.dot(cover, p_hi, preferred_element_type=F32) + jnp.dot(cover, p_lo, preferred_element_type=F32)
    nsr = imp.shape[0]
    sidx = lax.broadcasted_iota(I32, (nsr, TQ), 0)
    tq_s = t0 + lax.broadcasted_iota(I32, (nsr, TQ), 1)
    cur = tq_s // SLC_BLOCK
    valid = sidx * SLC_BLOCK <= tq_s
    forced = (sidx == 0) | (sidx == cur) | (sidx == cur - 1)
    score = jnp.where(forced, SEL_FORCED, jnp.where(valid, imp, -jnp.inf))
    rank = jnp.zeros((nsr, TQ), I32)
    for sp in range(nsb):
        c = score[sp:sp + 1, :]
        rank = rank + ((c > score) | ((c == score) & (sp < sidx))).astype(I32)
    sel_ref[...] = jnp.where((rank < ksel) & (sidx < nsb), 0.0, NEG)

    ntiles = ks_ref.shape[0] // TQ
    qs = [qr_ref[:, h * LANES:(h + 1) * LANES] for h in range(NSA_HG)]
    vrows = DH + 8

    def attend(k_ref, v_ref, j_first, n_tiles, bias_fn):
        unroll = n_tiles if isinstance(n_tiles, int) else NSA_UNROLL
        n_it = (n_tiles + unroll - 1) // unroll

        def tile_of(i, u):
            idx = i * unroll + u
            j = j_first + idx
            jc = jnp.clip(j, 0, ntiles - 1)
            return idx, j, jc, pl.multiple_of(jc * TQ, TQ)

        def scores(i, mx):
            mx = list(mx)
            for u in range(unroll):
                idx, j, jc, r0 = tile_of(i, u)
                kt = k_ref[pl.ds(r0, TQ), :]
                bias = jnp.where(idx < n_tiles, bias_fn(j, jc), NEG)
                for h in range(NSA_HG):
                    s = lax.dot_general(kt, qs[h], nt, preferred_element_type=F32) + bias
                    s_ref[h, idx] = s
                    mx[h] = jnp.maximum(mx[h], jnp.max(s.reshape(TQ // 8, 8, TQ), axis=0))
            return tuple(mx)

        mx = lax.fori_loop(0, n_it, scores, tuple(jnp.full((8, TQ), NEG, F32) for _ in range(NSA_HG)))
        ms = [jnp.max(m8, axis=0, keepdims=True) for m8 in mx]

        def values(i, accs):
            accs = list(accs)
            for u in range(unroll):
                idx, j, jc, r0 = tile_of(i, u)
                vt = v_ref[0:vrows, pl.ds(r0, TQ)]
                for h in range(NSA_HG):
                    p = jnp.exp(s_ref[h, idx] - ms[h])
                    accs[h] = accs[h] + jnp.dot(vt, p.astype(BF16), preferred_element_type=F32)
            return tuple(accs)

        accs = lax.fori_loop(0, n_it, values, tuple(jnp.zeros((vrows, TQ), F32) for _ in range(NSA_HG)))
        return [acc[0:DH] / acc[DH:DH + 1] for acc in accs]

    diag_bias = jnp.where(krow <= qcol, 0.0, NEG)
    dq = qcol - krow

    def slc_bias(j, jc):
        sel = jnp.where(krow < SLC_BLOCK, sel_ref[pl.ds(2 * jc, 1), :], sel_ref[pl.ds(2 * jc + 1, 1), :])
        return sel + jnp.where(j == qt, diag_bias, 0.0)

    def win_bias(j, jc):
        off = t0 - jc * TQ
        return jnp.where((dq >= -off) & (dq < WINDOW - off) & (j >= 0), 0.0, NEG)

    nwin = WINDOW // TQ
    gates = _sigmoid(ng_ref[...]).T
    o_slc = attend(ks_ref, vs_ref, 0, qt + 1, slc_bias)
    o_win = attend(kw_ref, vw_ref, qt - nwin, nwin + 1, win_bias)
    tot = []
    for h in range(NSA_HG):
        tot.append(gates[h:h + 1] * o_cmp[h] + gates[NSA_HG + h:NSA_HG + h + 1] * o_slc[h]
                   + gates[2 * NSA_HG + h:2 * NSA_HG + h + 1] * o_win[h])
    out_ref[...] = jnp.concatenate(tot, axis=0).T.astype(BF16)


def _cover_matrix(nc, nsb, lane0, rows, cols):
    c = np.arange(rows)[:, None]
    s = np.arange(cols)[None, :] - lane0
    cov = ((c < nc) & (s >= 0) & (s < nsb) & (c * CMP_STRIDE < (s + 1) * SLC_BLOCK)
           & (c * CMP_STRIDE + 2 * CMP_STRIDE > s * SLC_BLOCK))
    return jnp.asarray(cov.astype(np.float32), dtype=BF16)


def _nsa_prompt(qn, qr, cc, kvp, vt, ng, nb, t):
    nqt = t // TQ
    nch = t // CMP_STRIDE
    nsb = t // SLC_BLOCK
    nsr = 32
    assert nsb <= nsr and SLC_BLOCK * 2 == TQ
    cover = _cover_matrix(nch - 1, nsb, 0, nch, nsr).T
    kern = functools.partial(_nsa_prompt_kernel, nsb=nsb, ksel=min(SLC_TOPK, nsb))
    return pl.pallas_call(
        kern,
        grid=(nb, NSA_G, nqt),
        in_specs=[pl.BlockSpec((TQ, NSA_HG * LANES), lambda b, g, q: (b * nqt + q, g)),
                  pl.BlockSpec((TQ, NSA_HG * LANES), lambda b, g, q: (b * nqt + q, g)),
                  pl.BlockSpec((1, 1, nch, LANES), lambda b, g, q: (b, g, 0, 0)),
                  pl.BlockSpec((1, 1, nch, LANES), lambda b, g, q: (b, 2 + g, 0, 0)),
                  pl.BlockSpec((t, LANES), lambda b, g, q: (b, 2 * g)),
                  pl.BlockSpec((None, None, LANES, t), lambda b, g, q: (g, b, 0, 0)),
                  pl.BlockSpec((t, LANES), lambda b, g, q: (b, 4 + 2 * g)),
                  pl.BlockSpec((None, None, LANES, t), lambda b, g, q: (2 + g, b, 0, 0)),
                  pl.BlockSpec((TQ, LANES), lambda b, g, q: (b * nqt + q, g)),
                  pl.BlockSpec((nsr, nch), lambda b, g, q: (0, 0))],
        out_specs=pl.BlockSpec((TQ, 2 * LANES), lambda b, g, q: (b * nqt + q, g)),
        out_shape=jax.ShapeDtypeStruct((nb * t, NSA_G * 2 * LANES), BF16),
        scratch_shapes=[pltpu.VMEM((nsr, TQ), F32),
                        pltpu.VMEM((NSA_HG, NSA_UNROLL * (-(-max(nqt, WINDOW // TQ + 1) // NSA_UNROLL)), TQ, TQ), F32)],
        compiler_params=_cparams(("arbitrary", "arbitrary", "arbitrary")),
        name="nsa_prompt",
    )(qn, qr, cc, cc, kvp, vt, kvp, vt, ng, cover)


def _nsa_sample_kernel(pt_ref, qn_ref, qr_ref, cc_ref, new_ref, win_ref, ng_ref, cover_ref, expand_ref, cache_ref,
                       out_ref, buf_ref, s_ref, sem, *, past, tnew, nsb, ksel):
    b = pl.program_id(0)
    nb = pl.num_programs(0)
    npages = pt_ref.shape[1]
    page = past // npages
    nck = cc_ref.shape[2]
    rows = NSA_HEADS * SROWS
    half = rows // 2

    def copies(bb, slot):
        return [pltpu.make_async_copy(cache_ref.at[pt_ref[bb, p]], buf_ref.at[slot, p], sem.at[slot])
                for p in range(npages)]

    @pl.when(b == 0)
    def _():
        for c in copies(0, 0):
            c.start()

    @pl.when(b + 1 < nb)
    def _():
        for c in copies(b + 1, (b + 1) % 2):
            c.start()

    lane = lax.broadcasted_iota(I32, (rows, LANES), 1)
    row = lax.broadcasted_iota(I32, (rows, LANES), 0)
    tt = row % SROWS
    lo = lane < DH

    qn = jnp.concatenate([qn_ref[:, h * LANES:(h + 1) * LANES] for h in range(NSA_HEADS)], axis=0)
    qr32 = jnp.concatenate([qr_ref[:, h * LANES:(h + 1) * LANES] for h in range(NSA_HEADS)], axis=0).astype(F32)
    qr = jnp.where(row < half, qr32, pltpu.roll(qr32, DH, 1)).astype(BF16)

    nlt = nck // LANES
    o_cmp = []
    psums = []
    for g in range(NSA_G):
        ckb = cc_ref[0, g].astype(BF16)
        cvb = cc_ref[0, 2 + g].astype(BF16)
        qg = qn[g * half:(g + 1) * half]
        s = lax.dot_general(qg, ckb, (((1,), (1,)), ((), ())), preferred_element_type=F32)
        cl = lax.broadcasted_iota(I32, s.shape, 1)
        s = jnp.where(cl < nck - 1, s, -jnp.inf)
        m = jnp.max(s, axis=1, keepdims=True)
        e = jnp.exp(s - m)
        p = e / jnp.sum(e, axis=1, keepdims=True)
        o_cmp.append(jnp.dot(p.astype(BF16), cvb, preferred_element_type=F32))
        ps = p[0:SROWS]
        for h in range(1, NSA_HG):
            ps = ps + p[h * SROWS:(h + 1) * SROWS]
        psums.append(ps)
    psum = jnp.concatenate(psums, axis=0)
    cover = cover_ref[...]
    p_hi = psum.astype(BF16)
    p_lo = (psum - p_hi.astype(F32)).astype(BF16)
    imp = jnp.dot(p_hi, cover, preferred_element_type=F32) + jnp.dot(p_lo, cover, preferred_element_type=F32)
    nsl = imp.shape[1]
    sidx = lax.broadcasted_iota(I32, imp.shape, 1)
    tpos16 = past + lax.broadcasted_iota(I32, imp.shape, 0) % SROWS
    cur = tpos16 // SLC_BLOCK
    valid = sidx * SLC_BLOCK <= tpos16
    forced = (sidx == 0) | (sidx == cur) | (sidx == cur - 1)
    score = jnp.where(forced, SEL_FORCED, jnp.where(valid, imp, -jnp.inf))
    in_sel = sidx < nsb
    selected = _topk_mask(score, sidx, in_sel, 0, nsb, ksel)
    selb16 = jnp.where(selected, 0.0, NEG)
    selrows = jnp.concatenate([selb16[g * SROWS:(g + 1) * SROWS] for g in range(NSA_G) for _ in range(NSA_HG)],
                              axis=0).astype(BF16)

    slot = b % 2
    for c in copies(b, slot):
        c.wait()

    ppt = 4
    kt_rows = ppt * page
    nkt = npages // ppt
    unroll = 2 if nkt % 2 == 0 else 1
    nt = (((1,), (1,)), ((), ()))

    def lane_fold(x, op):
        out = x[:, 0:LANES]
        for i in range(1, kt_rows // LANES):
            out = op(out, x[:, i * LANES:(i + 1) * LANES])
        return out

    newk = new_ref[:, 2 * LANES:3 * LANES].astype(BF16)
    newv = new_ref[:, 3 * LANES:4 * LANES].astype(BF16)
    kk = lax.broadcasted_iota(I32, (rows, SROWS), 1)
    tt8 = lax.broadcasted_iota(I32, (rows, SROWS), 0) % SROWS
    new_ok = (kk <= tt8) & (kk < tnew)
    nb_new = past // SLC_BLOCK
    bias_new = selrows[:, nb_new:nb_new + 1].astype(F32)
    s_new = jnp.where(new_ok, lax.dot_general(qr, newk, nt, preferred_element_type=F32) + bias_new, NEG)

    def scores(i, mx):
        for u in range(unroll):
            j = i * unroll + u
            kt = jnp.concatenate([buf_ref[slot, j * ppt + a, 0] for a in range(ppt)], axis=1).astype(BF16)
            c0 = pl.multiple_of(j * kt_rows, kt_rows)
            s = (jnp.dot(qr, kt, preferred_element_type=F32)
                 + jnp.dot(selrows, expand_ref[:, pl.ds(c0, kt_rows)], preferred_element_type=F32))
            s_ref[j] = s
            mx = jnp.maximum(mx, lane_fold(s, jnp.maximum))
        return mx

    mx = lax.fori_loop(0, nkt // unroll, scores, jnp.full((rows, LANES), NEG, F32))
    m = jnp.maximum(jnp.max(mx, axis=1, keepdims=True), jnp.max(s_new, axis=1, keepdims=True))

    def values(i, st):
        acc, ls = st
        for u in range(unroll):
            j = i * unroll + u
            vt = jnp.concatenate([buf_ref[slot, j * ppt + a, 1] for a in range(ppt)], axis=1).astype(BF16)
            p = jnp.exp(s_ref[j] - m)
            acc = acc + lax.dot_general(p.astype(BF16), vt, nt, preferred_element_type=F32)
            ls = ls + lane_fold(p, jnp.add)
        return acc, ls

    acc, ls = lax.fori_loop(0, nkt // unroll, values,
                            (jnp.zeros((rows, LANES), F32), jnp.zeros((rows, LANES), F32)))
    p_new = jnp.exp(s_new - m)
    acc = acc + jnp.dot(p_new.astype(BF16), newv, preferred_element_type=F32)
    o_slc = acc / (jnp.sum(ls, axis=1, keepdims=True) + jnp.sum(p_new, axis=1, keepdims=True))
    zero_state = (jnp.full((rows, 1), NEG, F32), jnp.zeros((rows, 1), F32), jnp.zeros((rows, LANES), F32))

    wb = win_ref.shape[3]
    wk = win_ref[0, 0].astype(BF16)
    wv = win_ref[0, 1].astype(BF16)
    wi = lax.broadcasted_iota(I32, (rows, wb), 1)
    wt = lax.broadcasted_iota(I32, (rows, wb), 0) % SROWS
    w_ok = (wt + wb - wi < WINDOW) if wb >= WINDOW else (wi >= 0)
    s = jnp.dot(qr, wk, preferred_element_type=F32)
    s = jnp.where(w_ok, s, NEG)
    st = _softmax_update(s, *zero_state, wv, v_is_t=True)
    nwk = new_ref[:, 4 * LANES:5 * LANES].astype(BF16)
    nwv = new_ref[:, 5 * LANES:6 * LANES].astype(BF16)
    s = lax.dot_general(qr, nwk, (((1,), (1,)), ((), ())), preferred_element_type=F32)
    s = jnp.where(new_ok, s, NEG)
    m, l, acc = _softmax_update(s, *st, nwv)
    o_win = acc / l

    gates = _sigmoid(ng_ref[...])
    outs = []
    for hh in range(NSA_HEADS):
        g, h = divmod(hh, NSA_HG)
        gl = gates[:, g * LANES:(g + 1) * LANES]
        g0 = gl[:, h:h + 1]
        g1 = gl[:, NSA_HG + h:NSA_HG + h + 1]
        g2 = gl[:, 2 * NSA_HG + h:2 * NSA_HG + h + 1]
        r0 = hh * SROWS
        oc = o_cmp[g][h * SROWS:(h + 1) * SROWS]
        outs.append(g0 * oc + g1 * o_slc[r0:r0 + SROWS] + g2 * o_win[r0:r0 + SROWS])
    lo8 = lax.broadcasted_iota(I32, (SROWS, LANES), 1) < DH
    out_ref[:, 0 * LANES:1 * LANES] = jnp.where(lo8, outs[0], pltpu.roll(outs[1], DH, 1)).astype(BF16)
    out_ref[:, 1 * LANES:2 * LANES] = jnp.where(lo8, outs[2], pltpu.roll(outs[3], DH, 1)).astype(BF16)
    out_ref[:, 2 * LANES:3 * LANES] = jnp.where(lo8, pltpu.roll(outs[4], DH, 1), outs[5]).astype(BF16)
    out_ref[:, 3 * LANES:4 * LANES] = jnp.where(lo8, pltpu.roll(outs[6], DH, 1), outs[7]).astype(BF16)


def _nsa_sample(qn, qr, cc, kv, win_t, ng, slc_t, page_table, past, tnew, row0):
    nb, npages = page_table.shape
    page = slc_t.shape[-1]
    assert page * npages == past and npages % 4 == 0 and page % SLC_BLOCK == 0
    nck = past // CMP_STRIDE
    nsb = -(-(past + tnew) // SLC_BLOCK)
    nsl = -(-nsb // LANES) * LANES
    assert (nck - 2) * CMP_STRIDE + 2 * CMP_STRIDE - 1 <= past and (past + tnew) // CMP_STRIDE == nck
    assert tnew <= SROWS and past // SLC_BLOCK == nsb - 1
    cover = _cover_matrix(nck - 1, nsb, 0, nck, nsl)
    expand = jnp.asarray((np.arange(nsl)[:, None] == np.arange(past)[None, :] // SLC_BLOCK).astype(np.float32), dtype=BF16)
    wb = win_t.shape[3]
    kern = functools.partial(_nsa_sample_kernel, past=past, tnew=tnew, nsb=nsb, ksel=min(SLC_TOPK, nsb))
    rb = row0 // SROWS
    return pl.pallas_call(
        kern,
        grid_spec=pltpu.PrefetchScalarGridSpec(
            num_scalar_prefetch=1,
            grid=(nb,),
            in_specs=[pl.BlockSpec((SROWS, NSA_HEADS * LANES), lambda b, pt: (rb + b, 0)),
                      pl.BlockSpec((SROWS, NSA_HEADS * LANES), lambda b, pt: (rb + b, 0)),
                      pl.BlockSpec((1, 4, nck, LANES), lambda b, pt: (b, 0, 0, 0)),
                      pl.BlockSpec((SROWS, 6 * LANES), lambda b, pt: (rb + b, 0)),
                      pl.BlockSpec((1, 2, NSA_G * DH, wb), lambda b, pt: (b, 0, 0, 0)),
                      pl.BlockSpec((SROWS, 2 * LANES), lambda b, pt: (rb + b, 0)),
                      pl.BlockSpec((nck, nsl), lambda b, pt: (0, 0)),
                      pl.BlockSpec((nsl, past), lambda b, pt: (0, 0)),
                      pl.BlockSpec(memory_space=pl.ANY)],
            out_specs=pl.BlockSpec((SROWS, NSA_HEADS * DH), lambda b, pt: (b, 0)),
            scratch_shapes=[pltpu.VMEM((2, npages, 2, NSA_G * DH, page), F32),
                            pltpu.VMEM((npages // 4, NSA_HEADS * SROWS, 4 * page), F32),
                            pltpu.SemaphoreType.DMA((2,))]),
        out_shape=jax.ShapeDtypeStruct((nb * SROWS, NSA_HEADS * DH), BF16),
        compiler_params=_cparams(("arbitrary",)),
        name="nsa_sample",
    )(page_table, qn, qr, cc, kv, win_t, ng, cover, expand, slc_t)


def _hgrn_kernel(hq_ref, hf_ref, hi_ref, hg_ref, lb_ref, nw_ref, s0_ref, o_ref, sout_ref, *, chunk, valid, nchunks):
    heads = lb_ref.shape[0]
    for hd in range(heads):
        cols = slice(hd * HG_D, (hd + 1) * HG_D)
        _hgrn_head(hq_ref.at[:, cols], hf_ref.at[:, cols], hi_ref.at[:, cols], hg_ref.at[:, cols], lb_ref.at[hd],
                   nw_ref, s0_ref.at[0, hd], o_ref.at[:, cols], sout_ref.at[0, hd],
                   chunk=chunk, valid=valid, nchunks=nchunks)


def _hgrn_head(hq_ref, hf_ref, hi_ref, hg_ref, lb_ref, nw_ref, s0_ref, o_ref, sout_ref, *, chunk, valid, nchunks):
    lb = lb_ref[...]
    nw = nw_ref[...]
    row = lax.broadcasted_iota(I32, (chunk, HG_D), 0)
    rvalid = row < valid

    def body(i, st):
        r0 = pl.multiple_of(i * chunk, chunk)
        hq = hq_ref[pl.ds(r0, chunk), :]
        hf = hf_ref[pl.ds(r0, chunk), :]
        iv = hi_ref[pl.ds(r0, chunk), :]
        hg = hg_ref[pl.ds(r0, chunk), :]
        q = hq * _sigmoid(hq)
        f = lb + (1.0 - lb) * _sigmoid(hf)
        logf = jnp.where(rvalid, jnp.log(f), 0.0)
        k = jnp.where(rvalid, 1.0 - f, 0.0)
        b = logf
        sh = 1
        while sh < chunk:
            b = b + jnp.where(row >= sh, pltpu.roll(b, sh, 0), 0.0)
            sh *= 2
        o_in = jnp.zeros((chunk, HG_D), F32)
        for s in range(valid):
            e = jnp.exp(jnp.where(row >= s, b - b[s:s + 1, :], -jnp.inf))
            a = jnp.sum(q * e * k[s:s + 1, :], axis=1, keepdims=True)
            o_in = o_in + a * iv[s:s + 1, :]
        qd = (q * jnp.exp(b)).astype(BF16)
        o = o_in + lax.dot_general(qd, st.astype(BF16), (((1,), (1,)), ((), ())), preferred_element_type=F32)
        bl = b[chunk - 1:chunk, :]
        kd = (k * jnp.exp(bl - b)).astype(BF16)
        u = lax.dot_general(iv.astype(BF16), kd, (((0,), (0,)), ((), ())), preferred_element_type=F32)
        st = st * jnp.exp(bl) + u
        ms = jnp.mean(o * o, axis=1, keepdims=True)
        y = o * lax.rsqrt(ms + RMS_EPS) * nw * (hg * _sigmoid(hg))
        o_ref[pl.ds(r0, chunk), :] = y.astype(BF16)
        return st

    st = lax.fori_loop(0, nchunks, body, s0_ref[...].T, unroll=min(4, nchunks))
    sout_ref[...] = st.T


def _hgrn(hh, lb, nw, s0, nseq, rows_per_seq, row0, chunk, valid, heads_per_step):
    rb = row0 // rows_per_seq
    hps = heads_per_step
    ngrp = HG_HEADS // hps
    kern = functools.partial(_hgrn_kernel, chunk=chunk, valid=valid, nchunks=rows_per_seq // chunk)
    blk = lambda k: pl.BlockSpec((rows_per_seq, hps * HG_D), lambda b, h, k=k: (rb + b, k * ngrp + h))
    return pl.pallas_call(
        kern,
        grid=(nseq, ngrp),
        in_specs=[blk(0), blk(1), blk(2), blk(3),
                  pl.BlockSpec((hps, 1, HG_D), lambda b, h: (h, 0, 0)),
                  pl.BlockSpec((1, HG_D), lambda b, h: (0, 0)),
                  pl.BlockSpec((1, hps, HG_D, HG_D), lambda b, h: (b, h, 0, 0))],
        out_specs=[pl.BlockSpec((rows_per_seq, hps * HG_D), lambda b, h: (b, h)),
                   pl.BlockSpec((1, hps, HG_D, HG_D), lambda b, h: (b, h, 0, 0))],
        out_shape=[jax.ShapeDtypeStruct((nseq * rows_per_seq, HG_HEADS * HG_D), BF16),
                   jax.ShapeDtypeStruct((nseq, HG_HEADS, HG_D, HG_D), F32)],
        compiler_params=_cparams(("arbitrary", "arbitrary")),
        name="hgrn",
    )(hh, hh, hh, hh, lb, nw, s0)


def _layernorm(y, g, b):
    mu = jnp.mean(y, axis=1, keepdims=True)
    yc = y - mu
    var = jnp.mean(yc * yc, axis=1, keepdims=True)
    return yc * lax.rsqrt(var + LN_EPS) * g + b


def _finish_kernel(xp_ref, xs_ref, ap_ref, as_ref, hp_ref, hs_ref, gg_ref, wpa_ref, wpb_ref, wo_ref, g1_ref, b1_ref,
                   wr_ref, br_ref, ltri_ref, h_ref, topi_ref, topw_ref, pos_ref, cnt_ref, carry_ref, *, n_prompt_tiles):
    i = pl.program_id(0)

    @pl.when(i == 0)
    def _():
        carry_ref[...] = jnp.zeros_like(carry_ref)

    a = jnp.dot(_pick_rows(ap_ref, as_ref, n_prompt_tiles), wpa_ref[...], preferred_element_type=F32)
    bb = jnp.dot(_pick_rows(hp_ref, hs_ref, n_prompt_tiles), wpb_ref[...], preferred_element_type=F32)
    m = gg_ref[:, :D_MODEL] * a + gg_ref[:, D_MODEL:] * bb
    mix = jnp.dot(m.astype(BF16), wo_ref[...], preferred_element_type=F32)
    h = _layernorm(DN_ALPHA * _pick_rows(xp_ref, xs_ref, n_prompt_tiles) + mix, g1_ref[...], b1_ref[...])
    h_ref[...] = h

    h1 = h.astype(BF16)
    h2 = (h - h1.astype(F32)).astype(BF16)
    w = wr_ref[...]
    w1 = w.astype(BF16)
    w2 = (w - w1.astype(F32)).astype(BF16)
    lg = (jnp.dot(h1, w1, preferred_element_type=F32) + jnp.dot(h1, w2, preferred_element_type=F32)
          + jnp.dot(h2, w1, preferred_element_type=F32)) + br_ref[...]
    tm = lg.shape[0]
    lane = lax.broadcasted_iota(I32, (tm, LANES), 1)
    lg = jnp.where(lane < N_EXPERTS, lg, -jnp.inf)
    vals, idxs = [], []
    for _ in range(TOP_K):
        mx = jnp.max(lg, axis=1, keepdims=True)
        ix = jnp.min(jnp.where(lg == mx, lane, LANES), axis=1, keepdims=True)
        vals.append(mx)
        idxs.append(ix)
        lg = jnp.where(lane == ix, -jnp.inf, lg)
    es = [jnp.exp(v - vals[0]) for v in vals]
    den = es[0] + es[1] + es[2] + es[3]
    topi = jnp.zeros((tm, LANES), I32)
    topw = jnp.zeros((tm, LANES), F32)
    onehot = jnp.zeros((tm, LANES), F32)
    for j in range(TOP_K):
        topi = jnp.where(lane == j, idxs[j], topi)
        topw = jnp.where(lane == j, es[j] / den, topw)
        onehot = onehot + (lane == idxs[j]).astype(F32)
    topi_ref[...] = topi
    topw_ref[...] = topw
    prefix = jnp.dot(ltri_ref[...], onehot.astype(BF16), preferred_element_type=F32) + carry_ref[0:1, :]
    pos = jnp.zeros((tm, LANES), I32)
    for j in range(TOP_K):
        pj = jnp.sum(jnp.where(lane == idxs[j], prefix, 0.0), axis=1, keepdims=True)
        pos = jnp.where(lane == j, pj.astype(I32), pos)
    pos_ref[...] = pos
    carry_ref[...] = carry_ref[...] + jnp.sum(onehot, axis=0, keepdims=True)
    cnt_ref[...] = carry_ref[...]


def _finish(xp, xs, nsa_p, nsa_s, hg_p, hg_s, gg, wpa, wpb, wo, g1, b1, wr, br):
    npt = xp.shape[0] // TM
    n = xp.shape[0] + xs.shape[0]
    row = lambda i: (i, 0)
    const = lambda i: (0, 0)
    ltri = jnp.asarray(np.tril(np.ones((TM, TM), np.float32), -1), dtype=BF16)
    outs = pl.pallas_call(
        functools.partial(_finish_kernel, n_prompt_tiles=npt),
        grid=(n // TM,),
        in_specs=_row_specs(npt) + _row_specs(npt, 512) + _row_specs(npt, 512) + [
                  pl.BlockSpec((TM, 2 * D_MODEL), row),
                  pl.BlockSpec((512, D_MODEL), const), pl.BlockSpec((512, D_MODEL), const),
                  pl.BlockSpec((D_MODEL, D_MODEL), const),
                  pl.BlockSpec((1, D_MODEL), const), pl.BlockSpec((1, D_MODEL), const),
                  pl.BlockSpec((D_MODEL, LANES), const), pl.BlockSpec((1, LANES), const),
                  pl.BlockSpec((TM, TM), const)],
        out_specs=[pl.BlockSpec((TM, D_MODEL), row), pl.BlockSpec((TM, LANES), row), pl.BlockSpec((TM, LANES), row),
                   pl.BlockSpec((TM, LANES), row), pl.BlockSpec((8, LANES), const)],
        out_shape=[jax.ShapeDtypeStruct((n, D_MODEL), F32), jax.ShapeDtypeStruct((n, LANES), I32),
                   jax.ShapeDtypeStruct((n, LANES), F32), jax.ShapeDtypeStruct((n, LANES), I32),
                   jax.ShapeDtypeStruct((8, LANES), F32)],
        scratch_shapes=[pltpu.VMEM((8, LANES), F32)],
        compiler_params=_cparams(("arbitrary",)),
        name="finish",
    )(xp, xs, nsa_p, nsa_s, hg_p, hg_s, gg, wpa, wpb, wo, g1, b1, wr, br, ltri)
    return outs


def _dispatch_kernel(padrow_ref, nused_ref, dest_ref, h_ref, xs_ref, zero_ref, sem):
    @pl.when(pl.program_id(0) == 0)
    def _():
        zero_ref[...] = jnp.zeros_like(zero_ref)
        fills = [pltpu.make_async_copy(zero_ref, xs_ref.at[pl.ds(pl.multiple_of(padrow_ref[e], 8), TE + 8), :], sem)
                 for e in range(N_EXPERTS)]
        for c in fills:
            c.start()
        for c in fills:
            c.wait()
        ntiles = (xs_ref.shape[0] - TE - 8) // TE

        def fill_tile(k, c):
            cp = pltpu.make_async_copy(zero_ref, xs_ref.at[pl.ds(pl.multiple_of(k * TE, TE), TE + 8), :], sem)
            cp.start()
            cp.wait()
            return c

        lax.fori_loop(nused_ref[0], ntiles + 1, fill_tile, 0)

    def copy(r, j):
        d = dest_ref[0, 0, r * TOP_K + j]
        return pltpu.make_async_copy(h_ref.at[pl.ds(r, 1), :], xs_ref.at[pl.ds(d, 1), :], sem)

    def start(r, c):
        for j in range(TOP_K):
            copy(r, j).start(priority=j % 2)
        return c

    def wait(r, c):
        for j in range(TOP_K):
            copy(r, j).wait()
        return c

    lax.fori_loop(0, TD, start, 0)
    lax.fori_loop(0, TD, wait, 0)


def _dispatch(padrow, nused, dest3, h, nrows):
    n = h.shape[0]
    return pl.pallas_call(
        _dispatch_kernel,
        grid_spec=pltpu.PrefetchScalarGridSpec(
            num_scalar_prefetch=2,
            grid=(n // TD,),
            in_specs=[pl.BlockSpec((1, 1, TD * TOP_K), lambda i, pr, nu: (i, 0, 0), memory_space=pltpu.SMEM),
                      pl.BlockSpec((TD, D_MODEL), lambda i, pr, nu: (i, 0))],
            out_specs=pl.BlockSpec(memory_space=pl.ANY),
            scratch_shapes=[pltpu.VMEM((TE + 8, D_MODEL), F32), pltpu.SemaphoreType.DMA(())]),
        out_shape=jax.ShapeDtypeStruct((nrows + TE + 8, D_MODEL), F32),
        compiler_params=_cparams(("arbitrary",)),
        name="dispatch",
    )(padrow, nused, dest3, h)


def _experts_kernel(bexp_ref, nused_ref, x_ref, wgu_ref, bgu_ref, wdn_ref, bdn_ref, y_ref, wgu_b, wdn_b):
    i = pl.program_id(0)
    cw = 256
    used = i < nused_ref[0]

    @pl.when(used & ((i == 0) | (bexp_ref[i] != bexp_ref[jnp.maximum(i - 1, 0)])))
    def _():
        for c in range(2 * D_FF // cw):
            wgu_b[:, c * cw:(c + 1) * cw] = wgu_ref[0, :, c * cw:(c + 1) * cw].astype(BF16)
        for c in range(D_MODEL // cw):
            wdn_b[:, c * cw:(c + 1) * cw] = wdn_ref[0, :, c * cw:(c + 1) * cw].astype(BF16)

    @pl.when(used)
    def _():
        xb = x_ref[...].astype(BF16)
        acc = jnp.zeros((TE, D_MODEL), F32)
        for c in range(D_FF // cw):
            gt = jnp.dot(xb, wgu_b[:, c * cw:(c + 1) * cw], preferred_element_type=F32) + bgu_ref[0, :, c * cw:(c + 1) * cw]
            up = (jnp.dot(xb, wgu_b[:, D_FF + c * cw:D_FF + (c + 1) * cw], preferred_element_type=F32)
                  + bgu_ref[0, :, D_FF + c * cw:D_FF + (c + 1) * cw])
            gt = jnp.minimum(gt, SWIGLU_LIMIT)
            up = jnp.clip(up, -SWIGLU_LIMIT, SWIGLU_LIMIT)
            act = (up + 1.0) * (gt * _sigmoid(SWIGLU_ALPHA * gt))
            acc = acc + jnp.dot(act.astype(BF16), wdn_b[c * cw:(c + 1) * cw, :], preferred_element_type=F32)
        y_ref[...] = acc + bdn_ref[0]

    @pl.when(i >= nused_ref[0])
    def _():
        y_ref[...] = jnp.zeros_like(y_ref)


def _experts(bexp, nused, xs, wgu, bgu, wdn, bdn):
    nt = bexp.shape[0]
    return pl.pallas_call(
        _experts_kernel,
        grid_spec=pltpu.PrefetchScalarGridSpec(
            num_scalar_prefetch=2,
            grid=(nt,),
            in_specs=[pl.BlockSpec((TE, D_MODEL), lambda i, be, nu: (jnp.minimum(i, nu[0] - 1), 0)),
                      pl.BlockSpec((1, D_MODEL, 2 * D_FF), lambda i, be, nu: (be[i], 0, 0)),
                      pl.BlockSpec((1, 1, 2 * D_FF), lambda i, be, nu: (be[i], 0, 0)),
                      pl.BlockSpec((1, D_FF, D_MODEL), lambda i, be, nu: (be[i], 0, 0)),
                      pl.BlockSpec((1, 1, D_MODEL), lambda i, be, nu: (be[i], 0, 0))],
            out_specs=pl.BlockSpec((TE, D_MODEL), lambda i, be, nu: (i, 0)),
            scratch_shapes=[pltpu.VMEM((D_MODEL, 2 * D_FF), BF16), pltpu.VMEM((D_FF, D_MODEL), BF16)]),
        out_shape=jax.ShapeDtypeStruct((nt * TE, D_MODEL), F32),
        compiler_params=_cparams(("arbitrary",)),
        name="experts",
    )(bexp, nused, xs, wgu, bgu, wdn, bdn)


def _combine_kernel(dest_ref, dnext_ref, h_ref, topw_ref, g2_ref, b2_ref, ys_ref, outp_ref, outs_ref, buf_ref, sem,
                    *, n_prompt_tiles):
    i = pl.program_id(0)
    n = pl.num_programs(0)

    def copy(d_ref, slot, r, j):
        d = d_ref[0, 0, r * TOP_K + j]
        return pltpu.make_async_copy(ys_ref.at[pl.ds(d, 1), :], buf_ref.at[slot, j, pl.ds(r, 1), :], sem.at[slot])

    def start_all(d_ref, slot):
        def body(r, c):
            for j in range(TOP_K):
                copy(d_ref, slot, r, j).start(priority=j % 2)
            return c
        lax.fori_loop(0, TD, body, 0)

    @pl.when(i == 0)
    def _():
        start_all(dest_ref, 0)

    @pl.when(i + 1 < n)
    def _():
        start_all(dnext_ref, (i + 1) % 2)

    slot = i % 2

    def wait(r, c):
        for j in range(TOP_K):
            copy(dest_ref, slot, r, j).wait()
        return c

    lax.fori_loop(0, TD, wait, 0)
    tw = topw_ref[...]
    moe = tw[:, 0:1] * buf_ref[slot, 0]
    for j in range(1, TOP_K):
        moe = moe + tw[:, j:j + 1] * buf_ref[slot, j]
    y = _layernorm(DN_ALPHA * h_ref[...] + moe, g2_ref[...], b2_ref[...])

    @pl.when(i < n_prompt_tiles)
    def _():
        outp_ref[...] = y

    @pl.when(i >= n_prompt_tiles)
    def _():
        outs_ref[...] = y


def _combine(dest3, h, topw, g2, b2, ys, n_p):
    n = h.shape[0]
    last = n // TD - 1
    npt = n_p // TD
    return pl.pallas_call(
        functools.partial(_combine_kernel, n_prompt_tiles=npt),
        grid=(n // TD,),
        in_specs=[pl.BlockSpec((1, 1, TD * TOP_K), lambda i: (i, 0, 0), memory_space=pltpu.SMEM),
                  pl.BlockSpec((1, 1, TD * TOP_K), lambda i: (jnp.minimum(i + 1, last), 0, 0), memory_space=pltpu.SMEM),
                  pl.BlockSpec((TD, D_MODEL), lambda i: (i, 0)),
                  pl.BlockSpec((TD, LANES), lambda i: (i, 0)),
                  pl.BlockSpec((1, D_MODEL), lambda i: (0, 0)), pl.BlockSpec((1, D_MODEL), lambda i: (0, 0)),
                  pl.BlockSpec(memory_space=pl.ANY)],
        out_specs=[pl.BlockSpec((TD, D_MODEL), lambda i: (jnp.minimum(i, npt - 1), 0)),
                   pl.BlockSpec((TD, D_MODEL), lambda i: (jnp.maximum(i - npt, 0), 0))],
        out_shape=[jax.ShapeDtypeStruct((n_p, D_MODEL), F32), jax.ShapeDtypeStruct((n - n_p, D_MODEL), F32)],
        scratch_shapes=[pltpu.VMEM((2, TOP_K, TD, D_MODEL), F32), pltpu.SemaphoreType.DMA((2,))],
        compiler_params=_cparams(("arbitrary",)),
        name="combine",
    )(dest3, dest3, h, topw, g2, b2, ys)


def kernel(x_prompt, x_sample, cache_cmp_kv, cache_slc_kv, cache_win_kv, state_hgrn, page_table, w_in, pe_ck, w_ck1,
           w_ck2, pe_cv, w_cv1, w_cv2, hg_lb_logits, hg_norm_w, w_pa, w_pb, w_o, ln1_g, ln1_b, w_router, b_router,
           w_gu, b_gu, w_dn, b_dn, ln2_g, ln2_b):
    nb, t, d = x_prompt.shape
    ndb, tnew, _ = x_sample.shape
    depth, npool, page = cache_cmp_kv.shape[:3]
    past = page_table.shape[1] * page
    assert depth == 1 and d == D_MODEL and t % TM == 0 and (ndb * SROWS) % TM == 0 and tnew <= SROWS
    n_p = nb * t
    n_s = ndb * SROWS
    n = n_p + n_s

    xp = x_prompt.reshape(n_p, d)
    xs_pad = jnp.pad(x_sample, ((0, 0), (0, SROWS - tnew), (0, 0))).reshape(n_s, d)

    w_all = _repack_w_in(w_in[0])
    tables = _rope_tables(t, past)
    qn, qr, kv, kvp, ng, hh, gg = _project(xp, xs_pad, w_all, tables, t)

    cw = (_compress_weights(pe_ck[0], w_ck1[0], w_ck2[0]), _compress_weights(pe_cv[0], w_cv1[0], w_cv2[0]))
    cc_p = _compress_prompt(kv, nb, t, cw)

    def rows_minor(c):
        return jnp.transpose(c, (0, 2, 3, 4, 1)).reshape(c.shape[0], 2, NSA_G * DH, c.shape[1])

    cc_s = _compress_sample(rows_minor(cache_cmp_kv[0]), page_table, cw)
    vt = jnp.stack([kvp[:n_p, (2 * i + 1) * LANES:(2 * i + 2) * LANES].reshape(nb, t, LANES).transpose(0, 2, 1)
                    for i in range(4)], axis=0)
    nsa_p = _nsa_prompt(qn, qr, cc_p, kvp, vt, ng, nb, t)
    nsa_s = _nsa_sample(qn, qr, cc_s, kv, rows_minor(cache_win_kv[0]), ng, rows_minor(cache_slc_kv[0]),
                        page_table, past, tnew, n_p)

    lb_all = jnp.cumsum(jax.nn.softmax(hg_lb_logits.astype(F32), axis=0), axis=0)
    lb = lb_all[0].reshape(HG_HEADS, 1, HG_D)
    nw = hg_norm_w[0].reshape(1, HG_D)
    hg_p, st_p = _hgrn(hh, lb, nw, jnp.zeros((nb, HG_HEADS, HG_D, HG_D), F32), nb, t, 0, HG_CHUNK, HG_CHUNK, 1)
    hg_s, st_s = _hgrn(hh, lb, nw, state_hgrn[0], ndb, SROWS, n_p, SROWS, tnew, HG_HEADS)

    wr = jnp.pad(w_router[0], ((0, 0), (0, LANES - N_EXPERTS)))
    br = jnp.pad(b_router[0], (0, LANES - N_EXPERTS)).reshape(1, LANES)
    h, topi, topw, pos, cnt = _finish(xp, xs_pad, nsa_p, nsa_s, hg_p, hg_s, gg, w_pa[0].astype(BF16),
                                      w_pb[0].astype(BF16), w_o[0].astype(BF16), ln1_g[0].reshape(1, d),
                                      ln1_b[0].reshape(1, d), wr, br)

    counts = cnt[0, :N_EXPERTS].astype(I32)
    pcounts = (counts + TE - 1) // TE * TE
    pend = jnp.cumsum(pcounts)
    pstart = pend - pcounts
    nt = (n * TOP_K) // TE + N_EXPERTS
    bexp = jnp.minimum(jnp.sum(jnp.arange(nt, dtype=I32)[:, None] * TE >= pend[None, :], axis=1), N_EXPERTS - 1).astype(I32)
    nused = (pend[-1:] // TE).astype(I32)
    dest = pstart[topi[:, :TOP_K]] + pos[:, :TOP_K]
    dest3 = dest.reshape(n // TD, 1, TD * TOP_K)

    xs = _dispatch(((pstart + counts) // 8 * 8).astype(I32), nused, dest3, h, nt * TE)
    ys = _experts(bexp, nused, xs, w_gu[0], b_gu[0].reshape(N_EXPERTS, 1, 2 * D_FF),
                  w_dn[0], b_dn[0].reshape(N_EXPERTS, 1, D_MODEL))
    y_p, y_s = _combine(dest3, h, topw, ln2_g[0].reshape(1, d), ln2_b[0].reshape(1, d), ys, n_p)

    y_p = y_p.reshape(nb, t, d)
    y_s = y_s.reshape(ndb, SROWS, d)[:, :tnew]
    kv_p = kv[:n_p].reshape(nb, t, 3, 2, NSA_G, DH)
    kv_s = kv[n_p:].reshape(ndb, SROWS, 3, 2, NSA_G, DH)[:, :tnew]
    keep = min(WINDOW, t)
    win_s = jnp.concatenate([cache_win_kv[0], kv_s[:, :, 2]], axis=1)[:, tnew:]
    return (y_p, y_s, kv_p[:, :, 0][None], kv_s[:, :, 0][None], kv_p[:, :, 1][None], kv_s[:, :, 1][None],
            kv_p[:, t - keep:, 2][None], win_s[None], st_p[None], st_s[None])
```

```python
import functools
import math

import numpy as np
import jax
import jax.numpy as jnp
from jax import lax
from jax.experimental import pallas as pl
from jax.experimental.pallas import tpu as pltpu

F32 = jnp.float32
BF16 = jnp.bfloat16
I32 = jnp.int32

D_MODEL = 1024
NSA_HEADS = 8
NSA_G = 2
NSA_HG = 4
DH = 64
CMP_STRIDE = 16
CMP_HIDDEN = 256
SLC_BLOCK = 64
SLC_TOPK = 16
WINDOW = 512
SEL_FORCED = 1.0e4
ROPE_THETA = 10000.0
HG_HEADS = 4
HG_D = 128
N_EXPERTS = 32
TOP_K = 4
D_FF = 1024
SWIGLU_LIMIT = 7.0
SWIGLU_ALPHA = 1.702
LN_EPS = 1e-5
RMS_EPS = 1e-6
DN_ALPHA = 2.0 ** 0.25

LANES = 128
TM = 256
TQ = 128
SROWS = 8
NSA_UNROLL = 4
HG_CHUNK = 16
TE = 512
TD = 128
NEG = -1.0e30
VMEM_LIMIT = 56 * 1024 * 1024

_C_Q, _C_KV, _C_PK, _C_NG, _C_HH, _C_GG, _C_END = 0, 1024, 1792, 2816, 3072, 5120, 7168


def _sigmoid(x):
    return 1.0 / (1.0 + jnp.exp(-x))


def _cparams(sem, vmem=VMEM_LIMIT):
    return pltpu.CompilerParams(dimension_semantics=sem, vmem_limit_bytes=vmem)


def _repack_w_in(w_in):
    d = w_in.shape[0]
    z64 = jnp.zeros((d, DH), F32)
    q = w_in[:, :512].reshape(d, NSA_HEADS, DH) * (DH ** -0.5)
    qpad = jnp.concatenate([q, jnp.zeros_like(q)], axis=-1).reshape(d, NSA_HEADS * LANES)
    kv = w_in[:, 512:1280]
    kv6 = kv.reshape(d, 6, NSA_G, DH)
    packs = []
    for kidx, vidx in ((2, 3), (4, 5)):
        for g in range(NSA_G):
            packs += [kv6[:, kidx, g], z64, kv6[:, vidx, g], z64]
    ng = w_in[:, 1280:1304].reshape(d, 3, NSA_G, NSA_HG)
    ngp = []
    for g in range(NSA_G):
        ngp += [ng[:, :, g, :].reshape(d, 3 * NSA_HG), jnp.zeros((d, LANES - 3 * NSA_HG), F32)]
    w = jnp.concatenate([qpad, kv] + packs + ngp + [w_in[:, 1304:3352], w_in[:, 3352:5400]], axis=-1)
    assert w.shape[1] == _C_END
    return w.astype(BF16)


def _rope_tables(t_prompt, past):
    half = DH // 2
    inv = ROPE_THETA ** (-jnp.arange(half, dtype=F32) / half)
    pos = jnp.concatenate([jnp.arange(t_prompt, dtype=I32), past + (jnp.arange(TM, dtype=I32) % SROWS)])
    ang = pos.astype(F32)[:, None] * inv[None, :]
    cos, sin = jnp.cos(ang), jnp.sin(ang)
    cos64 = jnp.concatenate([cos, cos], axis=-1)
    sin64 = jnp.concatenate([-sin, sin], axis=-1)
    one, zero = jnp.ones_like(cos64), jnp.zeros_like(cos64)
    cos_a = jnp.concatenate([cos64, cos64], axis=-1)
    sin_a = jnp.concatenate([sin64, sin64], axis=-1)
    cos_b = jnp.concatenate([cos64, one], axis=-1)
    sin_b = jnp.concatenate([sin64, zero], axis=-1)
    return cos_a, sin_a, cos_b, sin_b


def _pick_rows(xp_ref, xs_ref, n_prompt_tiles):
    return jnp.where(pl.program_id(0) < n_prompt_tiles, xp_ref[...], xs_ref[...])


def _row_specs(n_prompt_tiles, width=D_MODEL):
    return [pl.BlockSpec((TM, width), lambda i: (jnp.minimum(i, n_prompt_tiles - 1), 0)),
            pl.BlockSpec((TM, width), lambda i: (jnp.maximum(i - n_prompt_tiles, 0), 0))]


def _proj_kernel(xp_ref, xs_ref, w_ref, ca_ref, sa_ref, cb_ref, sb_ref,
                 qn_ref, qr_ref, kv_ref, kvp_ref, ng_ref, hh_ref, gg_ref, *, n_prompt_tiles):
    xb = _pick_rows(xp_ref, xs_ref, n_prompt_tiles).astype(BF16)
    tm = xb.shape[0]
    lane = lax.broadcasted_iota(I32, (tm, LANES), 1)
    first = (lane % DH) < (DH // 2)
    ca, sa, cb, sb = ca_ref[...], sa_ref[...], cb_ref[...], sb_ref[...]
    ones_hi = (lane >= DH).astype(F32)

    def rope(a, c, s):
        partner = jnp.where(first, pltpu.roll(a, LANES - DH // 2, 1), pltpu.roll(a, DH // 2, 1))
        return a * c + partner * s

    def mm(c0, n):
        return jnp.dot(xb, w_ref[:, c0:c0 + n], preferred_element_type=F32)

    for c in range(2):
        acc = mm(_C_Q + c * 512, 512)
        for j in range(4):
            a = acc[:, j * LANES:(j + 1) * LANES]
            col = c * 512 + j * LANES
            qn_ref[:, col:col + LANES] = a.astype(BF16)
            qr_ref[:, col:col + LANES] = rope(a, cb, sb).astype(BF16)
    acc = mm(_C_KV, 768)
    for j in range(6):
        a = acc[:, j * LANES:(j + 1) * LANES]
        if j in (2, 4):
            a = rope(a, ca, sa)
        kv_ref[:, j * LANES:(j + 1) * LANES] = a
    for c in range(2):
        acc = mm(_C_PK + c * 512, 512)
        for g in range(NSA_G):
            k = rope(acc[:, g * 256:g * 256 + LANES], cb, sb)
            col = c * 512 + g * 256
            kvp_ref[:, col:col + LANES] = k.astype(BF16)
            v1 = acc[:, g * 256 + LANES:g * 256 + 2 * LANES] + ones_hi
            kvp_ref[:, col + LANES:col + 2 * LANES] = v1.astype(BF16)
    ng_ref[...] = mm(_C_NG, 256)
    for c in range(4):
        hh_ref[:, c * 512:(c + 1) * 512] = mm(_C_HH + c * 512, 512)
    for c in range(4):
        gg_ref[:, c * 512:(c + 1) * 512] = _sigmoid(mm(_C_GG + c * 512, 512))


def _project(xp, xs, w_all, tables, t_prompt):
    n_prompt_tiles = xp.shape[0] // TM
    n = xp.shape[0] + xs.shape[0]
    tp = t_prompt // TM

    def tab_map(i):
        return (jnp.where(i < n_prompt_tiles, i % tp, tp), 0)

    row = lambda i: (i, 0)
    tab_spec = pl.BlockSpec((TM, LANES), tab_map)
    widths = (1024, 1024, 768, 1024, 256, 2048, 2048)
    dtypes = (BF16, BF16, F32, BF16, F32, F32, F32)
    return pl.pallas_call(
        functools.partial(_proj_kernel, n_prompt_tiles=n_prompt_tiles),
        grid=(n // TM,),
        in_specs=_row_specs(n_prompt_tiles) + [pl.BlockSpec((D_MODEL, _C_END), lambda i: (0, 0))] + [tab_spec] * 4,
        out_specs=[pl.BlockSpec((TM, w), row) for w in widths],
        out_shape=[jax.ShapeDtypeStruct((n, w), dt) for w, dt in zip(widths, dtypes)],
        compiler_params=_cparams(("arbitrary",)),
        name="proj",
    )(xp, xs, w_all, *tables)


def _gelu_tanh(x):
    return x * (0.5 * (1.0 + jnp.tanh(math.sqrt(2.0 / math.pi) * (x + 0.044715 * (x * x * x)))))


def _compress_core(xk_ref, xv_ref, z_ref, w1k_ref, w1v_ref, pek_ref, pev_ref, w2k_ref, w2v_ref, out_ref):
    nch = xk_ref.shape[0] // CMP_STRIDE
    lane = lax.broadcasted_iota(I32, (nch, LANES), 1)
    lo = lane < DH
    for p, xr in enumerate((xk_ref, xv_ref)):
        for r2 in range(CMP_STRIDE // 2):
            a = xr[pl.ds(2 * r2, nch, stride=CMP_STRIDE), :]
            b = xr[pl.ds(2 * r2 + 1, nch, stride=CMP_STRIDE), :]
            z0 = jnp.where(lo, a, pltpu.roll(b, DH, 1))
            z1 = jnp.where(lo, pltpu.roll(a, DH, 1), b)
            z_ref[2 * p, :, r2 * LANES:(r2 + 1) * LANES] = z0.astype(BF16)
            z_ref[2 * p + 1, :, r2 * LANES:(r2 + 1) * LANES] = z1.astype(BF16)
    _compress_mlp(z_ref, w1k_ref, w1v_ref, pek_ref, pev_ref, w2k_ref, w2v_ref, out_ref)


def _compress_mlp(z_ref, w1k_ref, w1v_ref, pek_ref, pev_ref, w2k_ref, w2v_ref, out_ref):
    nch = z_ref.shape[1]
    for s in range(4):
        w1, pe, w2 = (w1k_ref, pek_ref, w2k_ref) if s < 2 else (w1v_ref, pev_ref, w2v_ref)
        pq = jnp.dot(z_ref[s], w1[...], preferred_element_type=F32)
        pb = jnp.dot(pe[...], w1[...], preferred_element_type=F32)
        bias = pb[0:1, :CMP_HIDDEN] + pb[8:9, CMP_HIDDEN:]
        h = pq[:, :CMP_HIDDEN] + pltpu.roll(pq[:, CMP_HIDDEN:], nch - 1, 0) + bias
        out_ref[s] = jnp.dot(_gelu_tanh(h).astype(BF16), w2[...], preferred_element_type=F32)


def _compress_prompt_kernel(xk_ref, xv_ref, w1k_ref, w1v_ref, pek_ref, pev_ref, w2k_ref, w2v_ref, out_ref, z_ref):
    _compress_core(xk_ref, xv_ref, z_ref, w1k_ref, w1v_ref, pek_ref, pev_ref, w2k_ref, w2v_ref, out_ref.at[0])


def _compress_sample_kernel(pt_ref, cache_ref, perm_ref, w1k_ref, w1v_ref, pek_ref, pev_ref, w2k_ref, w2v_ref,
                            out_ref, buf_ref, z_ref, sem):
    b = pl.program_id(0)
    nb = pl.num_programs(0)
    npages = pt_ref.shape[1]
    page = buf_ref.shape[-1]

    def copies(bb, slot):
        return [pltpu.make_async_copy(cache_ref.at[pt_ref[bb, p]], buf_ref.at[slot, p], sem.at[slot])
                for p in range(npages)]

    @pl.when(b == 0)
    def _():
        for c in copies(0, 0):
            c.start()

    @pl.when(b + 1 < nb)
    def _():
        for c in copies(b + 1, (b + 1) % 2):
            c.start()

    slot = b % 2
    for c in copies(b, slot):
        c.wait()

    cpp = page // CMP_STRIDE
    zero = jnp.zeros((DH, 2 * page), BF16)

    def build_z(pp, carry):
        c0 = pl.multiple_of(pp * 2 * cpp, 2 * cpp)
        for kv in range(2):
            xt = jnp.concatenate([buf_ref[slot, 2 * pp, kv], buf_ref[slot, 2 * pp + 1, kv]], axis=1).astype(BF16)
            for g in range(NSA_G):
                xg = xt[g * DH:(g + 1) * DH]
                y = jnp.concatenate([jnp.concatenate([xg, zero], axis=1), jnp.concatenate([zero, xg], axis=1)], axis=0)
                px = lax.dot_general(perm_ref[...], y, (((1,), (1,)), ((), ())), preferred_element_type=F32)
                for r2 in range(CMP_STRIDE // 2):
                    z_ref[2 * kv + g, pl.ds(c0, 2 * cpp), r2 * LANES:(r2 + 1) * LANES] = (
                        px[r2 * 2 * cpp:(r2 + 1) * 2 * cpp].astype(BF16))
        return carry

    lax.fori_loop(0, npages // 2, build_z, 0, unroll=4)
    _compress_mlp(z_ref, w1k_ref, w1v_ref, pek_ref, pev_ref, w2k_ref, w2v_ref, out_ref.at[0])


def _compress_weights(pe, w1, w2):
    w1cat = jnp.concatenate([w1[:CMP_STRIDE * DH], w1[CMP_STRIDE * DH:]], axis=1).astype(BF16)
    pef = pe.reshape(2, 1, CMP_STRIDE * DH)
    pe16 = jnp.broadcast_to(pef, (2, 8, CMP_STRIDE * DH)).reshape(16, CMP_STRIDE * DH).astype(BF16)
    w2cat = jnp.concatenate([w2, w2], axis=1).astype(BF16)
    return w1cat, pe16, w2cat


def _wspecs(n_grid_args):
    const = (lambda *a: (0, 0))
    return [pl.BlockSpec((CMP_STRIDE * DH, 2 * CMP_HIDDEN), const), pl.BlockSpec((CMP_STRIDE * DH, 2 * CMP_HIDDEN), const),
            pl.BlockSpec((16, CMP_STRIDE * DH), const), pl.BlockSpec((16, CMP_STRIDE * DH), const),
            pl.BlockSpec((CMP_HIDDEN, LANES), const), pl.BlockSpec((CMP_HIDDEN, LANES), const)]


def _compress_prompt(kv, nb, t, cw):
    nch = t // CMP_STRIDE
    (w1k, pek, w2k), (w1v, pev, w2v) = cw
    return pl.pallas_call(
        _compress_prompt_kernel,
        grid=(nb,),
        in_specs=[pl.BlockSpec((t, LANES), lambda b: (b, 0)), pl.BlockSpec((t, LANES), lambda b: (b, 1))] + _wspecs(1),
        out_specs=pl.BlockSpec((1, 4, nch, LANES), lambda b: (b, 0, 0, 0)),
        out_shape=jax.ShapeDtypeStruct((nb, 4, nch, LANES), F32),
        scratch_shapes=[pltpu.VMEM((4, nch, CMP_STRIDE * DH), BF16)],
        compiler_params=_cparams(("arbitrary",)),
        name="compress_prompt",
    )(kv, kv, w1k, w1v, pek, pev, w2k, w2v)


def _compress_sample(cache_t, page_table, cw):
    nb, npages = page_table.shape
    page = cache_t.shape[-1]
    assert page == LANES and cache_t.shape[1:3] == (2, NSA_G * DH)
    r = npages * page
    nch = r // CMP_STRIDE
    (w1k, pek, w2k), (w1v, pev, w2v) = cw
    wspecs = [pl.BlockSpec(s.block_shape, lambda b, pt: (0, 0)) for s in _wspecs(2)]
    cpp = page // CMP_STRIDE
    perm = np.zeros((page, 4 * page), np.float32)
    for r2 in range(CMP_STRIDE // 2):
        for pg in range(2):
            for j in range(cpp):
                m = r2 * 2 * cpp + pg * cpp + j
                perm[m, pg * page + CMP_STRIDE * j + 2 * r2] = 1.0
                perm[m, 2 * page + pg * page + CMP_STRIDE * j + 2 * r2 + 1] = 1.0
    assert npages % 4 == 0 and page == CMP_STRIDE * cpp
    return pl.pallas_call(
        _compress_sample_kernel,
        grid_spec=pltpu.PrefetchScalarGridSpec(
            num_scalar_prefetch=1,
            grid=(nb,),
            in_specs=[pl.BlockSpec(memory_space=pl.ANY), pl.BlockSpec((page, 4 * page), lambda b, pt: (0, 0))] + wspecs,
            out_specs=pl.BlockSpec((1, 4, nch, LANES), lambda b, pt: (b, 0, 0, 0)),
            scratch_shapes=[pltpu.VMEM((2, npages, 2, NSA_G * DH, page), F32),
                            pltpu.VMEM((4, nch, CMP_STRIDE * DH), BF16), pltpu.SemaphoreType.DMA((2,))]),
        out_shape=jax.ShapeDtypeStruct((nb, 4, nch, LANES), F32),
        compiler_params=_cparams(("arbitrary",)),
        name="compress_sample",
    )(page_table, cache_t, jnp.asarray(perm, dtype=BF16), w1k, w1v, pek, pev, w2k, w2v)


def _softmax_update(s, m, l, acc, v, v_is_t=False):
    m_new = jnp.maximum(m, jnp.max(s, axis=1, keepdims=True))
    alpha = jnp.exp(m - m_new)
    p = jnp.exp(s - m_new)
    l = alpha * l + jnp.sum(p, axis=1, keepdims=True)
    dims = (((1,), (1,)), ((), ())) if v_is_t else (((1,), (0,)), ((), ()))
    acc = alpha * acc + lax.dot_general(p.astype(BF16), v, dims, preferred_element_type=F32)
    return m_new, l, acc


def _topk_mask(score, sidx, in_sel, lane0, nsel, k):
    rank = jnp.zeros(score.shape, I32)
    for sp in range(nsel):
        c = score[:, lane0 + sp:lane0 + sp + 1]
        beats = (c > score) | ((c == score) & (sp < sidx))
        rank = rank + beats.astype(I32)
    return in_sel & (rank < k)


def _nsa_prompt_kernel(qn_ref, qr_ref, ck_ref, cv_ref, ks_ref, vs_ref, kw_ref, vw_ref, ng_ref, cover_ref,
                       out_ref, sel_ref, s_ref, *, nsb, ksel):
    qt = pl.program_id(2)
    t0 = qt * TQ
    nck = ck_ref.shape[2]
    krow = lax.broadcasted_iota(I32, (TQ, TQ), 0)
    qcol = lax.broadcasted_iota(I32, (TQ, TQ), 1)
    tq = t0 + qcol
    nt = (((1,), (1,)), ((), ()))

    def col_reduce(x, op):
        return op(op(x.reshape(x.shape[0] // 8, 8, x.shape[1]), axis=0), axis=0, keepdims=True)

    ckb = ck_ref[0, 0].astype(BF16)
    cvt = cv_ref[0, 0].T.astype(BF16)
    cidx = lax.broadcasted_iota(I32, (nck, TQ), 0)
    cmask = (cidx * CMP_STRIDE + 2 * CMP_STRIDE - 1 <= t0 + lax.broadcasted_iota(I32, (nck, TQ), 1)) & (cidx < nck - 1)
    psum = jnp.zeros((nck, TQ), F32)
    o_cmp = []
    for h in range(NSA_HG):
        s = lax.dot_general(ckb, qn_ref[:, h * LANES:(h + 1) * LANES], nt, preferred_element_type=F32)
        s = jnp.where(cmask, s, -jnp.inf)
        m = col_reduce(s, jnp.max)
        m = jnp.where(m > -jnp.inf, m, 0.0)
        e = jnp.exp(s - m)
        d = col_reduce(e, jnp.sum)
        p = e * (1.0 / jnp.where(d > 0, d, 1.0))
        psum = psum + p
        o_cmp.append(jnp.dot(cvt, p.astype(BF16), preferred_element_type=F32)[0:DH])

    cover = cover_ref[...]
    p_hi = psum.astype(BF16)
    p_lo = (psum - p_hi.astype(F32)).astype(BF16)
    imp = jnp.dot(cover, p_hi, preferred_element_type=F32) + jnp.dot(cover, p_lo, preferred_element_type=F32)
    nsr = imp.shape[0]
    sidx = lax.broadcasted_iota(I32, (nsr, TQ), 0)
    tq_s = t0 + lax.broadcasted_iota(I32, (nsr, TQ), 1)
    cur = tq_s // SLC_BLOCK
    valid = sidx * SLC_BLOCK <= tq_s
    forced = (sidx == 0) | (sidx == cur) | (sidx == cur - 1)
    score = jnp.where(forced, SEL_FORCED, jnp.where(valid, imp, -jnp.inf))
    rank = jnp.zeros((nsr, TQ), I32)
    for sp in range(nsb):
        c = score[sp:sp + 1, :]
        rank = rank + ((c > score) | ((c == score) & (sp < sidx))).astype(I32)
    sel_ref[...] = jnp.where((rank < ksel) & (sidx < nsb), 0.0, NEG)

    ntiles = ks_ref.shape[0] // TQ
    qs = [qr_ref[:, h * LANES:(h + 1) * LANES] for h in range(NSA_HG)]
    vrows = DH + 8

    def attend(k_ref, v_ref, j_first, n_tiles, bias_fn):
        unroll = n_tiles if isinstance(n_tiles, int) else NSA_UNROLL
        n_it = (n_tiles + unroll - 1) // unroll

        def tile_of(i, u):
            idx = i * unroll + u
            j = j_first + idx
            jc = jnp.clip(j, 0, ntiles - 1)
            return idx, j, jc, pl.multiple_of(jc * TQ, TQ)

        def scores(i, mx):
            mx = list(mx)
            for u in range(unroll):
                idx, j, jc, r0 = tile_of(i, u)
                kt = k_ref[pl.ds(r0, TQ), :]
                bias = jnp.where(idx < n_tiles, bias_fn(j, jc), NEG)
                for h in range(NSA_HG):
                    s = lax.dot_general(kt, qs[h], nt, preferred_element_type=F32) + bias
                    s_ref[h, idx] = s
                    mx[h] = jnp.maximum(mx[h], jnp.max(s.reshape(TQ // 8, 8, TQ), axis=0))
            return tuple(mx)

        mx = lax.fori_loop(0, n_it, scores, tuple(jnp.full((8, TQ), NEG, F32) for _ in range(NSA_HG)))
        ms = [jnp.max(m8, axis=0, keepdims=True) for m8 in mx]

        def values(i, accs):
            accs = list(accs)
            for u in range(unroll):
                idx, j, jc, r0 = tile_of(i, u)
                vt = v_ref[0:vrows, pl.ds(r0, TQ)]
                for h in range(NSA_HG):
                    p = jnp.exp(s_ref[h, idx] - ms[h])
                    accs[h] = accs[h] + jnp.dot(vt, p.astype(BF16), preferred_element_type=F32)
            return tuple(accs)

        accs = lax.fori_loop(0, n_it, values, tuple(jnp.zeros((vrows, TQ), F32) for _ in range(NSA_HG)))
        return [acc[0:DH] / acc[DH:DH + 1] for acc in accs]

    diag_bias = jnp.where(krow <= qcol, 0.0, NEG)
    dq = qcol - krow

    def slc_bias(j, jc):
        sel = jnp.where(krow < SLC_BLOCK, sel_ref[pl.ds(2 * jc, 1), :], sel_ref[pl.ds(2 * jc + 1, 1), :])
        return sel + jnp.where(j == qt, diag_bias, 0.0)

    def win_bias(j, jc):
        off = t0 - jc * TQ
        return jnp.where((dq >= -off) & (dq < WINDOW - off) & (j >= 0), 0.0, NEG)

    nwin = WINDOW // TQ
    gates = _sigmoid(ng_ref[...]).T
    o_slc = attend(ks_ref, vs_ref, 0, qt + 1, slc_bias)
    o_win = attend(kw_ref, vw_ref, qt - nwin, nwin + 1, win_bias)
    tot = []
    for h in range(NSA_HG):
        tot.append(gates[h:h + 1] * o_cmp[h] + gates[NSA_HG + h:NSA_HG + h + 1] * o_slc[h]
                   + gates[2 * NSA_HG + h:2 * NSA_HG + h + 1] * o_win[h])
    out_ref[...] = jnp.concatenate(tot, axis=0).T.astype(BF16)


def _cover_matrix(nc, nsb, lane0, rows, cols):
    c = np.arange(rows)[:, None]
    s = np.arange(cols)[None, :] - lane0
    cov = ((c < nc) & (s >= 0) & (s < nsb) & (c * CMP_STRIDE < (s + 1) * SLC_BLOCK)
           & (c * CMP_STRIDE + 2 * CMP_STRIDE > s * SLC_BLOCK))
    return jnp.asarray(cov.astype(np.float32), dtype=BF16)


def _nsa_prompt(qn, qr, cc, kvp, vt, ng, nb, t):
    nqt = t // TQ
    nch = t // CMP_STRIDE
    nsb = t // SLC_BLOCK
    nsr = 32
    assert nsb <= nsr and SLC_BLOCK * 2 == TQ
    cover = _cover_matrix(nch - 1, nsb, 0, nch, nsr).T
    kern = functools.partial(_nsa_prompt_kernel, nsb=nsb, ksel=min(SLC_TOPK, nsb))
    return pl.pallas_call(
        kern,
        grid=(nb, NSA_G, nqt),
        in_specs=[pl.BlockSpec((TQ, NSA_HG * LANES), lambda b, g, q: (b * nqt + q, g)),
                  pl.BlockSpec((TQ, NSA_HG * LANES), lambda b, g, q: (b * nqt + q, g)),
                  pl.BlockSpec((1, 1, nch, LANES), lambda b, g, q: (b, g, 0, 0)),
                  pl.BlockSpec((1, 1, nch, LANES), lambda b, g, q: (b, 2 + g, 0, 0)),
                  pl.BlockSpec((t, LANES), lambda b, g, q: (b, 2 * g)),
                  pl.BlockSpec((None, None, LANES, t), lambda b, g, q: (g, b, 0, 0)),
                  pl.BlockSpec((t, LANES), lambda b, g, q: (b, 4 + 2 * g)),
                  pl.BlockSpec((None, None, LANES, t), lambda b, g, q: (2 + g, b, 0, 0)),
                  pl.BlockSpec((TQ, LANES), lambda b, g, q: (b * nqt + q, g)),
                  pl.BlockSpec((nsr, nch), lambda b, g, q: (0, 0))],
        out_specs=pl.BlockSpec((TQ, 2 * LANES), lambda b, g, q: (b * nqt + q, g)),
        out_shape=jax.ShapeDtypeStruct((nb * t, NSA_G * 2 * LANES), BF16),
        scratch_shapes=[pltpu.VMEM((nsr, TQ), F32),
                        pltpu.VMEM((NSA_HG, NSA_UNROLL * (-(-max(nqt, WINDOW // TQ + 1) // NSA_UNROLL)), TQ, TQ), F32)],
        compiler_params=_cparams(("arbitrary", "arbitrary", "arbitrary")),
        name="nsa_prompt",
    )(qn, qr, cc, cc, kvp, vt, kvp, vt, ng, cover)


def _nsa_sample_kernel(pt_ref, qn_ref, qr_ref, cc_ref, new_ref, win_ref, ng_ref, cover_ref, expand_ref, cache_ref,
                       out_ref, buf_ref, s_ref, sem, *, past, tnew, nsb, ksel):
    b = pl.program_id(0)
    nb = pl.num_programs(0)
    npages = pt_ref.shape[1]
    page = past // npages
    nck = cc_ref.shape[2]
    rows = NSA_HEADS * SROWS
    half = rows // 2

    def copies(bb, slot):
        return [pltpu.make_async_copy(cache_ref.at[pt_ref[bb, p]], buf_ref.at[slot, p], sem.at[slot])
                for p in range(npages)]

    @pl.when(b == 0)
    def _():
        for c in copies(0, 0):
            c.start()

    @pl.when(b + 1 < nb)
    def _():
        for c in copies(b + 1, (b + 1) % 2):
            c.start()

    lane = lax.broadcasted_iota(I32, (rows, LANES), 1)
    row = lax.broadcasted_iota(I32, (rows, LANES), 0)
    tt = row % SROWS
    lo = lane < DH

    qn = jnp.concatenate([qn_ref[:, h * LANES:(h + 1) * LANES] for h in range(NSA_HEADS)], axis=0)
    qr32 = jnp.concatenate([qr_ref[:, h * LANES:(h + 1) * LANES] for h in range(NSA_HEADS)], axis=0).astype(F32)
    qr = jnp.where(row < half, qr32, pltpu.roll(qr32, DH, 1)).astype(BF16)

    nlt = nck // LANES
    o_cmp = []
    psums = []
    for g in range(NSA_G):
        ckb = cc_ref[0, g].astype(BF16)
        cvb = cc_ref[0, 2 + g].astype(BF16)
        qg = qn[g * half:(g + 1) * half]
        s = lax.dot_general(qg, ckb, (((1,), (1,)), ((), ())), preferred_element_type=F32)
        cl = lax.broadcasted_iota(I32, s.shape, 1)
        s = jnp.where(cl < nck - 1, s, -jnp.inf)
        m = jnp.max(s, axis=1, keepdims=True)
        e = jnp.exp(s - m)
        p = e / jnp.sum(e, axis=1, keepdims=True)
        o_cmp.append(jnp.dot(p.astype(BF16), cvb, preferred_element_type=F32))
        ps = p[0:SROWS]
        for h in range(1, NSA_HG):
            ps = ps + p[h * SROWS:(h + 1) * SROWS]
        psums.append(ps)
    psum = jnp.concatenate(psums, axis=0)
    cover = cover_ref[...]
    p_hi = psum.astype(BF16)
    p_lo = (psum - p_hi.astype(F32)).astype(BF16)
    imp = jnp.dot(p_hi, cover, preferred_element_type=F32) + jnp.dot(p_lo, cover, preferred_element_type=F32)
    nsl = imp.shape[1]
    sidx = lax.broadcasted_iota(I32, imp.shape, 1)
    tpos16 = past + lax.broadcasted_iota(I32, imp.shape, 0) % SROWS
    cur = tpos16 // SLC_BLOCK
    valid = sidx * SLC_BLOCK <= tpos16
    forced = (sidx == 0) | (sidx == cur) | (sidx == cur - 1)
    score = jnp.where(forced, SEL_FORCED, jnp.where(valid, imp, -jnp.inf))
    in_sel = sidx < nsb
    selected = _topk_mask(score, sidx, in_sel, 0, nsb, ksel)
    selb16 = jnp.where(selected, 0.0, NEG)
    selrows = jnp.concatenate([selb16[g * SROWS:(g + 1) * SROWS] for g in range(NSA_G) for _ in range(NSA_HG)],
                              axis=0).astype(BF16)

    slot = b % 2
    for c in copies(b, slot):
        c.wait()

    ppt = 4
    kt_rows = ppt * page
    nkt = npages // ppt
    unroll = 2 if nkt % 2 == 0 else 1
    nt = (((1,), (1,)), ((), ()))

    def lane_fold(x, op):
        out = x[:, 0:LANES]
        for i in range(1, kt_rows // LANES):
            out = op(out, x[:, i * LANES:(i + 1) * LANES])
        return out

    newk = new_ref[:, 2 * LANES:3 * LANES].astype(BF16)
    newv = new_ref[:, 3 * LANES:4 * LANES].astype(BF16)
    kk = lax.broadcasted_iota(I32, (rows, SROWS), 1)
    tt8 = lax.broadcasted_iota(I32, (rows, SROWS), 0) % SROWS
    new_ok = (kk <= tt8) & (kk < tnew)
    nb_new = past // SLC_BLOCK
    bias_new = selrows[:, nb_new:nb_new + 1].astype(F32)
    s_new = jnp.where(new_ok, lax.dot_general(qr, newk, nt, preferred_element_type=F32) + bias_new, NEG)

    def scores(i, mx):
        for u in range(unroll):
            j = i * unroll + u
            kt = jnp.concatenate([buf_ref[slot, j * ppt + a, 0] for a in range(ppt)], axis=1).astype(BF16)
            c0 = pl.multiple_of(j * kt_rows, kt_rows)
            s = (jnp.dot(qr, kt, preferred_element_type=F32)
                 + jnp.dot(selrows, expand_ref[:, pl.ds(c0, kt_rows)], preferred_element_type=F32))
            s_ref[j] = s
            mx = jnp.maximum(mx, lane_fold(s, jnp.maximum))
        return mx

    mx = lax.fori_loop(0, nkt // unroll, scores, jnp.full((rows, LANES), NEG, F32))
    m = jnp.maximum(jnp.max(mx, axis=1, keepdims=True), jnp.max(s_new, axis=1, keepdims=True))

    def values(i, st):
        acc, ls = st
        for u in range(unroll):
            j = i * unroll + u
            vt = jnp.concatenate([buf_ref[slot, j * ppt + a, 1] for a in range(ppt)], axis=1).astype(BF16)
            p = jnp.exp(s_ref[j] - m)
            acc = acc + lax.dot_general(p.astype(BF16), vt, nt, preferred_element_type=F32)
            ls = ls + lane_fold(p, jnp.add)
        return acc, ls

    acc, ls = lax.fori_loop(0, nkt // unroll, values,
                            (jnp.zeros((rows, LANES), F32), jnp.zeros((rows, LANES), F32)))
    p_new = jnp.exp(s_new - m)
    acc = acc + jnp.dot(p_new.astype(BF16), newv, preferred_element_type=F32)
    o_slc = acc / (jnp.sum(ls, axis=1, keepdims=True) + jnp.sum(p_new, axis=1, keepdims=True))
    zero_state = (jnp.full((rows, 1), NEG, F32), jnp.zeros((rows, 1), F32), jnp.zeros((rows, LANES), F32))

    wb = win_ref.shape[3]
    wk = win_ref[0, 0].astype(BF16)
    wv = win_ref[0, 1].astype(BF16)
    wi = lax.broadcasted_iota(I32, (rows, wb), 1)
    wt = lax.broadcasted_iota(I32, (rows, wb), 0) % SROWS
    w_ok = (wt + wb - wi < WINDOW) if wb >= WINDOW else (wi >= 0)
    s = jnp.dot(qr, wk, preferred_element_type=F32)
    s = jnp.where(w_ok, s, NEG)
    st = _softmax_update(s, *zero_state, wv, v_is_t=True)
    nwk = new_ref[:, 4 * LANES:5 * LANES].astype(BF16)
    nwv = new_ref[:, 5 * LANES:6 * LANES].astype(BF16)
    s = lax.dot_general(qr, nwk, (((1,), (1,)), ((), ())), preferred_element_type=F32)
    s = jnp.where(new_ok, s, NEG)
    m, l, acc = _softmax_update(s, *st, nwv)
    o_win = acc / l

    gates = _sigmoid(ng_ref[...])
    outs = []
    for hh in range(NSA_HEADS):
        g, h = divmod(hh, NSA_HG)
        gl = gates[:, g * LANES:(g + 1) * LANES]
        g0 = gl[:, h:h + 1]
        g1 = gl[:, NSA_HG + h:NSA_HG + h + 1]
        g2 = gl[:, 2 * NSA_HG + h:2 * NSA_HG + h + 1]
        r0 = hh * SROWS
        oc = o_cmp[g][h * SROWS:(h + 1) * SROWS]
        outs.append(g0 * oc + g1 * o_slc[r0:r0 + SROWS] + g2 * o_win[r0:r0 + SROWS])
    lo8 = lax.broadcasted_iota(I32, (SROWS, LANES), 1) < DH
    out_ref[:, 0 * LANES:1 * LANES] = jnp.where(lo8, outs[0], pltpu.roll(outs[1], DH, 1)).astype(BF16)
    out_ref[:, 1 * LANES:2 * LANES] = jnp.where(lo8, outs[2], pltpu.roll(outs[3], DH, 1)).astype(BF16)
    out_ref[:, 2 * LANES:3 * LANES] = jnp.where(lo8, pltpu.roll(outs[4], DH, 1), outs[5]).astype(BF16)
    out_ref[:, 3 * LANES:4 * LANES] = jnp.where(lo8, pltpu.roll(outs[6], DH, 1), outs[7]).astype(BF16)


def _nsa_sample(qn, qr, cc, kv, win_t, ng, slc_t, page_table, past, tnew, row0):
    nb, npages = page_table.shape
    page = slc_t.shape[-1]
    assert page * npages == past and npages % 4 == 0 and page % SLC_BLOCK == 0
    nck = past // CMP_STRIDE
    nsb = -(-(past + tnew) // SLC_BLOCK)
    nsl = -(-nsb // LANES) * LANES
    assert (nck - 2) * CMP_STRIDE + 2 * CMP_STRIDE - 1 <= past and (past + tnew) // CMP_STRIDE == nck
    assert tnew <= SROWS and past // SLC_BLOCK == nsb - 1
    cover = _cover_matrix(nck - 1, nsb, 0, nck, nsl)
    expand = jnp.asarray((np.arange(nsl)[:, None] == np.arange(past)[None, :] // SLC_BLOCK).astype(np.float32), dtype=BF16)
    wb = win_t.shape[3]
    kern = functools.partial(_nsa_sample_kernel, past=past, tnew=tnew, nsb=nsb, ksel=min(SLC_TOPK, nsb))
    rb = row0 // SROWS
    return pl.pallas_call(
        kern,
        grid_spec=pltpu.PrefetchScalarGridSpec(
            num_scalar_prefetch=1,
            grid=(nb,),
            in_specs=[pl.BlockSpec((SROWS, NSA_HEADS * LANES), lambda b, pt: (rb + b, 0)),
                      pl.BlockSpec((SROWS, NSA_HEADS * LANES), lambda b, pt: (rb + b, 0)),
                      pl.BlockSpec((1, 4, nck, LANES), lambda b, pt: (b, 0, 0, 0)),
                      pl.BlockSpec((SROWS, 6 * LANES), lambda b, pt: (rb + b, 0)),
                      pl.BlockSpec((1, 2, NSA_G * DH, wb), lambda b, pt: (b, 0, 0, 0)),
                      pl.BlockSpec((SROWS, 2 * LANES), lambda b, pt: (rb + b, 0)),
                      pl.BlockSpec((nck, nsl), lambda b, pt: (0, 0)),
                      pl.BlockSpec((nsl, past), lambda b, pt: (0, 0)),
                      pl.BlockSpec(memory_space=pl.ANY)],
            out_specs=pl.BlockSpec((SROWS, NSA_HEADS * DH), lambda b, pt: (b, 0)),
            scratch_shapes=[pltpu.VMEM((2, npages, 2, NSA_G * DH, page), F32),
                            pltpu.VMEM((npages // 4, NSA_HEADS * SROWS, 4 * page), F32),
                            pltpu.SemaphoreType.DMA((2,))]),
        out_shape=jax.ShapeDtypeStruct((nb * SROWS, NSA_HEADS * DH), BF16),
        compiler_params=_cparams(("arbitrary",)),
        name="nsa_sample",
    )(page_table, qn, qr, cc, kv, win_t, ng, cover, expand, slc_t)


def _hgrn_kernel(hq_ref, hf_ref, hi_ref, hg_ref, lb_ref, nw_ref, s0_ref, o_ref, sout_ref, *, chunk, valid, nchunks):
    heads = lb_ref.shape[0]
    nw = nw_ref[...]
    row = lax.broadcasted_iota(I32, (chunk, HG_D), 0)
    rvalid = row < valid

    def head_chunk(hd, r0, st):
        cols = slice(hd * HG_D, (hd + 1) * HG_D)
        lb = lb_ref[hd]
        hq = hq_ref[pl.ds(r0, chunk), cols]
        hf = hf_ref[pl.ds(r0, chunk), cols]
        iv = hi_ref[pl.ds(r0, chunk), cols]
        hg = hg_ref[pl.ds(r0, chunk), cols]
        q = hq * _sigmoid(hq)
        f = lb + (1.0 - lb) * _sigmoid(hf)
        logf = jnp.where(rvalid, jnp.log(f), 0.0)
        k = jnp.where(rvalid, 1.0 - f, 0.0)
        b = logf
        sh = 1
        while sh < chunk:
            b = b + jnp.where(row >= sh, pltpu.roll(b, sh, 0), 0.0)
            sh *= 2
        o_in = jnp.zeros((chunk, HG_D), F32)
        for s in range(min(valid, 8)):
            e = jnp.exp(jnp.where(row >= s, b - b[s:s + 1, :], -jnp.inf))
            a = jnp.sum(q * e * k[s:s + 1, :], axis=1, keepdims=True)
            o_in = o_in + a * iv[s:s + 1, :]
        if valid > 8:
            b_hi, q_hi, row_hi = b[8:], q[8:], row[8:]
            o_hi = jnp.zeros((chunk - 8, HG_D), F32)
            for s in range(8, valid):
                e = jnp.exp(jnp.where(row_hi >= s, b_hi - b[s:s + 1, :], -jnp.inf))
                a = jnp.sum(q_hi * e * k[s:s + 1, :], axis=1, keepdims=True)
                o_hi = o_hi + a * iv[s:s + 1, :]
            o_in = jnp.concatenate([o_in[:8], o_in[8:] + o_hi], axis=0)
        qd = (q * jnp.exp(b)).astype(BF16)
        o = o_in + lax.dot_general(qd, st.astype(BF16), (((1,), (1,)), ((), ())), preferred_element_type=F32)
        bl = b[chunk - 1:chunk, :]
        kd = (k * jnp.exp(bl - b)).astype(BF16)
        u = lax.dot_general(iv.astype(BF16), kd, (((0,), (0,)), ((), ())), preferred_element_type=F32)
        st = st * jnp.exp(bl) + u
        ms = jnp.mean(o * o, axis=1, keepdims=True)
        y = o * lax.rsqrt(ms + RMS_EPS) * nw * (hg * _sigmoid(hg))
        o_ref[pl.ds(r0, chunk), cols] = y.astype(BF16)
        return st

    def body(i, sts):
        r0 = pl.multiple_of(i * chunk, chunk)
        return tuple(head_chunk(hd, r0, sts[hd]) for hd in range(heads))

    sts = lax.fori_loop(0, nchunks, body, tuple(s0_ref[0, hd].T for hd in range(heads)),
                        unroll=min(max(1, 8 // heads), nchunks))
    for hd in range(heads):
        sout_ref[0, hd] = sts[hd].T


def _hgrn(hh, lb, nw, s0, nseq, rows_per_seq, row0, chunk, valid, heads_per_step):
    rb = row0 // rows_per_seq
    hps = heads_per_step
    ngrp = HG_HEADS // hps
    kern = functools.partial(_hgrn_kernel, chunk=chunk, valid=valid, nchunks=rows_per_seq // chunk)
    blk = lambda k: pl.BlockSpec((rows_per_seq, hps * HG_D), lambda b, h, k=k: (rb + b, k * ngrp + h))
    return pl.pallas_call(
        kern,
        grid=(nseq, ngrp),
        in_specs=[blk(0), blk(1), blk(2), blk(3),
                  pl.BlockSpec((hps, 1, HG_D), lambda b, h: (h, 0, 0)),
                  pl.BlockSpec((1, HG_D), lambda b, h: (0, 0)),
                  pl.BlockSpec((1, hps, HG_D, HG_D), lambda b, h: (b, h, 0, 0))],
        out_specs=[pl.BlockSpec((rows_per_seq, hps * HG_D), lambda b, h: (b, h)),
                   pl.BlockSpec((1, hps, HG_D, HG_D), lambda b, h: (b, h, 0, 0))],
        out_shape=[jax.ShapeDtypeStruct((nseq * rows_per_seq, HG_HEADS * HG_D), BF16),
                   jax.ShapeDtypeStruct((nseq, HG_HEADS, HG_D, HG_D), F32)],
        compiler_params=_cparams(("arbitrary", "arbitrary")),
        name="hgrn",
    )(hh, hh, hh, hh, lb, nw, s0)


def _layernorm(y, g, b):
    mu = jnp.mean(y, axis=1, keepdims=True)
    yc = y - mu
    var = jnp.mean(yc * yc, axis=1, keepdims=True)
    return yc * lax.rsqrt(var + LN_EPS) * g + b


def _finish_kernel(xp_ref, xs_ref, ap_ref, as_ref, hp_ref, hs_ref, gg_ref, wpa_ref, wpb_ref, wo_ref, g1_ref, b1_ref,
                   wr_ref, br_ref, ltri_ref, h_ref, topi_ref, topw_ref, pos_ref, cnt_ref, carry_ref, *, n_prompt_tiles):
    i = pl.program_id(0)

    @pl.when(i == 0)
    def _():
        carry_ref[...] = jnp.zeros_like(carry_ref)

    a = jnp.dot(_pick_rows(ap_ref, as_ref, n_prompt_tiles), wpa_ref[...], preferred_element_type=F32)
    bb = jnp.dot(_pick_rows(hp_ref, hs_ref, n_prompt_tiles), wpb_ref[...], preferred_element_type=F32)
    m = gg_ref[:, :D_MODEL] * a + gg_ref[:, D_MODEL:] * bb
    mix = jnp.dot(m.astype(BF16), wo_ref[...], preferred_element_type=F32)
    h = _layernorm(DN_ALPHA * _pick_rows(xp_ref, xs_ref, n_prompt_tiles) + mix, g1_ref[...], b1_ref[...])
    h_ref[...] = h

    h1 = h.astype(BF16)
    h2 = (h - h1.astype(F32)).astype(BF16)
    w = wr_ref[...]
    w1 = w.astype(BF16)
    w2 = (w - w1.astype(F32)).astype(BF16)
    lg = (jnp.dot(h1, w1, preferred_element_type=F32) + jnp.dot(h1, w2, preferred_element_type=F32)
          + jnp.dot(h2, w1, preferred_element_type=F32)) + br_ref[...]
    tm = lg.shape[0]
    lane = lax.broadcasted_iota(I32, (tm, LANES), 1)
    lg = jnp.where(lane < N_EXPERTS, lg, -jnp.inf)
    vals, idxs = [], []
    for _ in range(TOP_K):
        mx = jnp.max(lg, axis=1, keepdims=True)
        ix = jnp.min(jnp.where(lg == mx, lane, LANES), axis=1, keepdims=True)
        vals.append(mx)
        idxs.append(ix)
        lg = jnp.where(lane == ix, -jnp.inf, lg)
    es = [jnp.exp(v - vals[0]) for v in vals]
    den = es[0] + es[1] + es[2] + es[3]
    topi = jnp.zeros((tm, LANES), I32)
    topw = jnp.zeros((tm, LANES), F32)
    onehot = jnp.zeros((tm, LANES), F32)
    for j in range(TOP_K):
        topi = jnp.where(lane == j, idxs[j], topi)
        topw = jnp.where(lane == j, es[j] / den, topw)
        onehot = onehot + (lane == idxs[j]).astype(F32)
    topi_ref[...] = topi
    topw_ref[...] = topw
    prefix = jnp.dot(ltri_ref[...], onehot.astype(BF16), preferred_element_type=F32) + carry_ref[0:1, :]
    pos = jnp.zeros((tm, LANES), I32)
    for j in range(TOP_K):
        pj = jnp.sum(jnp.where(lane == idxs[j], prefix, 0.0), axis=1, keepdims=True)
        pos = jnp.where(lane == j, pj.astype(I32), pos)
    pos_ref[...] = pos
    carry_ref[...] = carry_ref[...] + jnp.sum(onehot, axis=0, keepdims=True)
    cnt_ref[...] = carry_ref[...]


def _finish(xp, xs, nsa_p, nsa_s, hg_p, hg_s, gg, wpa, wpb, wo, g1, b1, wr, br):
    npt = xp.shape[0] // TM
    n = xp.shape[0] + xs.shape[0]
    row = lambda i: (i, 0)
    const = lambda i: (0, 0)
    ltri = jnp.asarray(np.tril(np.ones((TM, TM), np.float32), -1), dtype=BF16)
    outs = pl.pallas_call(
        functools.partial(_finish_kernel, n_prompt_tiles=npt),
        grid=(n // TM,),
        in_specs=_row_specs(npt) + _row_specs(npt, 512) + _row_specs(npt, 512) + [
                  pl.BlockSpec((TM, 2 * D_MODEL), row),
                  pl.BlockSpec((512, D_MODEL), const), pl.BlockSpec((512, D_MODEL), const),
                  pl.BlockSpec((D_MODEL, D_MODEL), const),
                  pl.BlockSpec((1, D_MODEL), const), pl.BlockSpec((1, D_MODEL), const),
                  pl.BlockSpec((D_MODEL, LANES), const), pl.BlockSpec((1, LANES), const),
                  pl.BlockSpec((TM, TM), const)],
        out_specs=[pl.BlockSpec((TM, D_MODEL), row), pl.BlockSpec((TM, LANES), row), pl.BlockSpec((TM, LANES), row),
                   pl.BlockSpec((TM, LANES), row), pl.BlockSpec((8, LANES), const)],
        out_shape=[jax.ShapeDtypeStruct((n, D_MODEL), F32), jax.ShapeDtypeStruct((n, LANES), I32),
                   jax.ShapeDtypeStruct((n, LANES), F32), jax.ShapeDtypeStruct((n, LANES), I32),
                   jax.ShapeDtypeStruct((8, LANES), F32)],
        scratch_shapes=[pltpu.VMEM((8, LANES), F32)],
        compiler_params=_cparams(("arbitrary",)),
        name="finish",
    )(xp, xs, nsa_p, nsa_s, hg_p, hg_s, gg, wpa, wpb, wo, g1, b1, wr, br, ltri)
    return outs


def _dispatch_kernel(padrow_ref, nused_ref, dest_ref, h_ref, xs_ref, zero_ref, sem):
    @pl.when(pl.program_id(0) == 0)
    def _():
        zero_ref[...] = jnp.zeros_like(zero_ref)
        fills = [pltpu.make_async_copy(zero_ref, xs_ref.at[pl.ds(pl.multiple_of(padrow_ref[e], 8), TE + 8), :], sem)
                 for e in range(N_EXPERTS)]
        for c in fills:
            c.start()
        for c in fills:
            c.wait()
        ntiles = (xs_ref.shape[0] - TE - 8) // TE

        def fill_tile(k, c):
            cp = pltpu.make_async_copy(zero_ref, xs_ref.at[pl.ds(pl.multiple_of(k * TE, TE), TE + 8), :], sem)
            cp.start()
            cp.wait()
            return c

        lax.fori_loop(nused_ref[0], ntiles + 1, fill_tile, 0)

    def copy(r, j):
        d = dest_ref[0, 0, r * TOP_K + j]
        return pltpu.make_async_copy(h_ref.at[pl.ds(r, 1), :], xs_ref.at[pl.ds(d, 1), :], sem)

    def start(r, c):
        for j in range(TOP_K):
            copy(r, j).start(priority=j % 2)
        return c

    def wait(r, c):
        for j in range(TOP_K):
            copy(r, j).wait()
        return c

    lax.fori_loop(0, TD, start, 0)
    lax.fori_loop(0, TD, wait, 0)


def _dispatch(padrow, nused, dest3, h, nrows):
    n = h.shape[0]
    return pl.pallas_call(
        _dispatch_kernel,
        grid_spec=pltpu.PrefetchScalarGridSpec(
            num_scalar_prefetch=2,
            grid=(n // TD,),
            in_specs=[pl.BlockSpec((1, 1, TD * TOP_K), lambda i, pr, nu: (i, 0, 0), memory_space=pltpu.SMEM),
                      pl.BlockSpec((TD, D_MODEL), lambda i, pr, nu: (i, 0))],
            out_specs=pl.BlockSpec(memory_space=pl.ANY),
            scratch_shapes=[pltpu.VMEM((TE + 8, D_MODEL), F32), pltpu.SemaphoreType.DMA(())]),
        out_shape=jax.ShapeDtypeStruct((nrows + TE + 8, D_MODEL), F32),
        compiler_params=_cparams(("arbitrary",)),
        name="dispatch",
    )(padrow, nused, dest3, h)


def _experts_kernel(bexp_ref, nused_ref, x_ref, wgu_ref, bgu_ref, wdn_ref, bdn_ref, y_ref, wgu_b, wdn_b):
    i = pl.program_id(0)
    cw = 256
    used = i < nused_ref[0]

    @pl.when(used & ((i == 0) | (bexp_ref[i] != bexp_ref[jnp.maximum(i - 1, 0)])))
    def _():
        for c in range(2 * D_FF // cw):
            wgu_b[:, c * cw:(c + 1) * cw] = wgu_ref[0, :, c * cw:(c + 1) * cw].astype(BF16)
        for c in range(D_MODEL // cw):
            wdn_b[:, c * cw:(c + 1) * cw] = wdn_ref[0, :, c * cw:(c + 1) * cw].astype(BF16)

    @pl.when(used)
    def _():
        xb = x_ref[...].astype(BF16)
        acc = jnp.zeros((TE, D_MODEL), F32)
        for c in range(D_FF // cw):
            gt = jnp.dot(xb, wgu_b[:, c * cw:(c + 1) * cw], preferred_element_type=F32) + bgu_ref[0, :, c * cw:(c + 1) * cw]
            up = (jnp.dot(xb, wgu_b[:, D_FF + c * cw:D_FF + (c + 1) * cw], preferred_element_type=F32)
                  + bgu_ref[0, :, D_FF + c * cw:D_FF + (c + 1) * cw])
            gt = jnp.minimum(gt, SWIGLU_LIMIT)
            up = jnp.clip(up, -SWIGLU_LIMIT, SWIGLU_LIMIT)
            act = (up + 1.0) * (gt * _sigmoid(SWIGLU_ALPHA * gt))
            acc = acc + jnp.dot(act.astype(BF16), wdn_b[c * cw:(c + 1) * cw, :], preferred_element_type=F32)
        y_ref[...] = acc + bdn_ref[0]

    @pl.when(i >= nused_ref[0])
    def _():
        y_ref[...] = jnp.zeros_like(y_ref)


def _experts(bexp, nused, xs, wgu, bgu, wdn, bdn):
    nt = bexp.shape[0]
    return pl.pallas_call(
        _experts_kernel,
        grid_spec=pltpu.PrefetchScalarGridSpec(
            num_scalar_prefetch=2,
            grid=(nt,),
            in_specs=[pl.BlockSpec((TE, D_MODEL), lambda i, be, nu: (jnp.minimum(i, nu[0] - 1), 0)),
                      pl.BlockSpec((1, D_MODEL, 2 * D_FF), lambda i, be, nu: (be[i], 0, 0)),
                      pl.BlockSpec((1, 1, 2 * D_FF), lambda i, be, nu: (be[i], 0, 0)),
                      pl.BlockSpec((1, D_FF, D_MODEL), lambda i, be, nu: (be[i], 0, 0)),
                      pl.BlockSpec((1, 1, D_MODEL), lambda i, be, nu: (be[i], 0, 0))],
            out_specs=pl.BlockSpec((TE, D_MODEL), lambda i, be, nu: (i, 0)),
            scratch_shapes=[pltpu.VMEM((D_MODEL, 2 * D_FF), BF16), pltpu.VMEM((D_FF, D_MODEL), BF16)]),
        out_shape=jax.ShapeDtypeStruct((nt * TE, D_MODEL), F32),
        compiler_params=_cparams(("arbitrary",)),
        name="experts",
    )(bexp, nused, xs, wgu, bgu, wdn, bdn)


def _combine_kernel(dest_ref, dnext_ref, h_ref, topw_ref, g2_ref, b2_ref, ys_ref, outp_ref, outs_ref, buf_ref, sem,
                    *, n_prompt_tiles):
    i = pl.program_id(0)
    n = pl.num_programs(0)

    def copy(d_ref, slot, r, j):
        d = d_ref[0, 0, r * TOP_K + j]
        return pltpu.make_async_copy(ys_ref.at[pl.ds(d, 1), :], buf_ref.at[slot, j, pl.ds(r, 1), :], sem.at[slot])

    def start_all(d_ref, slot):
        def body(r, c):
            for j in range(TOP_K):
                copy(d_ref, slot, r, j).start(priority=j % 2)
            return c
        lax.fori_loop(0, TD, body, 0)

    @pl.when(i == 0)
    def _():
        start_all(dest_ref, 0)

    @pl.when(i + 1 < n)
    def _():
        start_all(dnext_ref, (i + 1) % 2)

    slot = i % 2

    def wait(r, c):
        for j in range(TOP_K):
            copy(dest_ref, slot, r, j).wait()
        return c

    lax.fori_loop(0, TD, wait, 0)
    tw = topw_ref[...]
    moe = tw[:, 0:1] * buf_ref[slot, 0]
    for j in range(1, TOP_K):
        moe = moe + tw[:, j:j + 1] * buf_ref[slot, j]
    y = _layernorm(DN_ALPHA * h_ref[...] + moe, g2_ref[...], b2_ref[...])

    @pl.when(i < n_prompt_tiles)
    def _():
        outp_ref[...] = y

    @pl.when(i >= n_prompt_tiles)
    def _():
        outs_ref[...] = y


def _combine(dest3, h, topw, g2, b2, ys, n_p):
    n = h.shape[0]
    last = n // TD - 1
    npt = n_p // TD
    return pl.pallas_call(
        functools.partial(_combine_kernel, n_prompt_tiles=npt),
        grid=(n // TD,),
        in_specs=[pl.BlockSpec((1, 1, TD * TOP_K), lambda i: (i, 0, 0), memory_space=pltpu.SMEM),
                  pl.BlockSpec((1, 1, TD * TOP_K), lambda i: (jnp.minimum(i + 1, last), 0, 0), memory_space=pltpu.SMEM),
                  pl.BlockSpec((TD, D_MODEL), lambda i: (i, 0)),
                  pl.BlockSpec((TD, LANES), lambda i: (i, 0)),
                  pl.BlockSpec((1, D_MODEL), lambda i: (0, 0)), pl.BlockSpec((1, D_MODEL), lambda i: (0, 0)),
                  pl.BlockSpec(memory_space=pl.ANY)],
        out_specs=[pl.BlockSpec((TD, D_MODEL), lambda i: (jnp.minimum(i, npt - 1), 0)),
                   pl.BlockSpec((TD, D_MODEL), lambda i: (jnp.maximum(i - npt, 0), 0))],
        out_shape=[jax.ShapeDtypeStruct((n_p, D_MODEL), F32), jax.ShapeDtypeStruct((n - n_p, D_MODEL), F32)],
        scratch_shapes=[pltpu.VMEM((2, TOP_K, TD, D_MODEL), F32), pltpu.SemaphoreType.DMA((2,))],
        compiler_params=_cparams(("arbitrary",)),
        name="combine",
    )(dest3, dest3, h, topw, g2, b2, ys)


def kernel(x_prompt, x_sample, cache_cmp_kv, cache_slc_kv, cache_win_kv, state_hgrn, page_table, w_in, pe_ck, w_ck1,
           w_ck2, pe_cv, w_cv1, w_cv2, hg_lb_logits, hg_norm_w, w_pa, w_pb, w_o, ln1_g, ln1_b, w_router, b_router,
           w_gu, b_gu, w_dn, b_dn, ln2_g, ln2_b):
    nb, t, d = x_prompt.shape
    ndb, tnew, _ = x_sample.shape
    depth, npool, page = cache_cmp_kv.shape[:3]
    past = page_table.shape[1] * page
    assert depth == 1 and d == D_MODEL and t % TM == 0 and (ndb * SROWS) % TM == 0 and tnew <= SROWS
    n_p = nb * t
    n_s = ndb * SROWS
    n = n_p + n_s

    xp = x_prompt.reshape(n_p, d)
    xs_pad = jnp.pad(x_sample, ((0, 0), (0, SROWS - tnew), (0, 0))).reshape(n_s, d)

    w_all = _repack_w_in(w_in[0])
    tables = _rope_tables(t, past)
    qn, qr, kv, kvp, ng, hh, gg = _project(xp, xs_pad, w_all, tables, t)

    cw = (_compress_weights(pe_ck[0], w_ck1[0], w_ck2[0]), _compress_weights(pe_cv[0], w_cv1[0], w_cv2[0]))
    cc_p = _compress_prompt(kv, nb, t, cw)

    def rows_minor(c):
        return jnp.transpose(c, (0, 2, 3, 4, 1)).reshape(c.shape[0], 2, NSA_G * DH, c.shape[1])

    cc_s = _compress_sample(rows_minor(cache_cmp_kv[0]), page_table, cw)
    vt = jnp.stack([kvp[:n_p, (2 * i + 1) * LANES:(2 * i + 2) * LANES].reshape(nb, t, LANES).transpose(0, 2, 1)
                    for i in range(4)], axis=0)
    nsa_p = _nsa_prompt(qn, qr, cc_p, kvp, vt, ng, nb, t)
    nsa_s = _nsa_sample(qn, qr, cc_s, kv, rows_minor(cache_win_kv[0]), ng, rows_minor(cache_slc_kv[0]),
                        page_table, past, tnew, n_p)

    lb_all = jnp.cumsum(jax.nn.softmax(hg_lb_logits.astype(F32), axis=0), axis=0)
    lb = lb_all[0].reshape(HG_HEADS, 1, HG_D)
    nw = hg_norm_w[0].reshape(1, HG_D)
    hg_p, st_p = _hgrn(hh, lb, nw, jnp.zeros((nb, HG_HEADS, HG_D, HG_D), F32), nb, t, 0, HG_CHUNK, HG_CHUNK, 2)
    hg_s, st_s = _hgrn(hh, lb, nw, state_hgrn[0], ndb, SROWS, n_p, SROWS, tnew, HG_HEADS)

    wr = jnp.pad(w_router[0], ((0, 0), (0, LANES - N_EXPERTS)))
    br = jnp.pad(b_router[0], (0, LANES - N_EXPERTS)).reshape(1, LANES)
    h, topi, topw, pos, cnt = _finish(xp, xs_pad, nsa_p, nsa_s, hg_p, hg_s, gg, w_pa[0].astype(BF16),
                                      w_pb[0].astype(BF16), w_o[0].astype(BF16), ln1_g[0].reshape(1, d),
                                      ln1_b[0].reshape(1, d), wr, br)

    counts = cnt[0, :N_EXPERTS].astype(I32)
    pcounts = (counts + TE - 1) // TE * TE
    pend = jnp.cumsum(pcounts)
    pstart = pend - pcounts
    nt = (n * TOP_K) // TE + N_EXPERTS
    bexp = jnp.minimum(jnp.sum(jnp.arange(nt, dtype=I32)[:, None] * TE >= pend[None, :], axis=1), N_EXPERTS - 1).astype(I32)
    nused = (pend[-1:] // TE).astype(I32)
    dest = pstart[topi[:, :TOP_K]] + pos[:, :TOP_K]
    dest3 = dest.reshape(n // TD, 1, TD * TOP_K)

    xs = _dispatch(((pstart + counts) // 8 * 8).astype(I32), nused, dest3, h, nt * TE)
    ys = _experts(bexp, nused, xs, w_gu[0], b_gu[0].reshape(N_EXPERTS, 1, 2 * D_FF),
                  w_dn[0], b_dn[0].reshape(N_EXPERTS, 1, D_MODEL))
    y_p, y_s = _combine(dest3, h, topw, ln2_g[0].reshape(1, d), ln2_b[0].reshape(1, d), ys, n_p)

    y_p = y_p.reshape(nb, t, d)
    y_s = y_s.reshape(ndb, SROWS, d)[:, :tnew]
    kv_p = kv[:n_p].reshape(nb, t, 3, 2, NSA_G, DH)
    kv_s = kv[n_p:].reshape(ndb, SROWS, 3, 2, NSA_G, DH)[:, :tnew]
    keep = min(WINDOW, t)
    win_s = jnp.concatenate([cache_win_kv[0], kv_s[:, :, 2]], axis=1)[:, tnew:]
    return (y_p, y_s, kv_p[:, :, 0][None], kv_s[:, :, 0][None], kv_p[:, :, 1][None], kv_s[:, :, 1][None],
            kv_p[:, t - keep:, 2][None], win_s[None], st_p[None], st_s[None])
```

```python
import functools
import math

import numpy as np
import jax
import jax.numpy as jnp
from jax import lax
from jax.experimental import pallas as pl
from jax.experimental.pallas import tpu as pltpu

F32 = jnp.float32
BF16 = jnp.bfloat16
I32 = jnp.int32

D_MODEL = 1024
NSA_HEADS = 8
NSA_G = 2
NSA_HG = 4
DH = 64
CMP_STRIDE = 16
CMP_HIDDEN = 256
SLC_BLOCK = 64
SLC_TOPK = 16
WINDOW = 512
SEL_FORCED = 1.0e4
ROPE_THETA = 10000.0
HG_HEADS = 4
HG_D = 128
N_EXPERTS = 32
TOP_K = 4
D_FF = 1024
SWIGLU_LIMIT = 7.0
SWIGLU_ALPHA = 1.702
LN_EPS = 1e-5
RMS_EPS = 1e-6
DN_ALPHA = 2.0 ** 0.25

LANES = 128
TM = 256
TQ = 256
TKEY = 128
SROWS = 8
NSA_UNROLL = 4
HG_CHUNK = 16
TE = 512
TD = 128
NEG = -1.0e30
VMEM_LIMIT = 56 * 1024 * 1024

_C_Q, _C_KV, _C_PK, _C_NG, _C_HH, _C_GG, _C_END = 0, 1024, 1792, 2816, 3072, 5120, 7168


def _sigmoid(x):
    return 1.0 / (1.0 + jnp.exp(-x))


def _cparams(sem, vmem=VMEM_LIMIT):
    return pltpu.CompilerParams(dimension_semantics=sem, vmem_limit_bytes=vmem)


def _repack_w_in(w_in):
    d = w_in.shape[0]
    z64 = jnp.zeros((d, DH), F32)
    q = w_in[:, :512].reshape(d, NSA_HEADS, DH) * (DH ** -0.5)
    qpad = jnp.concatenate([q, jnp.zeros_like(q)], axis=-1).reshape(d, NSA_HEADS * LANES)
    kv = w_in[:, 512:1280]
    kv6 = kv.reshape(d, 6, NSA_G, DH)
    packs = []
    for kidx, vidx in ((2, 3), (4, 5)):
        for g in range(NSA_G):
            packs += [kv6[:, kidx, g], z64, kv6[:, vidx, g], z64]
    ng = w_in[:, 1280:1304].reshape(d, 3, NSA_G, NSA_HG)
    ngp = []
    for g in range(NSA_G):
        ngp += [ng[:, :, g, :].reshape(d, 3 * NSA_HG), jnp.zeros((d, LANES - 3 * NSA_HG), F32)]
    w = jnp.concatenate([qpad, kv] + packs + ngp + [w_in[:, 1304:3352], w_in[:, 3352:5400]], axis=-1)
    assert w.shape[1] == _C_END
    return w.astype(BF16)


def _rope_tables(t_prompt, past):
    half = DH // 2
    inv = ROPE_THETA ** (-jnp.arange(half, dtype=F32) / half)
    pos = jnp.concatenate([jnp.arange(t_prompt, dtype=I32), past + (jnp.arange(TM, dtype=I32) % SROWS)])
    ang = pos.astype(F32)[:, None] * inv[None, :]
    cos, sin = jnp.cos(ang), jnp.sin(ang)
    cos64 = jnp.concatenate([cos, cos], axis=-1)
    sin64 = jnp.concatenate([-sin, sin], axis=-1)
    one, zero = jnp.ones_like(cos64), jnp.zeros_like(cos64)
    cos_a = jnp.concatenate([cos64, cos64], axis=-1)
    sin_a = jnp.concatenate([sin64, sin64], axis=-1)
    cos_b = jnp.concatenate([cos64, one], axis=-1)
    sin_b = jnp.concatenate([sin64, zero], axis=-1)
    return cos_a, sin_a, cos_b, sin_b


def _pick_rows(xp_ref, xs_ref, n_prompt_tiles):
    return jnp.where(pl.program_id(0) < n_prompt_tiles, xp_ref[...], xs_ref[...])


def _row_specs(n_prompt_tiles, width=D_MODEL):
    return [pl.BlockSpec((TM, width), lambda i: (jnp.minimum(i, n_prompt_tiles - 1), 0)),
            pl.BlockSpec((TM, width), lambda i: (jnp.maximum(i - n_prompt_tiles, 0), 0))]


def _proj_kernel(xp_ref, xs_ref, w_ref, ca_ref, sa_ref, cb_ref, sb_ref,
                 qn_ref, qr_ref, kv_ref, kvp_ref, ng_ref, hh_ref, gg_ref, *, n_prompt_tiles):
    xb = _pick_rows(xp_ref, xs_ref, n_prompt_tiles).astype(BF16)
    tm = xb.shape[0]
    lane = lax.broadcasted_iota(I32, (tm, LANES), 1)
    first = (lane % DH) < (DH // 2)
    ca, sa, cb, sb = ca_ref[...], sa_ref[...], cb_ref[...], sb_ref[...]
    ones_hi = (lane >= DH).astype(F32)

    def rope(a, c, s):
        partner = jnp.where(first, pltpu.roll(a, LANES - DH // 2, 1), pltpu.roll(a, DH // 2, 1))
        return a * c + partner * s

    def mm(c0, n):
        return jnp.dot(xb, w_ref[:, c0:c0 + n], preferred_element_type=F32)

    for c in range(2):
        acc = mm(_C_Q + c * 512, 512)
        for j in range(4):
            a = acc[:, j * LANES:(j + 1) * LANES]
            col = c * 512 + j * LANES
            qn_ref[:, col:col + LANES] = a.astype(BF16)
            qr_ref[:, col:col + LANES] = rope(a, cb, sb).astype(BF16)
    acc = mm(_C_KV, 768)
    for j in range(6):
        a = acc[:, j * LANES:(j + 1) * LANES]
        if j in (2, 4):
            a = rope(a, ca, sa)
        kv_ref[:, j * LANES:(j + 1) * LANES] = a
    for c in range(2):
        acc = mm(_C_PK + c * 512, 512)
        for g in range(NSA_G):
            k = rope(acc[:, g * 256:g * 256 + LANES], cb, sb)
            col = c * 512 + g * 256
            kvp_ref[:, col:col + LANES] = k.astype(BF16)
            v1 = acc[:, g * 256 + LANES:g * 256 + 2 * LANES] + ones_hi
            kvp_ref[:, col + LANES:col + 2 * LANES] = v1.astype(BF16)
    ng_ref[...] = mm(_C_NG, 256)
    for c in range(4):
        hh_ref[:, c * 512:(c + 1) * 512] = mm(_C_HH + c * 512, 512)
    for c in range(4):
        gg_ref[:, c * 512:(c + 1) * 512] = _sigmoid(mm(_C_GG + c * 512, 512))


def _project(xp, xs, w_all, tables, t_prompt):
    n_prompt_tiles = xp.shape[0] // TM
    n = xp.shape[0] + xs.shape[0]
    tp = t_prompt // TM

    def tab_map(i):
        return (jnp.where(i < n_prompt_tiles, i % tp, tp), 0)

    row = lambda i: (i, 0)
    tab_spec = pl.BlockSpec((TM, LANES), tab_map)
    widths = (1024, 1024, 768, 1024, 256, 2048, 2048)
    dtypes = (BF16, BF16, F32, BF16, F32, F32, F32)
    return pl.pallas_call(
        functools.partial(_proj_kernel, n_prompt_tiles=n_prompt_tiles),
        grid=(n // TM,),
        in_specs=_row_specs(n_prompt_tiles) + [pl.BlockSpec((D_MODEL, _C_END), lambda i: (0, 0))] + [tab_spec] * 4,
        out_specs=[pl.BlockSpec((TM, w), row) for w in widths],
        out_shape=[jax.ShapeDtypeStruct((n, w), dt) for w, dt in zip(widths, dtypes)],
        compiler_params=_cparams(("arbitrary",)),
        name="proj",
    )(xp, xs, w_all, *tables)


def _gelu_tanh(x):
    return x * (0.5 * (1.0 + jnp.tanh(math.sqrt(2.0 / math.pi) * (x + 0.044715 * (x * x * x)))))


def _compress_core(xk_ref, xv_ref, z_ref, w1k_ref, w1v_ref, pek_ref, pev_ref, w2k_ref, w2v_ref, out_ref):
    nch = xk_ref.shape[0] // CMP_STRIDE
    lane = lax.broadcasted_iota(I32, (nch, LANES), 1)
    lo = lane < DH
    for p, xr in enumerate((xk_ref, xv_ref)):
        for r2 in range(CMP_STRIDE // 2):
            a = xr[pl.ds(2 * r2, nch, stride=CMP_STRIDE), :]
            b = xr[pl.ds(2 * r2 + 1, nch, stride=CMP_STRIDE), :]
            z0 = jnp.where(lo, a, pltpu.roll(b, DH, 1))
            z1 = jnp.where(lo, pltpu.roll(a, DH, 1), b)
            z_ref[2 * p, :, r2 * LANES:(r2 + 1) * LANES] = z0.astype(BF16)
            z_ref[2 * p + 1, :, r2 * LANES:(r2 + 1) * LANES] = z1.astype(BF16)
    _compress_mlp(z_ref, w1k_ref, w1v_ref, pek_ref, pev_ref, w2k_ref, w2v_ref, out_ref)


def _compress_mlp(z_ref, w1k_ref, w1v_ref, pek_ref, pev_ref, w2k_ref, w2v_ref, out_ref):
    nch = z_ref.shape[1]
    for s in range(4):
        w1, pe, w2 = (w1k_ref, pek_ref, w2k_ref) if s < 2 else (w1v_ref, pev_ref, w2v_ref)
        pq = jnp.dot(z_ref[s], w1[...], preferred_element_type=F32)
        pb = jnp.dot(pe[...], w1[...], preferred_element_type=F32)
        bias = pb[0:1, :CMP_HIDDEN] + pb[8:9, CMP_HIDDEN:]
        h = pq[:, :CMP_HIDDEN] + pltpu.roll(pq[:, CMP_HIDDEN:], nch - 1, 0) + bias
        out_ref[s] = jnp.dot(_gelu_tanh(h).astype(BF16), w2[...], preferred_element_type=F32)


def _compress_prompt_kernel(xk_ref, xv_ref, w1k_ref, w1v_ref, pek_ref, pev_ref, w2k_ref, w2v_ref, out_ref, z_ref):
    _compress_core(xk_ref, xv_ref, z_ref, w1k_ref, w1v_ref, pek_ref, pev_ref, w2k_ref, w2v_ref, out_ref.at[0])


def _compress_sample_kernel(pt_ref, cache_ref, perm_ref, w1k_ref, w1v_ref, pek_ref, pev_ref, w2k_ref, w2v_ref,
                            out_ref, buf_ref, z_ref, sem):
    b = pl.program_id(0)
    nb = pl.num_programs(0)
    npages = pt_ref.shape[1]
    page = buf_ref.shape[-1]

    def copies(bb, slot):
        return [pltpu.make_async_copy(cache_ref.at[pt_ref[bb, p]], buf_ref.at[slot, p], sem.at[slot])
                for p in range(npages)]

    @pl.when(b == 0)
    def _():
        for c in copies(0, 0):
            c.start()

    @pl.when(b + 1 < nb)
    def _():
        for c in copies(b + 1, (b + 1) % 2):
            c.start()

    slot = b % 2
    for c in copies(b, slot):
        c.wait()

    cpp = page // CMP_STRIDE
    zero = jnp.zeros((DH, 2 * page), BF16)

    def build_z(pp, carry):
        c0 = pl.multiple_of(pp * 2 * cpp, 2 * cpp)
        for kv in range(2):
            xt = jnp.concatenate([buf_ref[slot, 2 * pp, kv], buf_ref[slot, 2 * pp + 1, kv]], axis=1).astype(BF16)
            for g in range(NSA_G):
                xg = xt[g * DH:(g + 1) * DH]
                y = jnp.concatenate([jnp.concatenate([xg, zero], axis=1), jnp.concatenate([zero, xg], axis=1)], axis=0)
                px = lax.dot_general(perm_ref[...], y, (((1,), (1,)), ((), ())), preferred_element_type=F32)
                for r2 in range(CMP_STRIDE // 2):
                    z_ref[2 * kv + g, pl.ds(c0, 2 * cpp), r2 * LANES:(r2 + 1) * LANES] = (
                        px[r2 * 2 * cpp:(r2 + 1) * 2 * cpp].astype(BF16))
        return carry

    lax.fori_loop(0, npages // 2, build_z, 0, unroll=4)
    _compress_mlp(z_ref, w1k_ref, w1v_ref, pek_ref, pev_ref, w2k_ref, w2v_ref, out_ref.at[0])


def _compress_weights(pe, w1, w2):
    w1cat = jnp.concatenate([w1[:CMP_STRIDE * DH], w1[CMP_STRIDE * DH:]], axis=1).astype(BF16)
    pef = pe.reshape(2, 1, CMP_STRIDE * DH)
    pe16 = jnp.broadcast_to(pef, (2, 8, CMP_STRIDE * DH)).reshape(16, CMP_STRIDE * DH).astype(BF16)
    w2cat = jnp.concatenate([w2, w2], axis=1).astype(BF16)
    return w1cat, pe16, w2cat


def _wspecs(n_grid_args):
    const = (lambda *a: (0, 0))
    return [pl.BlockSpec((CMP_STRIDE * DH, 2 * CMP_HIDDEN), const), pl.BlockSpec((CMP_STRIDE * DH, 2 * CMP_HIDDEN), const),
            pl.BlockSpec((16, CMP_STRIDE * DH), const), pl.BlockSpec((16, CMP_STRIDE * DH), const),
            pl.BlockSpec((CMP_HIDDEN, LANES), const), pl.BlockSpec((CMP_HIDDEN, LANES), const)]


def _compress_prompt(kv, nb, t, cw):
    nch = t // CMP_STRIDE
    (w1k, pek, w2k), (w1v, pev, w2v) = cw
    return pl.pallas_call(
        _compress_prompt_kernel,
        grid=(nb,),
        in_specs=[pl.BlockSpec((t, LANES), lambda b: (b, 0)), pl.BlockSpec((t, LANES), lambda b: (b, 1))] + _wspecs(1),
        out_specs=pl.BlockSpec((1, 4, nch, LANES), lambda b: (b, 0, 0, 0)),
        out_shape=jax.ShapeDtypeStruct((nb, 4, nch, LANES), F32),
        scratch_shapes=[pltpu.VMEM((4, nch, CMP_STRIDE * DH), BF16)],
        compiler_params=_cparams(("arbitrary",)),
        name="compress_prompt",
    )(kv, kv, w1k, w1v, pek, pev, w2k, w2v)


def _compress_sample(cache_t, page_table, cw):
    nb, npages = page_table.shape
    page = cache_t.shape[-1]
    assert page == LANES and cache_t.shape[1:3] == (2, NSA_G * DH)
    r = npages * page
    nch = r // CMP_STRIDE
    (w1k, pek, w2k), (w1v, pev, w2v) = cw
    wspecs = [pl.BlockSpec(s.block_shape, lambda b, pt: (0, 0)) for s in _wspecs(2)]
    cpp = page // CMP_STRIDE
    perm = np.zeros((page, 4 * page), np.float32)
    for r2 in range(CMP_STRIDE // 2):
        for pg in range(2):
            for j in range(cpp):
                m = r2 * 2 * cpp + pg * cpp + j
                perm[m, pg * page + CMP_STRIDE * j + 2 * r2] = 1.0
                perm[m, 2 * page + pg * page + CMP_STRIDE * j + 2 * r2 + 1] = 1.0
    assert npages % 4 == 0 and page == CMP_STRIDE * cpp
    return pl.pallas_call(
        _compress_sample_kernel,
        grid_spec=pltpu.PrefetchScalarGridSpec(
            num_scalar_prefetch=1,
            grid=(nb,),
            in_specs=[pl.BlockSpec(memory_space=pl.ANY), pl.BlockSpec((page, 4 * page), lambda b, pt: (0, 0))] + wspecs,
            out_specs=pl.BlockSpec((1, 4, nch, LANES), lambda b, pt: (b, 0, 0, 0)),
            scratch_shapes=[pltpu.VMEM((2, npages, 2, NSA_G * DH, page), F32),
                            pltpu.VMEM((4, nch, CMP_STRIDE * DH), BF16), pltpu.SemaphoreType.DMA((2,))]),
        out_shape=jax.ShapeDtypeStruct((nb, 4, nch, LANES), F32),
        compiler_params=_cparams(("arbitrary",)),
        name="compress_sample",
    )(page_table, cache_t, jnp.asarray(perm, dtype=BF16), w1k, w1v, pek, pev, w2k, w2v)


def _softmax_update(s, m, l, acc, v, v_is_t=False):
    m_new = jnp.maximum(m, jnp.max(s, axis=1, keepdims=True))
    alpha = jnp.exp(m - m_new)
    p = jnp.exp(s - m_new)
    l = alpha * l + jnp.sum(p, axis=1, keepdims=True)
    dims = (((1,), (1,)), ((), ())) if v_is_t else (((1,), (0,)), ((), ()))
    acc = alpha * acc + lax.dot_general(p.astype(BF16), v, dims, preferred_element_type=F32)
    return m_new, l, acc


def _topk_mask(score, sidx, in_sel, lane0, nsel, k):
    rank = jnp.zeros(score.shape, I32)
    for sp in range(nsel):
        c = score[:, lane0 + sp:lane0 + sp + 1]
        beats = (c > score) | ((c == score) & (sp < sidx))
        rank = rank + beats.astype(I32)
    return in_sel & (rank < k)


def _nsa_prompt_kernel(qn_ref, qr_ref, ck_ref, cv_ref, ks_ref, vs_ref, kw_ref, vw_ref, ng_ref, cover_ref,
                       out_ref, sel_ref, s_ref, *, nsb, ksel):
    qt = pl.program_id(2)
    t0 = qt * TQ
    nck = ck_ref.shape[2]
    krow = lax.broadcasted_iota(I32, (TKEY, TQ), 0)
    qcol = lax.broadcasted_iota(I32, (TKEY, TQ), 1)
    tq = t0 + qcol
    nt = (((1,), (1,)), ((), ()))

    def col_reduce(x, op):
        return op(op(x.reshape(x.shape[0] // 8, 8, x.shape[1]), axis=0), axis=0, keepdims=True)

    ckb = ck_ref[0, 0].astype(BF16)
    cvt = cv_ref[0, 0].T.astype(BF16)
    cidx = lax.broadcasted_iota(I32, (nck, TQ), 0)
    cmask = (cidx * CMP_STRIDE + 2 * CMP_STRIDE - 1 <= t0 + lax.broadcasted_iota(I32, (nck, TQ), 1)) & (cidx < nck - 1)
    psum = jnp.zeros((nck, TQ), F32)
    o_cmp = []
    for h in range(NSA_HG):
        s = lax.dot_general(ckb, qn_ref[:, h * LANES:(h + 1) * LANES], nt, preferred_element_type=F32)
        s = jnp.where(cmask, s, -jnp.inf)
        m = col_reduce(s, jnp.max)
        m = jnp.where(m > -jnp.inf, m, 0.0)
        e = jnp.exp(s - m)
        d = col_reduce(e, jnp.sum)
        p = e * (1.0 / jnp.where(d > 0, d, 1.0))
        psum = psum + p
        o_cmp.append(jnp.dot(cvt, p.astype(BF16), preferred_element_type=F32)[0:DH])

    cover = cover_ref[...]
    p_hi = psum.astype(BF16)
    p_lo = (psum - p_hi.astype(F32)).astype(BF16)
    imp = jnp.dot(cover, p_hi, preferred_element_type=F32) + jnp.dot(cover, p_lo, preferred_element_type=F32)
    nsr = imp.shape[0]
    sidx = lax.broadcasted_iota(I32, (nsr, TQ), 0)
    tq_s = t0 + lax.broadcasted_iota(I32, (nsr, TQ), 1)
    cur = tq_s // SLC_BLOCK
    valid = sidx * SLC_BLOCK <= tq_s
    forced = (sidx == 0) | (sidx == cur) | (sidx == cur - 1)
    score = jnp.where(forced, SEL_FORCED, jnp.where(valid, imp, -jnp.inf))
    rank = jnp.zeros((nsr, TQ), I32)
    for sp in range(nsb):
        c = score[sp:sp + 1, :]
        rank = rank + ((c > score) | ((c == score) & (sp < sidx))).astype(I32)
    sel_ref[...] = jnp.where((rank < ksel) & (sidx < nsb), 0.0, NEG)

    ntiles = ks_ref.shape[0] // TKEY
    qs =[qr_ref[:, h * LANES:(h + 1) * LANES] for h in range(NSA_HG)]
    vrows = DH + 8

    def attend(k_ref, v_ref, j_first, n_tiles, bias_fn):
        unroll = n_tiles if isinstance(n_tiles, int) else NSA_UNROLL
        n_it = (n_tiles + unroll - 1) // unroll

        def tile_of(i, u):
            idx = i * unroll + u
            j = j_first + idx
            jc = jnp.clip(j, 0, ntiles - 1)
            return idx, j, jc, pl.multiple_of(jc * TKEY, TKEY)

        def scores(i, mx):
            mx = list(mx)
            for u in range(unroll):
                idx, j, jc, r0 = tile_of(i, u)
                kt = k_ref[pl.ds(r0, TKEY), :]
                bias = jnp.where(idx < n_tiles, bias_fn(j, jc), NEG)
                for h in range(NSA_HG):
                    s = lax.dot_general(kt, qs[h], nt, preferred_element_type=F32) + bias
                    s_ref[h, idx] = s
                    mx[h] = jnp.maximum(mx[h], jnp.max(s.reshape(TKEY // 8, 8, TQ), axis=0))
            return tuple(mx)

        mx = lax.fori_loop(0, n_it, scores, tuple(jnp.full((8, TQ), NEG, F32) for _ in range(NSA_HG)))
        ms = [jnp.max(m8, axis=0, keepdims=True) for m8 in mx]

        def values(i, accs):
            accs = list(accs)
            for u in range(unroll):
                idx, j, jc, r0 = tile_of(i, u)
                vt = v_ref[0:vrows, pl.ds(r0, TKEY)]
                for h in range(NSA_HG):
                    p = jnp.exp(s_ref[h, idx] - ms[h])
                    accs[h] = accs[h] + jnp.dot(vt, p.astype(BF16), preferred_element_type=F32)
            return tuple(accs)

        accs = lax.fori_loop(0, n_it, values, tuple(jnp.zeros((vrows, TQ), F32) for _ in range(NSA_HG)))
        return [acc[0:DH] / acc[DH:DH + 1] for acc in accs]

    dq = qcol - krow
    kpq = TQ // TKEY

    def slc_bias(j, jc):
        sel = jnp.where(krow < SLC_BLOCK, sel_ref[pl.ds(2 * jc, 1), :], sel_ref[pl.ds(2 * jc + 1, 1), :])
        return sel + jnp.where(dq >= jc * TKEY - t0, 0.0, NEG)

    def win_bias(j, jc):
        off = t0 - jc * TKEY
        return jnp.where((dq >= -off) & (dq < WINDOW - off) & (j >= 0), 0.0, NEG)

    nwin = WINDOW // TKEY
    gates = _sigmoid(ng_ref[...]).T
    o_slc = attend(ks_ref, vs_ref, 0, kpq * (qt + 1), slc_bias)
    o_win = attend(kw_ref, vw_ref, kpq * qt - nwin, nwin + kpq, win_bias)
    tot = []
    for h in range(NSA_HG):
        tot.append(gates[h:h + 1] * o_cmp[h] + gates[NSA_HG + h:NSA_HG + h + 1] * o_slc[h]
                   + gates[2 * NSA_HG + h:2 * NSA_HG + h + 1] * o_win[h])
    out_ref[...] = jnp.concatenate(tot, axis=0).T.astype(BF16)


def _cover_matrix(nc, nsb, lane0, rows, cols):
    c = np.arange(rows)[:, None]
    s = np.arange(cols)[None, :] - lane0
    cov = ((c < nc) & (s >= 0) & (s < nsb) & (c * CMP_STRIDE < (s + 1) * SLC_BLOCK)
           & (c * CMP_STRIDE + 2 * CMP_STRIDE > s * SLC_BLOCK))
    return jnp.asarray(cov.astype(np.float32), dtype=BF16)


def _nsa_prompt(qn, qr, cc, kvp, vt, ng, nb, t):
    nqt = t // TQ
    nch = t // CMP_STRIDE
    nsb = t // SLC_BLOCK
    nsr = 32
    assert nsb <= nsr and SLC_BLOCK * 2 == TKEY and t % TQ == 0
    nslots = NSA_UNROLL * (-(-max(t // TKEY, WINDOW // TKEY + TQ // TKEY) // NSA_UNROLL))
    cover = _cover_matrix(nch - 1, nsb, 0, nch, nsr).T
    kern = functools.partial(_nsa_prompt_kernel, nsb=nsb, ksel=min(SLC_TOPK, nsb))
    return pl.pallas_call(
        kern,
        grid=(nb, NSA_G, nqt),
        in_specs=[pl.BlockSpec((TQ, NSA_HG * LANES), lambda b, g, q: (b * nqt + q, g)),
                  pl.BlockSpec((TQ, NSA_HG * LANES), lambda b, g, q: (b * nqt + q, g)),
                  pl.BlockSpec((1, 1, nch, LANES), lambda b, g, q: (b, g, 0, 0)),
                  pl.BlockSpec((1, 1, nch, LANES), lambda b, g, q: (b, 2 + g, 0, 0)),
                  pl.BlockSpec((t, LANES), lambda b, g, q: (b, 2 * g)),
                  pl.BlockSpec((None, None, LANES, t), lambda b, g, q: (g, b, 0, 0)),
                  pl.BlockSpec((t, LANES), lambda b, g, q: (b, 4 + 2 * g)),
                  pl.BlockSpec((None, None, LANES, t), lambda b, g, q: (2 + g, b, 0, 0)),
                  pl.BlockSpec((TQ, LANES), lambda b, g, q: (b * nqt + q, g)),
                  pl.BlockSpec((nsr, nch), lambda b, g, q: (0, 0))],
        out_specs=pl.BlockSpec((TQ, 2 * LANES), lambda b, g, q: (b * nqt + q, g)),
        out_shape=jax.ShapeDtypeStruct((nb * t, NSA_G * 2 * LANES), BF16),
        scratch_shapes=[pltpu.VMEM((nsr, TQ), F32),
                        pltpu.VMEM((NSA_HG, nslots, TKEY, TQ), F32)],
        compiler_params=_cparams(("arbitrary", "arbitrary", "arbitrary")),
        name="nsa_prompt",
    )(qn, qr, cc, cc, kvp, vt, kvp, vt, ng, cover)


def _nsa_sample_kernel(pt_ref, qn_ref, qr_ref, cc_ref, new_ref, win_ref, ng_ref, cover_ref, expand_ref, cache_ref,
                       out_ref, buf_ref, s_ref, sem, *, past, tnew, nsb, ksel):
    b = pl.program_id(0)
    nb = pl.num_programs(0)
    npages = pt_ref.shape[1]
    page = past // npages
    nck = cc_ref.shape[2]
    rows = NSA_HEADS * SROWS
    half = rows // 2

    def copies(bb, slot):
        return [pltpu.make_async_copy(cache_ref.at[pt_ref[bb, p]], buf_ref.at[slot, p], sem.at[slot])
                for p in range(npages)]

    @pl.when(b == 0)
    def _():
        for c in copies(0, 0):
            c.start()

    @pl.when(b + 1 < nb)
    def _():
        for c in copies(b + 1, (b + 1) % 2):
            c.start()

    lane = lax.broadcasted_iota(I32, (rows, LANES), 1)
    row = lax.broadcasted_iota(I32, (rows, LANES), 0)
    tt = row % SROWS
    lo = lane < DH

    qn = jnp.concatenate([qn_ref[:, h * LANES:(h + 1) * LANES] for h in range(NSA_HEADS)], axis=0)
    qr32 = jnp.concatenate([qr_ref[:, h * LANES:(h + 1) * LANES] for h in range(NSA_HEADS)], axis=0).astype(F32)
    qr = jnp.where(row < half, qr32, pltpu.roll(qr32, DH, 1)).astype(BF16)

    nlt = nck // LANES
    o_cmp = []
    psums = []
    for g in range(NSA_G):
        ckb = cc_ref[0, g].astype(BF16)
        cvb = cc_ref[0, 2 + g].astype(BF16)
        qg = qn[g * half:(g + 1) * half]
        s = lax.dot_general(qg, ckb, (((1,), (1,)), ((), ())), preferred_element_type=F32)
        cl = lax.broadcasted_iota(I32, s.shape, 1)
        s = jnp.where(cl < nck - 1, s, -jnp.inf)
        m = jnp.max(s, axis=1, keepdims=True)
        e = jnp.exp(s - m)
        p = e / jnp.sum(e, axis=1, keepdims=True)
        o_cmp.append(jnp.dot(p.astype(BF16), cvb, preferred_element_type=F32))
        ps = p[0:SROWS]
        for h in range(1, NSA_HG):
            ps = ps + p[h * SROWS:(h + 1) * SROWS]
        psums.append(ps)
    psum = jnp.concatenate(psums, axis=0)
    cover = cover_ref[...]
    p_hi = psum.astype(BF16)
    p_lo = (psum - p_hi.astype(F32)).astype(BF16)
    imp = jnp.dot(p_hi, cover, preferred_element_type=F32) + jnp.dot(p_lo, cover, preferred_element_type=F32)
    nsl = imp.shape[1]
    sidx = lax.broadcasted_iota(I32, imp.shape, 1)
    tpos16 = past + lax.broadcasted_iota(I32, imp.shape, 0) % SROWS
    cur = tpos16 // SLC_BLOCK
    valid = sidx * SLC_BLOCK <= tpos16
    forced = (sidx == 0) | (sidx == cur) | (sidx == cur - 1)
    score = jnp.where(forced, SEL_FORCED, jnp.where(valid, imp, -jnp.inf))
    in_sel = sidx < nsb
    selected = _topk_mask(score, sidx, in_sel, 0, nsb, ksel)
    selb16 = jnp.where(selected, 0.0, NEG)
    selrows = jnp.concatenate([selb16[g * SROWS:(g + 1) * SROWS] for g in range(NSA_G) for _ in range(NSA_HG)],
                              axis=0).astype(BF16)

    slot = b % 2
    for c in copies(b, slot):
        c.wait()

    ppt = 4
    kt_rows = ppt * page
    nkt = npages // ppt
    unroll = 2 if nkt % 2 == 0 else 1
    nt = (((1,), (1,)), ((), ()))

    def lane_fold(x, op):
        out = x[:, 0:LANES]
        for i in range(1, kt_rows // LANES):
            out = op(out, x[:, i * LANES:(i + 1) * LANES])
        return out

    newk = new_ref[:, 2 * LANES:3 * LANES].astype(BF16)
    newv = new_ref[:, 3 * LANES:4 * LANES].astype(BF16)
    kk = lax.broadcasted_iota(I32, (rows, SROWS), 1)
    tt8 = lax.broadcasted_iota(I32, (rows, SROWS), 0) % SROWS
    new_ok = (kk <= tt8) & (kk < tnew)
    nb_new = past // SLC_BLOCK
    bias_new = selrows[:, nb_new:nb_new + 1].astype(F32)
    s_new = jnp.where(new_ok, lax.dot_general(qr, newk, nt, preferred_element_type=F32) + bias_new, NEG)

    def scores(i, mx):
        for u in range(unroll):
            j = i * unroll + u
            kt = jnp.concatenate([buf_ref[slot, j * ppt + a, 0] for a in range(ppt)], axis=1).astype(BF16)
            c0 = pl.multiple_of(j * kt_rows, kt_rows)
            s = (jnp.dot(qr, kt, preferred_element_type=F32)
                 + jnp.dot(selrows, expand_ref[:, pl.ds(c0, kt_rows)], preferred_element_type=F32))
            s_ref[j] = s
            mx = jnp.maximum(mx, lane_fold(s, jnp.maximum))
        return mx

    mx = lax.fori_loop(0, nkt // unroll, scores, jnp.full((rows, LANES), NEG, F32))
    m = jnp.maximum(jnp.max(mx, axis=1, keepdims=True), jnp.max(s_new, axis=1, keepdims=True))

    def values(i, st):
        acc, ls = st
        for u in range(unroll):
            j = i * unroll + u
            vt = jnp.concatenate([buf_ref[slot, j * ppt + a, 1] for a in range(ppt)], axis=1).astype(BF16)
            p = jnp.exp(s_ref[j] - m)
            acc = acc + lax.dot_general(p.astype(BF16), vt, nt, preferred_element_type=F32)
            ls = ls + lane_fold(p, jnp.add)
        return acc, ls

    acc, ls = lax.fori_loop(0, nkt // unroll, values,
                            (jnp.zeros((rows, LANES), F32), jnp.zeros((rows, LANES), F32)))
    p_new = jnp.exp(s_new - m)
    acc = acc + jnp.dot(p_new.astype(BF16), newv, preferred_element_type=F32)
    o_slc = acc / (jnp.sum(ls, axis=1, keepdims=True) + jnp.sum(p_new, axis=1, keepdims=True))
    zero_state = (jnp.full((rows, 1), NEG, F32), jnp.zeros((rows, 1), F32), jnp.zeros((rows, LANES), F32))

    wb = win_ref.shape[3]
    wk = win_ref[0, 0].astype(BF16)
    wv = win_ref[0, 1].astype(BF16)
    wi = lax.broadcasted_iota(I32, (rows, wb), 1)
    wt = lax.broadcasted_iota(I32, (rows, wb), 0) % SROWS
    w_ok = (wt + wb - wi < WINDOW) if wb >= WINDOW else (wi >= 0)
    s = jnp.dot(qr, wk, preferred_element_type=F32)
    s = jnp.where(w_ok, s, NEG)
    st = _softmax_update(s, *zero_state, wv, v_is_t=True)
    nwk = new_ref[:, 4 * LANES:5 * LANES].astype(BF16)
    nwv = new_ref[:, 5 * LANES:6 * LANES].astype(BF16)
    s = lax.dot_general(qr, nwk, (((1,), (1,)), ((), ())), preferred_element_type=F32)
    s = jnp.where(new_ok, s, NEG)
    m, l, acc = _softmax_update(s, *st, nwv)
    o_win = acc / l

    gates = _sigmoid(ng_ref[...])
    outs = []
    for hh in range(NSA_HEADS):
        g, h = divmod(hh, NSA_HG)
        gl = gates[:, g * LANES:(g + 1) * LANES]
        g0 = gl[:, h:h + 1]
        g1 = gl[:, NSA_HG + h:NSA_HG + h + 1]
        g2 = gl[:, 2 * NSA_HG + h:2 * NSA_HG + h + 1]
        r0 = hh * SROWS
        oc = o_cmp[g][h * SROWS:(h + 1) * SROWS]
        outs.append(g0 * oc + g1 * o_slc[r0:r0 + SROWS] + g2 * o_win[r0:r0 + SROWS])
    lo8 = lax.broadcasted_iota(I32, (SROWS, LANES), 1) < DH
    out_ref[:, 0 * LANES:1 * LANES] = jnp.where(lo8, outs[0], pltpu.roll(outs[1], DH, 1)).astype(BF16)
    out_ref[:, 1 * LANES:2 * LANES] = jnp.where(lo8, outs[2], pltpu.roll(outs[3], DH, 1)).astype(BF16)
    out_ref[:, 2 * LANES:3 * LANES] = jnp.where(lo8, pltpu.roll(outs[4], DH, 1), outs[5]).astype(BF16)
    out_ref[:, 3 * LANES:4 * LANES] = jnp.where(lo8, pltpu.roll(outs[6], DH, 1), outs[7]).astype(BF16)


def _nsa_sample(qn, qr, cc, kv, win_t, ng, slc_t, page_table, past, tnew, row0):
    nb, npages = page_table.shape
    page = slc_t.shape[-1]
    assert page * npages == past and npages % 4 == 0 and page % SLC_BLOCK == 0
    nck = past // CMP_STRIDE
    nsb = -(-(past + tnew) // SLC_BLOCK)
    nsl = -(-nsb // LANES) * LANES
    assert (nck - 2) * CMP_STRIDE + 2 * CMP_STRIDE - 1 <= past and (past + tnew) // CMP_STRIDE == nck
    assert tnew <= SROWS and past // SLC_BLOCK == nsb - 1
    cover = _cover_matrix(nck - 1, nsb, 0, nck, nsl)
    expand = jnp.asarray((np.arange(nsl)[:, None] == np.arange(past)[None, :] // SLC_BLOCK).astype(np.float32), dtype=BF16)
    wb = win_t.shape[3]
    kern = functools.partial(_nsa_sample_kernel, past=past, tnew=tnew, nsb=nsb, ksel=min(SLC_TOPK, nsb))
    rb = row0 // SROWS
    return pl.pallas_call(
        kern,
        grid_spec=pltpu.PrefetchScalarGridSpec(
            num_scalar_prefetch=1,
            grid=(nb,),
            in_specs=[pl.BlockSpec((SROWS, NSA_HEADS * LANES), lambda b, pt: (rb + b, 0)),
                      pl.BlockSpec((SROWS, NSA_HEADS * LANES), lambda b, pt: (rb + b, 0)),
                      pl.BlockSpec((1, 4, nck, LANES), lambda b, pt: (b, 0, 0, 0)),
                      pl.BlockSpec((SROWS, 6 * LANES), lambda b, pt: (rb + b, 0)),
                      pl.BlockSpec((1, 2, NSA_G * DH, wb), lambda b, pt: (b, 0, 0, 0)),
                      pl.BlockSpec((SROWS, 2 * LANES), lambda b, pt: (rb + b, 0)),
                      pl.BlockSpec((nck, nsl), lambda b, pt: (0, 0)),
                      pl.BlockSpec((nsl, past), lambda b, pt: (0, 0)),
                      pl.BlockSpec(memory_space=pl.ANY)],
            out_specs=pl.BlockSpec((SROWS, NSA_HEADS * DH), lambda b, pt: (b, 0)),
            scratch_shapes=[pltpu.VMEM((2, npages, 2, NSA_G * DH, page), F32),
                            pltpu.VMEM((npages // 4, NSA_HEADS * SROWS, 4 * page), F32),
                            pltpu.SemaphoreType.DMA((2,))]),
        out_shape=jax.ShapeDtypeStruct((nb * SROWS, NSA_HEADS * DH), BF16),
        compiler_params=_cparams(("arbitrary",)),
        name="nsa_sample",
    )(page_table, qn, qr, cc, kv, win_t, ng, cover, expand, slc_t)


def _hgrn_kernel(hq_ref, hf_ref, hi_ref, hg_ref, lb_ref, nw_ref, s0_ref, o_ref, sout_ref, *, chunk, valid, nchunks):
    heads = lb_ref.shape[0]
    nw = nw_ref[...]
    row = lax.broadcasted_iota(I32, (chunk, HG_D), 0)
    rvalid = row < valid

    def head_chunk(hd, r0, st):
        cols = slice(hd * HG_D, (hd + 1) * HG_D)
        lb = lb_ref[hd]
        hq = hq_ref[pl.ds(r0, chunk), cols]
        hf = hf_ref[pl.ds(r0, chunk), cols]
        iv = hi_ref[pl.ds(r0, chunk), cols]
        hg = hg_ref[pl.ds(r0, chunk), cols]
        q = hq * _sigmoid(hq)
        f = lb + (1.0 - lb) * _sigmoid(hf)
        logf = jnp.where(rvalid, jnp.log(f), 0.0)
        k = jnp.where(rvalid, 1.0 - f, 0.0)
        b = logf
        sh = 1
        while sh < chunk:
            b = b + jnp.where(row >= sh, pltpu.roll(b, sh, 0), 0.0)
            sh *= 2
        o_in = jnp.zeros((chunk, HG_D), F32)
        for s in range(min(valid, 8)):
            e = jnp.exp(jnp.where(row >= s, b - b[s:s + 1, :], -jnp.inf))
            a = jnp.sum(q * e * k[s:s + 1, :], axis=1, keepdims=True)
            o_in = o_in + a * iv[s:s + 1, :]
        if valid > 8:
            b_hi, q_hi, row_hi = b[8:], q[8:], row[8:]
            o_hi = jnp.zeros((chunk - 8, HG_D), F32)
            for s in range(8, valid):
                e = jnp.exp(jnp.where(row_hi >= s, b_hi - b[s:s + 1, :], -jnp.inf))
                a = jnp.sum(q_hi * e * k[s:s + 1, :], axis=1, keepdims=True)
                o_hi = o_hi + a * iv[s:s + 1, :]
            o_in = jnp.concatenate([o_in[:8], o_in[8:] + o_hi], axis=0)
        qd = (q * jnp.exp(b)).astype(BF16)
        o = o_in + lax.dot_general(qd, st.astype(BF16), (((1,), (1,)), ((), ())), preferred_element_type=F32)
        bl = b[chunk - 1:chunk, :]
        kd = (k * jnp.exp(bl - b)).astype(BF16)
        u = lax.dot_general(iv.astype(BF16), kd, (((0,), (0,)), ((), ())), preferred_element_type=F32)
        st = st * jnp.exp(bl) + u
        ms = jnp.mean(o * o, axis=1, keepdims=True)
        y = o * lax.rsqrt(ms + RMS_EPS) * nw * (hg * _sigmoid(hg))
        o_ref[pl.ds(r0, chunk), cols] = y.astype(BF16)
        return st

    def body(i, sts):
        r0 = pl.multiple_of(i * chunk, chunk)
        return tuple(head_chunk(hd, r0, sts[hd]) for hd in range(heads))

    sts = lax.fori_loop(0, nchunks, body, tuple(s0_ref[0, hd].T for hd in range(heads)),
                        unroll=min(max(1, 8 // heads), nchunks))
    for hd in range(heads):
        sout_ref[0, hd] = sts[hd].T


def _hgrn(hh, lb, nw, s0, nseq, rows_per_seq, row0, chunk, valid, heads_per_step):
    rb = row0 // rows_per_seq
    hps = heads_per_step
    ngrp = HG_HEADS // hps
    kern = functools.partial(_hgrn_kernel, chunk=chunk, valid=valid, nchunks=rows_per_seq // chunk)
    blk = lambda k: pl.BlockSpec((rows_per_seq, hps * HG_D), lambda b, h, k=k: (rb + b, k * ngrp + h))
    return pl.pallas_call(
        kern,
        grid=(nseq, ngrp),
        in_specs=[blk(0), blk(1), blk(2), blk(3),
                  pl.BlockSpec((hps, 1, HG_D), lambda b, h: (h, 0, 0)),
                  pl.BlockSpec((1, HG_D), lambda b, h: (0, 0)),
                  pl.BlockSpec((1, hps, HG_D, HG_D), lambda b, h: (b, h, 0, 0))],
        out_specs=[pl.BlockSpec((rows_per_seq, hps * HG_D), lambda b, h: (b, h)),
                   pl.BlockSpec((1, hps, HG_D, HG_D), lambda b, h: (b, h, 0, 0))],
        out_shape=[jax.ShapeDtypeStruct((nseq * rows_per_seq, HG_HEADS * HG_D), BF16),
                   jax.ShapeDtypeStruct((nseq, HG_HEADS, HG_D, HG_D), F32)],
        compiler_params=_cparams(("arbitrary", "arbitrary")),
        name="hgrn",
    )(hh, hh, hh, hh, lb, nw, s0)


def _layernorm(y, g, b):
    mu = jnp.mean(y, axis=1, keepdims=True)
    yc = y - mu
    var = jnp.mean(yc * yc, axis=1, keepdims=True)
    return yc * lax.rsqrt(var + LN_EPS) * g + b


def _finish_kernel(xp_ref, xs_ref, ap_ref, as_ref, hp_ref, hs_ref, gg_ref, wpa_ref, wpb_ref, wo_ref, g1_ref, b1_ref,
                   wr_ref, br_ref, ltri_ref, h_ref, topi_ref, topw_ref, pos_ref, cnt_ref, carry_ref, *, n_prompt_tiles):
    i = pl.program_id(0)

    @pl.when(i == 0)
    def _():
        carry_ref[...] = jnp.zeros_like(carry_ref)

    a = jnp.dot(_pick_rows(ap_ref, as_ref, n_prompt_tiles), wpa_ref[...], preferred_element_type=F32)
    bb = jnp.dot(_pick_rows(hp_ref, hs_ref, n_prompt_tiles), wpb_ref[...], preferred_element_type=F32)
    m = gg_ref[:, :D_MODEL] * a + gg_ref[:, D_MODEL:] * bb
    mix = jnp.dot(m.astype(BF16), wo_ref[...], preferred_element_type=F32)
    h = _layernorm(DN_ALPHA * _pick_rows(xp_ref, xs_ref, n_prompt_tiles) + mix, g1_ref[...], b1_ref[...])
    h_ref[...] = h

    h1 = h.astype(BF16)
    h2 = (h - h1.astype(F32)).astype(BF16)
    w = wr_ref[...]
    w1 = w.astype(BF16)
    w2 = (w - w1.astype(F32)).astype(BF16)
    lg = (jnp.dot(h1, w1, preferred_element_type=F32) + jnp.dot(h1, w2, preferred_element_type=F32)
          + jnp.dot(h2, w1, preferred_element_type=F32)) + br_ref[...]
    tm = lg.shape[0]
    lane = lax.broadcasted_iota(I32, (tm, LANES), 1)
    lg = jnp.where(lane < N_EXPERTS, lg, -jnp.inf)
    vals, idxs = [], []
    for _ in range(TOP_K):
        mx = jnp.max(lg, axis=1, keepdims=True)
        ix = jnp.min(jnp.where(lg == mx, lane, LANES), axis=1, keepdims=True)
        vals.append(mx)
        idxs.append(ix)
        lg = jnp.where(lane == ix, -jnp.inf, lg)
    es = [jnp.exp(v - vals[0]) for v in vals]
    den = es[0] + es[1] + es[2] + es[3]
    topi = jnp.zeros((tm, LANES), I32)
    topw = jnp.zeros((tm, LANES), F32)
    onehot = jnp.zeros((tm, LANES), F32)
    for j in range(TOP_K):
        topi = jnp.where(lane == j, idxs[j], topi)
        topw = jnp.where(lane == j, es[j] / den, topw)
        onehot = onehot + (lane == idxs[j]).astype(F32)
    topi_ref[...] = topi
    topw_ref[...] = topw
    prefix = jnp.dot(ltri_ref[...], onehot.astype(BF16), preferred_element_type=F32) + carry_ref[0:1, :]
    pos = jnp.zeros((tm, LANES), I32)
    for j in range(TOP_K):
        pj = jnp.sum(jnp.where(lane == idxs[j], prefix, 0.0), axis=1, keepdims=True)
        pos = jnp.where(lane == j, pj.astype(I32), pos)
    pos_ref[...] = pos
    carry_ref[...] = carry_ref[...] + jnp.sum(onehot, axis=0, keepdims=True)
    cnt_ref[...] = carry_ref[...]


def _finish(xp, xs, nsa_p, nsa_s, hg_p, hg_s, gg, wpa, wpb, wo, g1, b1, wr, br):
    npt = xp.shape[0] // TM
    n = xp.shape[0] + xs.shape[0]
    row = lambda i: (i, 0)
    const = lambda i: (0, 0)
    ltri = jnp.asarray(np.tril(np.ones((TM, TM), np.float32), -1), dtype=BF16)
    outs = pl.pallas_call(
        functools.partial(_finish_kernel, n_prompt_tiles=npt),
        grid=(n // TM,),
        in_specs=_row_specs(npt) + _row_specs(npt, 512) + _row_specs(npt, 512) + [
                  pl.BlockSpec((TM, 2 * D_MODEL), row),
                  pl.BlockSpec((512, D_MODEL), const), pl.BlockSpec((512, D_MODEL), const),
                  pl.BlockSpec((D_MODEL, D_MODEL), const),
                  pl.BlockSpec((1, D_MODEL), const), pl.BlockSpec((1, D_MODEL), const),
                  pl.BlockSpec((D_MODEL, LANES), const), pl.BlockSpec((1, LANES), const),
                  pl.BlockSpec((TM, TM), const)],
        out_specs=[pl.BlockSpec((TM, D_MODEL), row), pl.BlockSpec((TM, LANES), row), pl.BlockSpec((TM, LANES), row),
                   pl.BlockSpec((TM, LANES), row), pl.BlockSpec((8, LANES), const)],
        out_shape=[jax.ShapeDtypeStruct((n, D_MODEL), F32), jax.ShapeDtypeStruct((n, LANES), I32),
                   jax.ShapeDtypeStruct((n, LANES), F32), jax.ShapeDtypeStruct((n, LANES), I32),
                   jax.ShapeDtypeStruct((8, LANES), F32)],
        scratch_shapes=[pltpu.VMEM((8, LANES), F32)],
        compiler_params=_cparams(("arbitrary",)),
        name="finish",
    )(xp, xs, nsa_p, nsa_s, hg_p, hg_s, gg, wpa, wpb, wo, g1, b1, wr, br, ltri)
    return outs


def _dispatch_kernel(padrow_ref, nused_ref, dest_ref, h_ref, xs_ref, zero_ref, sem):
    @pl.when(pl.program_id(0) == 0)
    def _():
        zero_ref[...] = jnp.zeros_like(zero_ref)
        fills = [pltpu.make_async_copy(zero_ref, xs_ref.at[pl.ds(pl.multiple_of(padrow_ref[e], 8), TE + 8), :], sem)
                 for e in range(N_EXPERTS)]
        for c in fills:
            c.start()
        for c in fills:
            c.wait()
        ntiles = (xs_ref.shape[0] - TE - 8) // TE

        def fill_tile(k, c):
            cp = pltpu.make_async_copy(zero_ref, xs_ref.at[pl.ds(pl.multiple_of(k * TE, TE), TE + 8), :], sem)
            cp.start()
            cp.wait()
            return c

        lax.fori_loop(nused_ref[0], ntiles + 1, fill_tile, 0)

    def copy(r, j):
        d = dest_ref[0, 0, r * TOP_K + j]
        return pltpu.make_async_copy(h_ref.at[pl.ds(r, 1), :], xs_ref.at[pl.ds(d, 1), :], sem)

    def start(r, c):
        for j in range(TOP_K):
            copy(r, j).start(priority=j % 2)
        return c

    def wait(r, c):
        for j in range(TOP_K):
            copy(r, j).wait()
        return c

    lax.fori_loop(0, TD, start, 0)
    lax.fori_loop(0, TD, wait, 0)


def _dispatch(padrow, nused, dest3, h, nrows):
    n = h.shape[0]
    return pl.pallas_call(
        _dispatch_kernel,
        grid_spec=pltpu.PrefetchScalarGridSpec(
            num_scalar_prefetch=2,
            grid=(n // TD,),
            in_specs=[pl.BlockSpec((1, 1, TD * TOP_K), lambda i, pr, nu: (i, 0, 0), memory_space=pltpu.SMEM),
                      pl.BlockSpec((TD, D_MODEL), lambda i, pr, nu: (i, 0))],
            out_specs=pl.BlockSpec(memory_space=pl.ANY),
            scratch_shapes=[pltpu.VMEM((TE + 8, D_MODEL), F32), pltpu.SemaphoreType.DMA(())]),
        out_shape=jax.ShapeDtypeStruct((nrows + TE + 8, D_MODEL), F32),
        compiler_params=_cparams(("arbitrary",)),
        name="dispatch",
    )(padrow, nused, dest3, h)


def _experts_kernel(bexp_ref, nused_ref, x_ref, wgu_ref, bgu_ref, wdn_ref, bdn_ref, y_ref, wgu_b, wdn_b):
    i = pl.program_id(0)
    cw = 256
    used = i < nused_ref[0]

    @pl.when(used & ((i == 0) | (bexp_ref[i] != bexp_ref[jnp.maximum(i - 1, 0)])))
    def _():
        for c in range(2 * D_FF // cw):
            wgu_b[:, c * cw:(c + 1) * cw] = wgu_ref[0, :, c * cw:(c + 1) * cw].astype(BF16)
        for c in range(D_MODEL // cw):
            wdn_b[:, c * cw:(c + 1) * cw] = wdn_ref[0, :, c * cw:(c + 1) * cw].astype(BF16)

    @pl.when(used)
    def _():
        xb = x_ref[...].astype(BF16)
        acc = jnp.zeros((TE, D_MODEL), F32)
        for c in range(D_FF // cw):
            gt = jnp.dot(xb, wgu_b[:, c * cw:(c + 1) * cw], preferred_element_type=F32) + bgu_ref[0, :, c * cw:(c + 1) * cw]
            up = (jnp.dot(xb, wgu_b[:, D_FF + c * cw:D_FF + (c + 1) * cw], preferred_element_type=F32)
                  + bgu_ref[0, :, D_FF + c * cw:D_FF + (c + 1) * cw])
            gt = jnp.minimum(gt, SWIGLU_LIMIT)
            up = jnp.clip(up, -SWIGLU_LIMIT, SWIGLU_LIMIT)
            act = (up + 1.0) * (gt * _sigmoid(SWIGLU_ALPHA * gt))
            acc = acc + jnp.dot(act.astype(BF16), wdn_b[c * cw:(c + 1) * cw, :], preferred_element_type=F32)
        y_ref[...] = acc + bdn_ref[0]

    @pl.when(i >= nused_ref[0])
    def _():
        y_ref[...] = jnp.zeros_like(y_ref)


def _experts(bexp, nused, xs, wgu, bgu, wdn, bdn):
    nt = bexp.shape[0]
    return pl.pallas_call(
        _experts_kernel,
        grid_spec=pltpu.PrefetchScalarGridSpec(
            num_scalar_prefetch=2,
            grid=(nt,),
            in_specs=[pl.BlockSpec((TE, D_MODEL), lambda i, be, nu: (jnp.minimum(i, nu[0] - 1), 0)),
                      pl.BlockSpec((1, D_MODEL, 2 * D_FF), lambda i, be, nu: (be[i], 0, 0)),
                      pl.BlockSpec((1, 1, 2 * D_FF), lambda i, be, nu: (be[i], 0, 0)),
                      pl.BlockSpec((1, D_FF, D_MODEL), lambda i, be, nu: (be[i], 0, 0)),
                      pl.BlockSpec((1, 1, D_MODEL), lambda i, be, nu: (be[i], 0, 0))],
            out_specs=pl.BlockSpec((TE, D_MODEL), lambda i, be, nu: (i, 0)),
            scratch_shapes=[pltpu.VMEM((D_MODEL, 2 * D_FF), BF16), pltpu.VMEM((D_FF, D_MODEL), BF16)]),
        out_shape=jax.ShapeDtypeStruct((nt * TE, D_MODEL), F32),
        compiler_params=_cparams(("arbitrary",)),
        name="experts",
    )(bexp, nused, xs, wgu, bgu, wdn, bdn)


def _combine_kernel(dest_ref, dnext_ref, h_ref, topw_ref, g2_ref, b2_ref, ys_ref, outp_ref, outs_ref, buf_ref, sem,
                    *, n_prompt_tiles):
    i = pl.program_id(0)
    n = pl.num_programs(0)

    def copy(d_ref, slot, r, j):
        d = d_ref[0, 0, r * TOP_K + j]
        return pltpu.make_async_copy(ys_ref.at[pl.ds(d, 1), :], buf_ref.at[slot, j, pl.ds(r, 1), :], sem.at[slot])

    def start_all(d_ref, slot):
        def body(r, c):
            for j in range(TOP_K):
                copy(d_ref, slot, r, j).start(priority=j % 2)
            return c
        lax.fori_loop(0, TD, body, 0)

    @pl.when(i == 0)
    def _():
        start_all(dest_ref, 0)

    @pl.when(i + 1 < n)
    def _():
        start_all(dnext_ref, (i + 1) % 2)

    slot = i % 2

    def wait(r, c):
        for j in range(TOP_K):
            copy(dest_ref, slot, r, j).wait()
        return c

    lax.fori_loop(0, TD, wait, 0)
    tw = topw_ref[...]
    moe = tw[:, 0:1] * buf_ref[slot, 0]
    for j in range(1, TOP_K):
        moe = moe + tw[:, j:j + 1] * buf_ref[slot, j]
    y = _layernorm(DN_ALPHA * h_ref[...] + moe, g2_ref[...], b2_ref[...])

    @pl.when(i < n_prompt_tiles)
    def _():
        outp_ref[...] = y

    @pl.when(i >= n_prompt_tiles)
    def _():
        outs_ref[...] = y


def _combine(dest3, h, topw, g2, b2, ys, n_p):
    n = h.shape[0]
    last = n // TD - 1
    npt = n_p // TD
    return pl.pallas_call(
        functools.partial(_combine_kernel, n_prompt_tiles=npt),
        grid=(n // TD,),
        in_specs=[pl.BlockSpec((1, 1, TD * TOP_K), lambda i: (i, 0, 0), memory_space=pltpu.SMEM),
                  pl.BlockSpec((1, 1, TD * TOP_K), lambda i: (jnp.minimum(i + 1, last), 0, 0), memory_space=pltpu.SMEM),
                  pl.BlockSpec((TD, D_MODEL), lambda i: (i, 0)),
                  pl.BlockSpec((TD, LANES), lambda i: (i, 0)),
                  pl.BlockSpec((1, D_MODEL), lambda i: (0, 0)), pl.BlockSpec((1, D_MODEL), lambda i: (0, 0)),
                  pl.BlockSpec(memory_space=pl.ANY)],
        out_specs=[pl.BlockSpec((TD, D_MODEL), lambda i: (jnp.minimum(i, npt - 1), 0)),
                   pl.BlockSpec((TD, D_MODEL), lambda i: (jnp.maximum(i - npt, 0), 0))],
        out_shape=[jax.ShapeDtypeStruct((n_p, D_MODEL), F32), jax.ShapeDtypeStruct((n - n_p, D_MODEL), F32)],
        scratch_shapes=[pltpu.VMEM((2, TOP_K, TD, D_MODEL), F32), pltpu.SemaphoreType.DMA((2,))],
        compiler_params=_cparams(("arbitrary",)),
        name="combine",
    )(dest3, dest3, h, topw, g2, b2, ys)


def kernel(x_prompt, x_sample, cache_cmp_kv, cache_slc_kv, cache_win_kv, state_hgrn, page_table, w_in, pe_ck, w_ck1,
           w_ck2, pe_cv, w_cv1, w_cv2, hg_lb_logits, hg_norm_w, w_pa, w_pb, w_o, ln1_g, ln1_b, w_router, b_router,
           w_gu, b_gu, w_dn, b_dn, ln2_g, ln2_b):
    nb, t, d = x_prompt.shape
    ndb, tnew, _ = x_sample.shape
    depth, npool, page = cache_cmp_kv.shape[:3]
    past = page_table.shape[1] * page
    assert depth == 1 and d == D_MODEL and t % TM == 0 and (ndb * SROWS) % TM == 0 and tnew <= SROWS
    n_p = nb * t
    n_s = ndb * SROWS
    n = n_p + n_s

    xp = x_prompt.reshape(n_p, d)
    xs_pad = jnp.pad(x_sample, ((0, 0), (0, SROWS - tnew), (0, 0))).reshape(n_s, d)

    w_all = _repack_w_in(w_in[0])
    tables = _rope_tables(t, past)
    qn, qr, kv, kvp, ng, hh, gg = _project(xp, xs_pad, w_all, tables, t)

    cw = (_compress_weights(pe_ck[0], w_ck1[0], w_ck2[0]), _compress_weights(pe_cv[0], w_cv1[0], w_cv2[0]))
    cc_p = _compress_prompt(kv, nb, t, cw)

    def rows_minor(c):
        return jnp.transpose(c, (0, 2, 3, 4, 1)).reshape(c.shape[0], 2, NSA_G * DH, c.shape[1])

    cc_s = _compress_sample(rows_minor(cache_cmp_kv[0]), page_table, cw)
    vt = jnp.stack([kvp[:n_p, (2 * i + 1) * LANES:(2 * i + 2) * LANES].reshape(nb, t, LANES).transpose(0, 2, 1)
                    for i in range(4)], axis=0)
    nsa_p = _nsa_prompt(qn, qr, cc_p, kvp, vt, ng, nb, t)
    nsa_s = _nsa_sample(qn, qr, cc_s, kv, rows_minor(cache_win_kv[0]), ng, rows_minor(cache_slc_kv[0]),
                        page_table, past, tnew, n_p)

    lb_all = jnp.cumsum(jax.nn.softmax(hg_lb_logits.astype(F32), axis=0), axis=0)
    lb = lb_all[0].reshape(HG_HEADS, 1, HG_D)
    nw = hg_norm_w[0].reshape(1, HG_D)
    hg_p, st_p = _hgrn(hh, lb, nw, jnp.zeros((nb, HG_HEADS, HG_D, HG_D), F32), nb, t, 0, HG_CHUNK, HG_CHUNK, 2)
    hg_s, st_s = _hgrn(hh, lb, nw, state_hgrn[0], ndb, SROWS, n_p, SROWS, tnew, HG_HEADS)

    wr = jnp.pad(w_router[0], ((0, 0), (0, LANES - N_EXPERTS)))
    br = jnp.pad(b_router[0], (0, LANES - N_EXPERTS)).reshape(1, LANES)
    h, topi, topw, pos, cnt = _finish(xp, xs_pad, nsa_p, nsa_s, hg_p, hg_s, gg, w_pa[0].astype(BF16),
                                      w_pb[0].astype(BF16), w_o[0].astype(BF16), ln1_g[0].reshape(1, d),
                                      ln1_b[0].reshape(1, d), wr, br)

    counts = cnt[0, :N_EXPERTS].astype(I32)
    pcounts = (counts + TE - 1) // TE * TE
    pend = jnp.cumsum(pcounts)
    pstart = pend - pcounts
    nt = (n * TOP_K) // TE + N_EXPERTS
    bexp = jnp.minimum(jnp.sum(jnp.arange(nt, dtype=I32)[:, None] * TE >= pend[None, :], axis=1), N_EXPERTS - 1).astype(I32)
    nused = (pend[-1:] // TE).astype(I32)
    dest = pstart[topi[:, :TOP_K]] + pos[:, :TOP_K]
    dest3 = dest.reshape(n // TD, 1, TD * TOP_K)

    xs = _dispatch(((pstart + counts) // 8 * 8).astype(I32), nused, dest3, h, nt * TE)
    ys = _experts(bexp, nused, xs, w_gu[0], b_gu[0].reshape(N_EXPERTS, 1, 2 * D_FF),
                  w_dn[0], b_dn[0].reshape(N_EXPERTS, 1, D_MODEL))
    y_p, y_s = _combine(dest3, h, topw, ln2_g[0].reshape(1, d), ln2_b[0].reshape(1, d), ys, n_p)

    y_p = y_p.reshape(nb, t, d)
    y_s = y_s.reshape(ndb, SROWS, d)[:, :tnew]
    kv_p = kv[:n_p].reshape(nb, t, 3, 2, NSA_G, DH)
    kv_s = kv[n_p:].reshape(ndb, SROWS, 3, 2, NSA_G, DH)[:, :tnew]
    keep = min(WINDOW, t)
    win_s = jnp.concatenate([cache_win_kv[0], kv_s[:, :, 2]], axis=1)[:, tnew:]
    return (y_p, y_s, kv_p[:, :, 0][None], kv_s[:, :, 0][None], kv_p[:, :, 1][None], kv_s[:, :, 1][None],
            kv_p[:, t - keep:, 2][None], win_s[None], st_p[None], st_s[None])
```
